```python
import math
import jax
import jax.numpy as jnp
from jax import lax
import numpy as np

D_MODEL = 1024
BATCH = 16
SEQ = 2048
DEPTH = 1

N_META = 16
GDN_HEADS = 4
GDN_HEAD_DIM = 128
GDN_WIDTH = GDN_HEADS * GDN_HEAD_DIM
GDN_CONV = 4
CHUNK = 64
SC_WIDTH = D_MODEL - GDN_WIDTH
SC_GROUPS = 8
SC_CONV = 3
MIX_WIDTH = GDN_WIDTH + SC_WIDTH
D_FF = -(-8 * D_MODEL // (3 * 256)) * 256
IN_SPLITS = (GDN_WIDTH, GDN_WIDTH, GDN_WIDTH, GDN_WIDTH, GDN_HEADS, GDN_HEADS, SC_WIDTH, SC_WIDTH, SC_WIDTH)
IN_WIDTH = sum(IN_SPLITS)
EPS = 1e-6

kernel_name = 'hymba_gdn_shortconv_block'


def rms_norm(x, w):
    xf = x.astype(jnp.float32)
    y = xf * lax.rsqrt(jnp.mean(xf * xf, axis=-1, keepdims=True) + EPS)
    return (y * w.astype(jnp.float32)).astype(x.dtype)


def l2_normalize(x):
    xf = x.astype(jnp.float32)
    return xf * lax.rsqrt(jnp.sum(xf * xf, axis=-1, keepdims=True) + EPS)


def causal_depthwise_conv(x, w):
    k_width = w.shape[0]
    seq_len = x.shape[1]
    xp = jnp.pad(x, ((0, 0), (k_width - 1, 0), (0, 0)))
    return sum(xp[:, i:i + seq_len] * w[i].astype(x.dtype) for i in range(k_width))


def chunked_gated_delta_rule(q, k, v, g, beta):
    b, seq_len, n_heads, dk = q.shape
    dv = v.shape[-1]
    pad = (-seq_len) % CHUNK
    f32 = jnp.float32

    def to_chunks(t):
        t = jnp.pad(t.astype(f32), ((0, 0), (pad, 0)) + ((0, 0),) * (t.ndim - 2))
        n = t.shape[1] // CHUNK
        t = t.reshape((b, n, CHUNK) + t.shape[2:])
        return jnp.moveaxis(t, 3, 1)

    qc, kc, vc, g_raw, bc = (to_chunks(t) for t in (q, k, v, g, beta))
    gc = jnp.cumsum(g_raw, axis=-1)
    idx = jnp.arange(CHUNK)
    incl = idx[:, None] >= idx[None, :]
    strict = idx[:, None] > idx[None, :]
    diff = gc[..., :, None] - gc[..., None, :]
    decay = jnp.where(incl, jnp.exp(jnp.where(incl, diff, 0.0)), 0.0)
    kb = kc * bc[..., None]
    a = jnp.where(strict, jnp.einsum('bhnid,bhnjd->bhnij', kb, kc) * decay, 0.0)
    eye = jnp.eye(CHUNK, dtype=f32)
    t_inv = lax.linalg.triangular_solve(a + eye, jnp.broadcast_to(eye, a.shape), left_side=True, lower=True)
    u = jnp.einsum('bhnij,bhnje->bhnie', t_inv, vc * bc[..., None])
    w = jnp.einsum('bhnij,bhnjd->bhnid', t_inv, kb * jnp.exp(gc)[..., None])
    qk = jnp.where(incl, jnp.einsum('bhnid,bhnjd->bhnij', qc, kc) * decay, 0.0)
    q_dec = qc * jnp.exp(gc)[..., None]
    k_dec = kc * jnp.exp(gc[..., -1:] - gc)[..., None]
    g_last = jnp.exp(gc[..., -1])

    def step(state, xs):
        q_i, k_i, u_i, w_i, qk_i, gl_i = xs
        v_new = u_i - jnp.einsum('bhcd,bhde->bhce', w_i, state)
        o_i = jnp.einsum('bhcd,bhde->bhce', q_i, state) + jnp.einsum('bhij,bhje->bhie', qk_i, v_new)
        state = state * gl_i[..., None, None] + jnp.einsum('bhcd,bhce->bhde', k_i, v_new)
        return state, o_i

    xs = tuple(jnp.moveaxis(t, 2, 0) for t in (q_dec, k_dec, u, w, qk, g_last))
    state0 = jnp.zeros((b, n_heads, dk, dv), f32)
    _, o = lax.scan(step, state0, xs)
    o = jnp.moveaxis(o, 0, 2).reshape(b, n_heads, -1, dv)
    return jnp.transpose(o, (0, 2, 1, 3))[:, pad:]


def token_mixer(u, w_in, conv_qkv, a_log, dt_bias, gdn_norm, conv_sc, w_out):
    b, seq_len, _ = u.shape
    f32 = jnp.float32
    proj = u @ w_in
    cuts = [int(c) for c in np.cumsum(IN_SPLITS)[:-1]]
    q, k, v, z, b_logit, a_logit, sc_x, sc_b, sc_c = jnp.split(proj, cuts, axis=-1)

    qkv = jax.nn.silu(causal_depthwise_conv(jnp.concatenate([q, k, v], axis=-1), conv_qkv))
    q, k, v = (t.reshape(b, seq_len, GDN_HEADS, GDN_HEAD_DIM) for t in jnp.split(qkv, 3, axis=-1))
    q = l2_normalize(q) * (GDN_HEAD_DIM ** -0.5)
    k = l2_normalize(k)
    beta = jax.nn.sigmoid(b_logit.astype(f32))
    g = -jnp.exp(a_log.astype(f32)) * jax.nn.softplus(a_logit.astype(f32) + dt_bias.astype(f32))
    o = chunked_gated_delta_rule(q, k, v, g, beta)
    gate = jax.nn.silu(z.astype(f32)).reshape(b, seq_len, GDN_HEADS, GDN_HEAD_DIM)
    o = (rms_norm(o, gdn_norm) * gate).astype(u.dtype).reshape(b, seq_len, GDN_WIDTH)

    y_sc = sc_b * causal_depthwise_conv(sc_c * sc_x, conv_sc)

    return jnp.concatenate([o, y_sc], axis=-1) @ w_out


def swiglu(u, w_gate, w_up, w_down):
    return (jax.nn.silu(u @ w_gate) * (u @ w_up)) @ w_down


def setup_inputs(seed: int = 0) -> dict:
    key = jax.random.key(seed)
    ks = jax.random.split(key, 17)
    f32 = jnp.float32

    def nrm(k, shape, scale):
        return jax.random.normal(k, shape, f32) * scale

    def gain(k, width):
        return 1.0 + 0.02 * jax.random.normal(k, (DEPTH, width), f32)

    dt = jnp.exp(jax.random.uniform(ks[9], (DEPTH, GDN_HEADS), f32, math.log(1e-3), math.log(1e-1)))
    return {
        'x': nrm(ks[0], (BATCH, SEQ, D_MODEL), 1.0),
        'meta_tokens': nrm(ks[1], (N_META, D_MODEL), 1.0),
        'mix_pre_norm': gain(ks[2], D_MODEL),
        'mix_post_norm': gain(ks[3], D_MODEL),
        'ffn_pre_norm': gain(ks[4], D_MODEL),
        'ffn_post_norm': gain(ks[5], D_MODEL),
        'w_in': nrm(ks[6], (DEPTH, D_MODEL, IN_WIDTH), D_MODEL ** -0.5),
        'conv_qkv': nrm(ks[7], (DEPTH, GDN_CONV, 3 * GDN_WIDTH), GDN_CONV ** -0.5),
        'a_log': jnp.log(jax.random.uniform(ks[8], (DEPTH, GDN_HEADS), f32, 1.0, 16.0)),
        'dt_bias': dt + jnp.log(-jnp.expm1(-dt)),
        'gdn_norm': gain(ks[10], GDN_HEAD_DIM),
        'conv_sc': nrm(ks[11], (DEPTH, SC_CONV, SC_WIDTH), SC_CONV ** -0.5),
        'w_out': nrm(ks[12], (DEPTH, MIX_WIDTH, D_MODEL), MIX_WIDTH ** -0.5),
        'w_gate': nrm(ks[13], (DEPTH, D_MODEL, D_FF), D_MODEL ** -0.5),
        'w_up': nrm(ks[14], (DEPTH, D_MODEL, D_FF), D_MODEL ** -0.5),
        'w_down': nrm(ks[15], (DEPTH, D_FF, D_MODEL), D_FF ** -0.5),
    }


def reference(x, meta_tokens, mix_pre_norm, mix_post_norm, ffn_pre_norm, ffn_post_norm, w_in, conv_qkv,
              a_log, dt_bias, gdn_norm, conv_sc, w_out, w_gate, w_up, w_down):
    b = x.shape[0]
    meta = jnp.broadcast_to(meta_tokens.astype(x.dtype)[None], (b, N_META, D_MODEL))
    h = jnp.concatenate([meta, x], axis=1)
    for l in range(DEPTH):
        mix = token_mixer(rms_norm(h, mix_pre_norm[l]), w_in[l], conv_qkv[l], a_log[l], dt_bias[l],
                          gdn_norm[l], conv_sc[l], w_out[l])
        h = h + rms_norm(mix, mix_post_norm[l])
        ffn = swiglu(rms_norm(h, ffn_pre_norm[l]), w_gate[l], w_up[l], w_down[l])
        h = h + rms_norm(ffn, ffn_post_norm[l])
    return h[:, N_META:]
```

```python
import functools

import jax
import jax.numpy as jnp
from jax import lax
from jax.experimental import pallas as pl
from jax.experimental.pallas import tpu as pltpu

F32 = jnp.float32
BF16 = jnp.bfloat16
HIGHEST = lax.Precision.HIGHEST

D_MODEL = 1024
N_META = 16
HEADS = 4
HEAD_DIM = 128
GDN_WIDTH = HEADS * HEAD_DIM
SC_WIDTH = D_MODEL - GDN_WIDTH
CHUNK = 64
D_FF = 2816
EPS = 1e-6
LANES = 128
SUBLANES = 8
MAIN_COLS = 4 * GDN_WIDTH + 3 * SC_WIDTH
N_COLBLK = MAIN_COLS // LANES
VMEM_LIMIT = 56 * 1024 * 1024


def _sigmoid(x):
    return 1.0 / (1.0 + jnp.exp(-x))


def _silu(x):
    return x * _sigmoid(x)


def _softplus(x):
    return jnp.maximum(x, 0.0) + jnp.log1p(jnp.exp(-jnp.abs(x)))


def _rms_norm(x, gain):
    return x * lax.rsqrt(jnp.mean(x * x, axis=-1, keepdims=True) + EPS) * gain


def _inproj_kernel(x_ref, g_ref, w_ref, wl_ref, p_ref, l_ref):
    xb = _rms_norm(x_ref[...], g_ref[...]).astype(BF16)
    for j in range(N_COLBLK // 4):
        r = jnp.dot(xb, w_ref[:, 4 * LANES * j:4 * LANES * (j + 1)], preferred_element_type=F32)
        for c in range(4):
            p_ref[4 * j + c] = r[:, LANES * c:LANES * (c + 1)]
    l_ref[...] = jnp.dot(xb, wl_ref[...], preferred_element_type=F32)


def _inproj(x2d, gain, w_main, w_logit, tm):
    rows = x2d.shape[0]
    return pl.pallas_call(
        _inproj_kernel,
        grid=(rows // tm,),
        in_specs=[
            pl.BlockSpec((tm, D_MODEL), lambda i: (i, 0)),
            pl.BlockSpec((1, D_MODEL), lambda i: (0, 0)),
            pl.BlockSpec((D_MODEL, MAIN_COLS), lambda i: (0, 0)),
            pl.BlockSpec((D_MODEL, LANES), lambda i: (0, 0)),
        ],
        out_specs=[
            pl.BlockSpec((N_COLBLK, tm, LANES), lambda i: (0, i, 0)),
            pl.BlockSpec((tm, LANES), lambda i: (i, 0)),
        ],
        out_shape=[
            jax.ShapeDtypeStruct((N_COLBLK, rows, LANES), F32),
            jax.ShapeDtypeStruct((rows, LANES), F32),
        ],
        compiler_params=pltpu.CompilerParams(
            dimension_semantics=("parallel",), vmem_limit_bytes=VMEM_LIMIT),
        name="inproj",
    )(x2d, gain, w_main, w_logit)


def _unit_lower_inverse(a):
    n = CHUNK // SUBLANES
    r8 = lax.broadcasted_iota(jnp.int32, (SUBLANES, CHUNK), 0)
    c8 = lax.broadcasted_iota(jnp.int32, (SUBLANES, CHUNK), 1)
    x = [(c8 == r8 + SUBLANES * b).astype(F32) for b in range(n)]
    for j in range(CHUNK - 1):
        bj, rj = divmod(j, SUBLANES)
        pivot = x[bj][rj:rj + 1, :]
        for b in range(bj, n):
            if b == bj and rj == SUBLANES - 1:
                continue
            x[b] = x[b] - a[SUBLANES * b:SUBLANES * (b + 1), j:j + 1] * pivot
    return jnp.concatenate(x, axis=0)


def _causal_conv4(x_ext, w):
    acc = x_ext[SUBLANES:SUBLANES + CHUNK] * w[3:4]
    for i in range(3):
        lo = SUBLANES - 3 + i
        acc = acc + x_ext[lo:lo + CHUNK] * w[i:i + 1]
    return acc


def _l2_normalize(x):
    return x * lax.rsqrt(jnp.sum(x * x, axis=-1, keepdims=True) + EPS)


def _dot_nt(a, b):
    return lax.dot_general(a, b, (((1,), (1,)), ((), ())), preferred_element_type=F32)


def _dot_tn(a, b, **kw):
    return lax.dot_general(a, b, (((0,), (0,)), ((), ())), preferred_element_type=F32, **kw)


def _gdn_chunk(q_ext, k_ext, v_ext, z, logits, valid, states, consts, need_out):
    cw_ref, neg_a, dtb, gnorm = consts
    row = lax.broadcasted_iota(jnp.int32, (CHUNK, CHUNK), 0)
    col = lax.broadcasted_iota(jnp.int32, (CHUNK, CHUNK), 1)
    incl = row >= col
    strict = row > col

    beta_all = _sigmoid(logits)
    g_all = neg_a * _softplus(logits + dtb)
    if valid is not None:
        beta_all = jnp.where(valid, beta_all, 0.0)
        g_all = jnp.where(valid, g_all, 0.0)
    gc_cols = jnp.dot(incl.astype(F32), g_all, precision=HIGHEST, preferred_element_type=F32)
    gc_rows = _dot_tn(g_all, (row <= col).astype(F32), precision=HIGHEST)

    outs, new_states = [], []
    for h in range(HEADS):
        q = _silu(_causal_conv4(q_ext[h], cw_ref[h])) if need_out else None
        k = _silu(_causal_conv4(k_ext[h], cw_ref[HEADS + h]))
        v = _silu(_causal_conv4(v_ext[h], cw_ref[2 * HEADS + h]))
        k = _l2_normalize(k)
        beta = beta_all[:, h:h + 1]
        gc = gc_cols[:, HEADS + h:HEADS + h + 1]
        gc_r = gc_rows[HEADS + h:HEADS + h + 1, :]
        gc_last = gc[CHUNK - 1:CHUNK, :]
        decay = jnp.where(incl, jnp.exp(jnp.where(incl, gc - gc_r, 0.0)), 0.0)
        eg = jnp.exp(gc)
        kb = k * beta
        k16 = k.astype(BF16)
        a = jnp.where(strict, _dot_nt(kb.astype(BF16), k16) * decay, 0.0)
        t_inv = _unit_lower_inverse(a).astype(BF16)
        u = jnp.dot(t_inv, (v * beta).astype(BF16), preferred_element_type=F32)
        w = jnp.dot(t_inv, (kb * eg).astype(BF16), preferred_element_type=F32)
        s16 = states[h].astype(BF16)
        v_new = u - jnp.dot(w.astype(BF16), s16, preferred_element_type=F32)
        vn16 = v_new.astype(BF16)
        if need_out:
            q = _l2_normalize(q) * (HEAD_DIM ** -0.5)
            qk = jnp.where(incl, _dot_nt(q.astype(BF16), k16) * decay, 0.0)
            o = (jnp.dot((q * eg).astype(BF16), s16, preferred_element_type=F32)
                 + jnp.dot(qk.astype(BF16), vn16, preferred_element_type=F32))
            outs.append(_rms_norm(o, gnorm) * _silu(z[h]))
        k_dec = k * jnp.exp(gc_last - gc)
        new_states.append(states[h] * jnp.exp(gc_last) + _dot_tn(k_dec.astype(BF16), vn16))
    return outs, new_states


def _gdn_kernel(pq_ref, pk_ref, pv_ref, pz_ref, lg_ref, mq_ref, mk_ref, mv_ref, ml_ref,
                cw_ref, alog_ref, dtb_ref, gn_ref, o_ref, s_ref, halo_ref, *, n_chunks):
    consts = (cw_ref, -jnp.exp(alog_ref[...]), dtb_ref[...], gn_ref[...])

    @pl.when(pl.program_id(1) == 0)
    def _meta_chunk():
        pad = CHUNK - N_META
        zeros = jnp.zeros((SUBLANES + pad, LANES), F32)
        k_ext = [jnp.concatenate([zeros, mk_ref[h]], axis=0) for h in range(HEADS)]
        v_ext = [jnp.concatenate([zeros, mv_ref[h]], axis=0) for h in range(HEADS)]
        logits = jnp.concatenate([jnp.zeros((pad, LANES), F32), ml_ref[...]], axis=0)
        valid = lax.broadcasted_iota(jnp.int32, (CHUNK, LANES), 0) >= pad
        zero_state = [jnp.zeros((HEAD_DIM, HEAD_DIM), F32)] * HEADS
        _, st = _gdn_chunk(None, k_ext, v_ext, None, logits, valid, zero_state, consts, False)
        for h in range(HEADS):
            s_ref[h] = st[h]
            for t, m_ref in enumerate((mq_ref, mk_ref, mv_ref)):
                halo_ref[t * HEADS + h] = m_ref[h, N_META - SUBLANES:N_META, :]

    def body(c, carry):
        r0 = pl.multiple_of(c * CHUNK, CHUNK)
        ext = []
        for t, p_ref in enumerate((pq_ref, pk_ref, pv_ref)):
            ext.append([jnp.concatenate([halo_ref[t * HEADS + h], p_ref[h, pl.ds(r0, CHUNK), :]], axis=0)
                        for h in range(HEADS)])
        z = [pz_ref[h, pl.ds(r0, CHUNK), :] for h in range(HEADS)]
        states = [s_ref[h] for h in range(HEADS)]
        outs, st = _gdn_chunk(ext[0], ext[1], ext[2], z, lg_ref[pl.ds(r0, CHUNK), :], None,
                              states, consts, True)
        for h in range(HEADS):
            o_ref[pl.ds(r0, CHUNK), LANES * h:LANES * (h + 1)] = outs[h]
            s_ref[h] = st[h]
            for t in range(3):
                halo_ref[t * HEADS + h] = ext[t][h][CHUNK:CHUNK + SUBLANES]
        return carry

    lax.fori_loop(0, n_chunks, body, 0)


def _gdn(proj, logits, meta_proj, meta_logits, conv_w, alog_row, dtb_row, gnorm, batch, seq, ts):
    ns = seq // ts
    rows = batch * seq
    blk = lambda cb: pl.BlockSpec((HEADS, ts, LANES), lambda b, s: (cb, b * ns + s, 0))
    mblk = lambda cb: pl.BlockSpec((HEADS, N_META, LANES), lambda b, s: (cb, 0, 0))
    full = lambda shape: pl.BlockSpec(shape, lambda b, s: (0,) * len(shape))
    return pl.pallas_call(
        functools.partial(_gdn_kernel, n_chunks=ts // CHUNK),
        grid=(batch, ns),
        in_specs=[
            blk(0), blk(1), blk(2), blk(3),
            pl.BlockSpec((ts, LANES), lambda b, s: (b * ns + s, 0)),
            mblk(0), mblk(1), mblk(2),
            full((N_META, LANES)),
            full((3 * HEADS, 4, LANES)),
            full((1, LANES)), full((1, LANES)), full((1, LANES)),
        ],
        out_specs=pl.BlockSpec((ts, GDN_WIDTH), lambda b, s: (b * ns + s, 0)),
        out_shape=jax.ShapeDtypeStruct((rows, GDN_WIDTH), F32),
        scratch_shapes=[
            pltpu.VMEM((HEADS, HEAD_DIM, HEAD_DIM), F32),
            pltpu.VMEM((3 * HEADS, SUBLANES, LANES), F32),
        ],
        compiler_params=pltpu.CompilerParams(
            dimension_semantics=("parallel", "arbitrary"), vmem_limit_bytes=VMEM_LIMIT),
        name="gdn",
    )(proj, proj, proj, proj, logits, meta_proj, meta_proj, meta_proj, meta_logits,
      conv_w, alog_row, dtb_row, gnorm)


def _post_kernel(x_ref, o_ref, sx_ref, sb_ref, sc_ref, hx_ref, hc_ref, mx_ref, mc_ref, cw_ref,
                 wo_ref, g_post_ref, g_pre_ref, wg_ref, wu_ref, wd_ref, g_fpost_ref, out_ref,
                 *, tiles_per_seq, ff_tile):
    tm = x_ref.shape[0]
    cat = lambda ref: jnp.concatenate([ref[c] for c in range(SC_WIDTH // LANES)], axis=1)
    p = cat(sc_ref) * cat(sx_ref)
    first = pl.program_id(0) % tiles_per_seq == 0
    prev_tile = cat(hc_ref) * cat(hx_ref)
    prev_meta = (cat(mc_ref) * cat(mx_ref))[N_META - SUBLANES:N_META]
    prev = jnp.where(first, prev_meta, prev_tile)
    p_ext = jnp.concatenate([prev, p], axis=0)
    cw = cw_ref[...]
    y = p_ext[SUBLANES:] * cw[2:3]
    y = y + p_ext[SUBLANES - 1:SUBLANES - 1 + tm] * cw[1:2]
    y = y + p_ext[SUBLANES - 2:SUBLANES - 2 + tm] * cw[0:1]
    y_sc = cat(sb_ref) * y
    mixed = jnp.concatenate([o_ref[...], y_sc], axis=1).astype(BF16)
    mix = jnp.dot(mixed, wo_ref[...], preferred_element_type=F32)
    h1 = x_ref[...] + _rms_norm(mix, g_post_ref[...])
    u = _rms_norm(h1, g_pre_ref[...]).astype(BF16)
    ffn = jnp.zeros((tm, D_MODEL), F32)
    for j in range(D_FF // ff_tile):
        sl = slice(ff_tile * j, ff_tile * (j + 1))
        gate = jnp.dot(u, wg_ref[:, sl], preferred_element_type=F32)
        up = jnp.dot(u, wu_ref[:, sl], preferred_element_type=F32)
        act = (_silu(gate) * up).astype(BF16)
        ffn = ffn + jnp.dot(act, wd_ref[sl, :], preferred_element_type=F32)
    out_ref[...] = h1 + _rms_norm(ffn, g_fpost_ref[...])


def _post(x2d, o, proj, meta_proj, conv_sc, w_out, g_post, g_pre, w_gate, w_up, w_down, g_fpost,
          seq, tm):
    rows = x2d.shape[0]
    nblk = SC_WIDTH // LANES
    tiles_per_seq = seq // tm
    hb = tm // SUBLANES
    blk = lambda cb: pl.BlockSpec((nblk, tm, LANES), lambda i: (cb, i, 0))
    halo = lambda cb: pl.BlockSpec((nblk, SUBLANES, LANES),
                                   lambda i: (cb, jnp.maximum(i * hb - 1, 0), 0))
    mblk = lambda cb: pl.BlockSpec((nblk, N_META, LANES), lambda i: (cb, 0, 0))
    once = lambda shape: pl.BlockSpec(shape, lambda i: (0, 0), pipeline_mode=pl.Buffered(1))
    return pl.pallas_call(
        functools.partial(_post_kernel, tiles_per_seq=tiles_per_seq, ff_tile=704),
        grid=(rows // tm,),
        in_specs=[
            pl.BlockSpec((tm, D_MODEL), lambda i: (i, 0)),
            pl.BlockSpec((tm, GDN_WIDTH), lambda i: (i, 0)),
            blk(4), blk(5), blk(6),
            halo(4), halo(6),
            mblk(4), mblk(6),
            once((3, SC_WIDTH)),
            once((D_MODEL, D_MODEL)), once((1, D_MODEL)), once((1, D_MODEL)),
            once((D_MODEL, D_FF)), once((D_MODEL, D_FF)), once((D_FF, D_MODEL)), once((1, D_MODEL)),
        ],
        out_specs=pl.BlockSpec((tm, D_MODEL), lambda i: (i, 0)),
        out_shape=jax.ShapeDtypeStruct((rows, D_MODEL), F32),
        compiler_params=pltpu.CompilerParams(
            dimension_semantics=("parallel",), vmem_limit_bytes=VMEM_LIMIT),
        name="post",
    )(x2d, o, proj, proj, proj, proj, proj, meta_proj, meta_proj, conv_sc,
      w_out, g_post, g_pre, w_gate, w_up, w_down, g_fpost)


def kernel(x, meta_tokens, mix_pre_norm, mix_post_norm, ffn_pre_norm, ffn_post_norm, w_in, conv_qkv,
           a_log, dt_bias, gdn_norm, conv_sc, w_out, w_gate, w_up, w_down):
    batch, seq, _ = x.shape
    assert mix_pre_norm.shape[0] == 1, "single-layer block"
    n_logit = 2 * HEADS
    cut = 4 * GDN_WIDTH
    w_in0 = w_in[0]
    w_main = jnp.concatenate([w_in0[:, :cut], w_in0[:, cut + n_logit:]], axis=1).astype(BF16)
    w_logit = jnp.pad(w_in0[:, cut:cut + n_logit], ((0, 0), (0, LANES - n_logit))).astype(BF16)
    lane_pad = lambda v: jnp.pad(v.reshape(1, HEADS), ((0, 0), (HEADS, LANES - 2 * HEADS)))
    conv_w = conv_qkv[0].reshape(4, 3 * HEADS, LANES).transpose(1, 0, 2)
    g_mix_pre = mix_pre_norm[0].reshape(1, D_MODEL)

    x2d = x.reshape(batch * seq, D_MODEL)
    tm = min(512, seq)
    proj, logits = _inproj(x2d, g_mix_pre, w_main, w_logit, tm)
    meta_proj, meta_logits = _inproj(meta_tokens, g_mix_pre, w_main, w_logit, N_META)
    o = _gdn(proj, logits, meta_proj, meta_logits, conv_w, lane_pad(a_log[0]), lane_pad(dt_bias[0]),
             gdn_norm[0].reshape(1, HEAD_DIM), batch, seq, min(512, seq))
    out = _post(x2d, o, proj, meta_proj, conv_sc[0], w_out[0].astype(BF16),
                mix_post_norm[0].reshape(1, D_MODEL), ffn_pre_norm[0].reshape(1, D_MODEL),
                w_gate[0].astype(BF16), w_up[0].astype(BF16), w_down[0].astype(BF16),
                ffn_post_norm[0].reshape(1, D_MODEL), seq, tm)
    return out.reshape(batch, seq, D_MODEL)
```

```python
import functools

import jax
import jax.numpy as jnp
from jax import lax
from jax.experimental import pallas as pl
from jax.experimental.pallas import tpu as pltpu

F32 = jnp.float32
BF16 = jnp.bfloat16
HIGHEST = lax.Precision.HIGHEST

D_MODEL = 1024
N_META = 16
HEADS = 4
HEAD_DIM = 128
GDN_WIDTH = HEADS * HEAD_DIM
SC_WIDTH = D_MODEL - GDN_WIDTH
CHUNK = 64
D_FF = 2816
EPS = 1e-6
LANES = 128
SUBLANES = 8
MAIN_COLS = 4 * GDN_WIDTH + 3 * SC_WIDTH
N_COLBLK = MAIN_COLS // LANES
INV_BLK = 16
VMEM_LIMIT = 56 * 1024 * 1024


def _sigmoid(x):
    return 1.0 / (1.0 + jnp.exp(-x))


def _silu(x):
    return x * _sigmoid(x)


def _softplus(x):
    return jnp.maximum(x, 0.0) + jnp.log1p(jnp.exp(-jnp.abs(x)))


def _rms_norm(x, gain):
    return x * lax.rsqrt(jnp.mean(x * x, axis=-1, keepdims=True) + EPS) * gain


def _inproj_kernel(x_ref, g_ref, w_ref, wl_ref, p_ref, l_ref):
    xb = _rms_norm(x_ref[...], g_ref[...]).astype(BF16)
    for j in range(N_COLBLK // 4):
        r = jnp.dot(xb, w_ref[:, 4 * LANES * j:4 * LANES * (j + 1)], preferred_element_type=F32)
        for c in range(4):
            p_ref[4 * j + c] = r[:, LANES * c:LANES * (c + 1)]
    l_ref[...] = jnp.dot(xb, wl_ref[...], preferred_element_type=F32)


def _inproj(x2d, gain, w_main, w_logit, tm):
    rows = x2d.shape[0]
    return pl.pallas_call(
        _inproj_kernel,
        grid=(rows // tm,),
        in_specs=[
            pl.BlockSpec((tm, D_MODEL), lambda i: (i, 0)),
            pl.BlockSpec((1, D_MODEL), lambda i: (0, 0)),
            pl.BlockSpec((D_MODEL, MAIN_COLS), lambda i: (0, 0)),
            pl.BlockSpec((D_MODEL, LANES), lambda i: (0, 0)),
        ],
        out_specs=[
            pl.BlockSpec((N_COLBLK, tm, LANES), lambda i: (0, i, 0)),
            pl.BlockSpec((tm, LANES), lambda i: (i, 0)),
        ],
        out_shape=[
            jax.ShapeDtypeStruct((N_COLBLK, rows, LANES), F32),
            jax.ShapeDtypeStruct((rows, LANES), F32),
        ],
        compiler_params=pltpu.CompilerParams(
            dimension_semantics=("parallel",), vmem_limit_bytes=VMEM_LIMIT),
        name="inproj",
    )(x2d, gain, w_main, w_logit)


def _pair_diag(x0, x1):
    z = jnp.zeros_like(x0)
    return jnp.concatenate([jnp.concatenate([x0, z], axis=1), jnp.concatenate([z, x1], axis=1)], axis=0)


def _diag_block_inverse(a2):
    nblk = CHUNK // INV_BLK
    lane = lax.broadcasted_iota(jnp.int32, (INV_BLK, LANES), 1)
    row = lax.broadcasted_iota(jnp.int32, (INV_BLK, LANES), 0)
    grp = (lane & (CHUNK - 1)) // INV_BLK
    z = jnp.zeros((INV_BLK, LANES), F32)
    for m in range(nblk):
        z = jnp.where(grp == m, a2[INV_BLK * m:INV_BLK * (m + 1), :], z)
    base = lane - (lane & (INV_BLK - 1))
    d = ((lane & (INV_BLK - 1)) == row).astype(F32)
    halves = [d[:SUBLANES], d[SUBLANES:]]
    z_halves = [z[:SUBLANES], z[SUBLANES:]]
    for j in range(INV_BLK - 1):
        idx = base[:SUBLANES] + j
        pivot = halves[j // SUBLANES][j % SUBLANES:j % SUBLANES + 1, :]
        for t in range(2):
            if SUBLANES * (t + 1) - 1 <= j:
                continue
            halves[t] = halves[t] - jnp.take_along_axis(z_halves[t], idx, axis=1) * pivot
    dz = jnp.concatenate(halves, axis=0)
    return jnp.concatenate([jnp.where(grp == m, dz, 0.0) for m in range(nblk)], axis=0)


def _pair_inverse_apply(a2, rhs0, rhs1):
    row = lax.broadcasted_iota(jnp.int32, (CHUNK, LANES), 0)
    lane = lax.broadcasted_iota(jnp.int32, (CHUNK, LANES), 1)
    xd = _diag_block_inverse(a2).astype(BF16)
    off = jnp.where(row // INV_BLK == (lane & (CHUNK - 1)) // INV_BLK, 0.0, a2)
    off_bd = jnp.concatenate([jnp.where(lane < CHUNK, off, 0.0), jnp.where(lane >= CHUNK, off, 0.0)], axis=0)
    n16 = jnp.dot(xd, off_bd.astype(BF16), preferred_element_type=F32).astype(BF16)
    width = rhs0.shape[1]
    y1 = jnp.dot(xd, _pair_diag(rhs0, rhs1), preferred_element_type=F32)
    t = y1
    for _ in range(CHUNK // INV_BLK - 1):
        t16 = t.astype(BF16)
        t = y1 - jnp.dot(n16, _pair_diag(t16[:, :width], t16[:, width:]), preferred_element_type=F32)
    return t


def _causal_conv4(x_ext, w):
    acc = x_ext[SUBLANES:SUBLANES + CHUNK] * w[3:4]
    for i in range(3):
        lo = SUBLANES - 3 + i
        acc = acc + x_ext[lo:lo + CHUNK] * w[i:i + 1]
    return acc


def _l2_normalize(x):
    return x * lax.rsqrt(jnp.sum(x * x, axis=-1, keepdims=True) + EPS)


def _dot_nt(a, b):
    return lax.dot_general(a, b, (((1,), (1,)), ((), ())), preferred_element_type=F32)


def _dot_tn(a, b, **kw):
    return lax.dot_general(a, b, (((0,), (0,)), ((), ())), preferred_element_type=F32, **kw)


def _gdn_chunk(q_ext, k_ext, v_ext, z, logits, valid, states, consts, need_out):
    cw_ref, neg_a, dtb, gnorm = consts
    row = lax.broadcasted_iota(jnp.int32, (CHUNK, CHUNK), 0)
    col = lax.broadcasted_iota(jnp.int32, (CHUNK, CHUNK), 1)
    incl = row >= col
    strict = row > col

    beta_all = _sigmoid(logits)
    g_all = neg_a * _softplus(logits + dtb)
    if valid is not None:
        beta_all = jnp.where(valid, beta_all, 0.0)
        g_all = jnp.where(valid, g_all, 0.0)
    gc_cols = jnp.dot(incl.astype(F32), g_all, precision=HIGHEST, preferred_element_type=F32)
    gc_rows = _dot_tn(g_all, (row <= col).astype(F32), precision=HIGHEST)

    row2 = lax.broadcasted_iota(jnp.int32, (CHUNK, LANES), 0)
    lane2 = lax.broadcasted_iota(jnp.int32, (CHUNK, LANES), 1)
    strict2 = row2 > (lane2 & (CHUNK - 1))

    ks, k16s, kbs, gcs, egs, decays, rhs = [], [], [], [], [], [], []
    for h in range(HEADS):
        k = _l2_normalize(_silu(_causal_conv4(k_ext[h], cw_ref[HEADS + h])))
        v = _silu(_causal_conv4(v_ext[h], cw_ref[2 * HEADS + h]))
        beta = beta_all[:, h:h + 1]
        gc = gc_cols[:, HEADS + h:HEADS + h + 1]
        gc_r = gc_rows[HEADS + h:HEADS + h + 1, :]
        decays.append(jnp.where(incl, jnp.exp(jnp.where(incl, gc - gc_r, 0.0)), 0.0))
        eg = jnp.exp(gc)
        kb = k * beta
        ks.append(k)
        k16s.append(k.astype(BF16))
        kbs.append(kb.astype(BF16))
        gcs.append(gc)
        egs.append(eg)
        rhs.append(jnp.concatenate([(v * beta).astype(BF16), (kb * eg).astype(BF16)], axis=1))

    us, ws = [], []
    for h0 in range(0, HEADS, 2):
        h1 = h0 + 1
        kk = _dot_nt(jnp.concatenate([kbs[h0], kbs[h1]], axis=1), _pair_diag(k16s[h0], k16s[h1]))
        a2 = jnp.where(strict2, kk * jnp.concatenate([decays[h0], decays[h1]], axis=1), 0.0)
        y = _pair_inverse_apply(a2, rhs[h0], rhs[h1])
        for i in range(2):
            us.append(y[:, 2 * i * HEAD_DIM:(2 * i + 1) * HEAD_DIM])
            ws.append(y[:, (2 * i + 1) * HEAD_DIM:(2 * i + 2) * HEAD_DIM])

    outs, new_states = [], []
    for h in range(HEADS):
        gc = gcs[h]
        gc_last = gc[CHUNK - 1:CHUNK, :]
        s16 = states[h].astype(BF16)
        v_new = us[h] - jnp.dot(ws[h].astype(BF16), s16, preferred_element_type=F32)
        vn16 = v_new.astype(BF16)
        if need_out:
            q = _l2_normalize(_silu(_causal_conv4(q_ext[h], cw_ref[h]))) * (HEAD_DIM ** -0.5)
            qk = jnp.where(incl, _dot_nt(q.astype(BF16), k16s[h]) * decays[h], 0.0)
            o = (jnp.dot((q * egs[h]).astype(BF16), s16, preferred_element_type=F32)
                 + jnp.dot(qk.astype(BF16), vn16, preferred_element_type=F32))
            outs.append(_rms_norm(o, gnorm) * _silu(z[h]))
        k_dec = ks[h] * jnp.exp(gc_last - gc)
        new_states.append(states[h] * jnp.exp(gc_last) + _dot_tn(k_dec.astype(BF16), vn16))
    return outs, new_states


def _gdn_kernel(pq_ref, pk_ref, pv_ref, pz_ref, lg_ref, mq_ref, mk_ref, mv_ref, ml_ref,
                cw_ref, alog_ref, dtb_ref, gn_ref, o_ref, s_ref, halo_ref, *, n_chunks):
    consts = (cw_ref, -jnp.exp(alog_ref[...]), dtb_ref[...], gn_ref[...])

    @pl.when(pl.program_id(1) == 0)
    def _meta_chunk():
        pad = CHUNK - N_META
        zeros = jnp.zeros((SUBLANES + pad, LANES), F32)
        k_ext = [jnp.concatenate([zeros, mk_ref[h]], axis=0) for h in range(HEADS)]
        v_ext = [jnp.concatenate([zeros, mv_ref[h]], axis=0) for h in range(HEADS)]
        logits = jnp.concatenate([jnp.zeros((pad, LANES), F32), ml_ref[...]], axis=0)
        valid = lax.broadcasted_iota(jnp.int32, (CHUNK, LANES), 0) >= pad
        zero_state = [jnp.zeros((HEAD_DIM, HEAD_DIM), F32)] * HEADS
        _, st = _gdn_chunk(None, k_ext, v_ext, None, logits, valid, zero_state, consts, False)
        for h in range(HEADS):
            s_ref[h] = st[h]
            for t, m_ref in enumerate((mq_ref, mk_ref, mv_ref)):
                halo_ref[t * HEADS + h] = m_ref[h, N_META - SUBLANES:N_META, :]

    def body(c, carry):
        r0 = pl.multiple_of(c * CHUNK, CHUNK)
        ext = []
        for t, p_ref in enumerate((pq_ref, pk_ref, pv_ref)):
            ext.append([jnp.concatenate([halo_ref[t * HEADS + h], p_ref[h, pl.ds(r0, CHUNK), :]], axis=0)
                        for h in range(HEADS)])
        z = [pz_ref[h, pl.ds(r0, CHUNK), :] for h in range(HEADS)]
        states = [s_ref[h] for h in range(HEADS)]
        outs, st = _gdn_chunk(ext[0], ext[1], ext[2], z, lg_ref[pl.ds(r0, CHUNK), :], None,
                              states, consts, True)
        for h in range(HEADS):
            o_ref[pl.ds(r0, CHUNK), LANES * h:LANES * (h + 1)] = outs[h]
            s_ref[h] = st[h]
            for t in range(3):
                halo_ref[t * HEADS + h] = ext[t][h][CHUNK:CHUNK + SUBLANES]
        return carry

    lax.fori_loop(0, n_chunks, body, 0)


def _gdn(proj, logits, meta_proj, meta_logits, conv_w, alog_row, dtb_row, gnorm, batch, seq, ts):
    ns = seq // ts
    rows = batch * seq
    blk = lambda cb: pl.BlockSpec((HEADS, ts, LANES), lambda b, s: (cb, b * ns + s, 0))
    mblk = lambda cb: pl.BlockSpec((HEADS, N_META, LANES), lambda b, s: (cb, 0, 0))
    full = lambda shape: pl.BlockSpec(shape, lambda b, s: (0,) * len(shape))
    return pl.pallas_call(
        functools.partial(_gdn_kernel, n_chunks=ts // CHUNK),
        grid=(batch, ns),
        in_specs=[
            blk(0), blk(1), blk(2), blk(3),
            pl.BlockSpec((ts, LANES), lambda b, s: (b * ns + s, 0)),
            mblk(0), mblk(1), mblk(2),
            full((N_META, LANES)),
            full((3 * HEADS, 4, LANES)),
            full((1, LANES)), full((1, LANES)), full((1, LANES)),
        ],
        out_specs=pl.BlockSpec((ts, GDN_WIDTH), lambda b, s: (b * ns + s, 0)),
        out_shape=jax.ShapeDtypeStruct((rows, GDN_WIDTH), F32),
        scratch_shapes=[
            pltpu.VMEM((HEADS, HEAD_DIM, HEAD_DIM), F32),
            pltpu.VMEM((3 * HEADS, SUBLANES, LANES), F32),
        ],
        compiler_params=pltpu.CompilerParams(
            dimension_semantics=("parallel", "arbitrary"), vmem_limit_bytes=VMEM_LIMIT),
        name="gdn",
    )(proj, proj, proj, proj, logits, meta_proj, meta_proj, meta_proj, meta_logits,
      conv_w, alog_row, dtb_row, gnorm)


def _post_kernel(x_ref, o_ref, sx_ref, sb_ref, sc_ref, hx_ref, hc_ref, mx_ref, mc_ref, cw_ref,
                 wo_ref, g_post_ref, g_pre_ref, wg_ref, wu_ref, wd_ref, g_fpost_ref, out_ref,
                 *, tiles_per_seq, ff_tile):
    tm = x_ref.shape[0]
    cat = lambda ref: jnp.concatenate([ref[c] for c in range(SC_WIDTH // LANES)], axis=1)
    p = cat(sc_ref) * cat(sx_ref)
    first = pl.program_id(0) % tiles_per_seq == 0
    prev_tile = cat(hc_ref) * cat(hx_ref)
    prev_meta = (cat(mc_ref) * cat(mx_ref))[N_META - SUBLANES:N_META]
    prev = jnp.where(first, prev_meta, prev_tile)
    p_ext = jnp.concatenate([prev, p], axis=0)
    cw = cw_ref[...]
    y = p_ext[SUBLANES:] * cw[2:3]
    y = y + p_ext[SUBLANES - 1:SUBLANES - 1 + tm] * cw[1:2]
    y = y + p_ext[SUBLANES - 2:SUBLANES - 2 + tm] * cw[0:1]
    y_sc = cat(sb_ref) * y
    mixed = jnp.concatenate([o_ref[...], y_sc], axis=1).astype(BF16)
    mix = jnp.dot(mixed, wo_ref[...], preferred_element_type=F32)
    h1 = x_ref[...] + _rms_norm(mix, g_post_ref[...])
    u = _rms_norm(h1, g_pre_ref[...]).astype(BF16)
    ffn = jnp.zeros((tm, D_MODEL), F32)
    for j in range(D_FF // ff_tile):
        sl = slice(ff_tile * j, ff_tile * (j + 1))
        gate = jnp.dot(u, wg_ref[:, sl], preferred_element_type=F32)
        up = jnp.dot(u, wu_ref[:, sl], preferred_element_type=F32)
        act = (_silu(gate) * up).astype(BF16)
        ffn = ffn + jnp.dot(act, wd_ref[sl, :], preferred_element_type=F32)
    out_ref[...] = h1 + _rms_norm(ffn, g_fpost_ref[...])


def _post(x2d, o, proj, meta_proj, conv_sc, w_out, g_post, g_pre, w_gate, w_up, w_down, g_fpost,
          seq, tm):
    rows = x2d.shape[0]
    nblk = SC_WIDTH // LANES
    tiles_per_seq = seq // tm
    hb = tm // SUBLANES
    blk = lambda cb: pl.BlockSpec((nblk, tm, LANES), lambda i: (cb, i, 0))
    halo = lambda cb: pl.BlockSpec((nblk, SUBLANES, LANES),
                                   lambda i: (cb, jnp.maximum(i * hb - 1, 0), 0))
    mblk = lambda cb: pl.BlockSpec((nblk, N_META, LANES), lambda i: (cb, 0, 0))
    once = lambda shape: pl.BlockSpec(shape, lambda i: (0, 0), pipeline_mode=pl.Buffered(1))
    return pl.pallas_call(
        functools.partial(_post_kernel, tiles_per_seq=tiles_per_seq, ff_tile=704),
        grid=(rows // tm,),
        in_specs=[
            pl.BlockSpec((tm, D_MODEL), lambda i: (i, 0)),
            pl.BlockSpec((tm, GDN_WIDTH), lambda i: (i, 0)),
            blk(4), blk(5), blk(6),
            halo(4), halo(6),
            mblk(4), mblk(6),
            once((3, SC_WIDTH)),
            once((D_MODEL, D_MODEL)), once((1, D_MODEL)), once((1, D_MODEL)),
            once((D_MODEL, D_FF)), once((D_MODEL, D_FF)), once((D_FF, D_MODEL)), once((1, D_MODEL)),
        ],
        out_specs=pl.BlockSpec((tm, D_MODEL), lambda i: (i, 0)),
        out_shape=jax.ShapeDtypeStruct((rows, D_MODEL), F32),
        compiler_params=pltpu.CompilerParams(
            dimension_semantics=("parallel",), vmem_limit_bytes=VMEM_LIMIT),
        name="post",
    )(x2d, o, proj, proj, proj, proj, proj, meta_proj, meta_proj, conv_sc,
      w_out, g_post, g_pre, w_gate, w_up, w_down, g_fpost)


def kernel(x, meta_tokens, mix_pre_norm, mix_post_norm, ffn_pre_norm, ffn_post_norm, w_in, conv_qkv,
           a_log, dt_bias, gdn_norm, conv_sc, w_out, w_gate, w_up, w_down):
    batch, seq, _ = x.shape
    assert mix_pre_norm.shape[0] == 1, "single-layer block"
    n_logit = 2 * HEADS
    cut = 4 * GDN_WIDTH
    w_in0 = w_in[0]
    w_main = jnp.concatenate([w_in0[:, :cut], w_in0[:, cut + n_logit:]], axis=1).astype(BF16)
    w_logit = jnp.pad(w_in0[:, cut:cut + n_logit], ((0, 0), (0, LANES - n_logit))).astype(BF16)
    lane_pad = lambda v: jnp.pad(v.reshape(1, HEADS), ((0, 0), (HEADS, LANES - 2 * HEADS)))
    conv_w = conv_qkv[0].reshape(4, 3 * HEADS, LANES).transpose(1, 0, 2)
    g_mix_pre = mix_pre_norm[0].reshape(1, D_MODEL)

    x2d = x.reshape(batch * seq, D_MODEL)
    tm = min(512, seq)
    proj, logits = _inproj(x2d, g_mix_pre, w_main, w_logit, tm)
    meta_proj, meta_logits = _inproj(meta_tokens, g_mix_pre, w_main, w_logit, N_META)
    o = _gdn(proj, logits, meta_proj, meta_logits, conv_w, lane_pad(a_log[0]), lane_pad(dt_bias[0]),
             gdn_norm[0].reshape(1, HEAD_DIM), batch, seq, min(512, seq))
    out = _post(x2d, o, proj, meta_proj, conv_sc[0], w_out[0].astype(BF16),
                mix_post_norm[0].reshape(1, D_MODEL), ffn_pre_norm[0].reshape(1, D_MODEL),
                w_gate[0].astype(BF16), w_up[0].astype(BF16), w_down[0].astype(BF16),
                ffn_post_norm[0].reshape(1, D_MODEL), seq, tm)
    return out.reshape(batch, seq, D_MODEL)
```

```python
import functools

import jax
import jax.numpy as jnp
from jax import lax
from jax.experimental import pallas as pl
from jax.experimental.pallas import tpu as pltpu

F32 = jnp.float32
BF16 = jnp.bfloat16

D_MODEL = 1024
N_META = 16
HEADS = 4
HEAD_DIM = 128
GDN_WIDTH = HEADS * HEAD_DIM
SC_WIDTH = D_MODEL - GDN_WIDTH
CHUNK = 64
D_FF = 2816
EPS = 1e-6
LANES = 128
SUBLANES = 8
GROUP = 4 * LANES
N_GROUPS = 7
N_HALO = 4 * HEADS
INV_BLK = 16
FACTOR_UNROLL = 4
VMEM_LIMIT = 56 * 1024 * 1024


def _sigmoid(x):
    return 1.0 / (1.0 + jnp.exp(-x))


def _silu(x):
    return x * _sigmoid(x)


def _softplus(x):
    return jnp.maximum(x, 0.0) + jnp.log1p(jnp.exp(-jnp.abs(x)))


def _rms_norm(x, gain):
    return x * lax.rsqrt(jnp.mean(x * x, axis=-1, keepdims=True) + EPS) * gain


def _l2_normalize(x):
    return x * lax.rsqrt(jnp.sum(x * x, axis=-1, keepdims=True) + EPS)


def _dot(a, b):
    return jnp.dot(a, b, preferred_element_type=F32)


def _dot_nt(a, b):
    return lax.dot_general(a, b, (((1,), (1,)), ((), ())), preferred_element_type=F32)


def _dot_tn(a, b):
    return lax.dot_general(a, b, (((0,), (0,)), ((), ())), preferred_element_type=F32)


def _split_bf16(x):
    hi = x.astype(BF16)
    return hi, (x - hi.astype(F32)).astype(BF16)


def _causal_conv(halo, x, w):
    rows, taps = x.shape[0], w.shape[0]
    ext = jnp.concatenate([halo, x], axis=0)
    acc = x * w[taps - 1:taps]
    for i in range(taps - 1):
        lo = SUBLANES - (taps - 1) + i
        acc = acc + ext[lo:lo + rows] * w[i:i + 1]
    return acc


def _inproj_kernel(x_ref, g_ref, w_ref, wl_ref, cw_ref, csc_ref, alog_ref, dtb_ref, head_ref,
                   qkv_ref, z_ref, ysc_ref, gate_ref, tail_ref, halo_ref, *, tiles_per_seq):
    tm = x_ref.shape[0]

    @pl.when(pl.program_id(0) % tiles_per_seq == 0)
    def _sequence_start():
        halo_ref[...] = head_ref[...]

    xb = _rms_norm(x_ref[...], g_ref[...]).astype(BF16)
    group = lambda j: _dot(xb, w_ref[:, GROUP * j:GROUP * (j + 1)])
    blocks = lambda r: [r[:, LANES * c:LANES * (c + 1)] for c in range(HEADS)]

    def conv_block(slot, x, w):
        y = _causal_conv(halo_ref[slot], x, w)
        halo_ref[slot] = x[tm - SUBLANES:tm]
        return y

    for t in range(3):
        for h, x in enumerate(blocks(group(t))):
            y = _silu(conv_block(HEADS * t + h, x, cw_ref[HEADS * t + h]))
            if t == 0:
                y = _l2_normalize(y) * (HEAD_DIM ** -0.5)
            elif t == 1:
                y = _l2_normalize(y)
            qkv_ref[HEADS * t + h] = y.astype(BF16)
    z_ref[...] = group(3)
    sc_x, sc_c = blocks(group(4)), blocks(group(6))
    conv = [conv_block(3 * HEADS + c, sc_c[c] * sc_x[c], csc_ref[c]) for c in range(HEADS)]
    sc_b = blocks(group(5))
    ysc_ref[...] = jnp.concatenate([sc_b[c] * conv[c] for c in range(HEADS)], axis=1).astype(BF16)
    logits = _dot(xb, wl_ref[...])
    lane = lax.broadcasted_iota(jnp.int32, logits.shape, 1)
    gate_ref[...] = jnp.where(lane < HEADS, _sigmoid(logits),
                              -jnp.exp(alog_ref[...]) * _softplus(logits + dtb_ref[...]))
    tail_ref[...] = halo_ref[...]


def _inproj(x2d, head_halo, gain, w_main, w_logit, conv_w, conv_sc, alog_row, dtb_row, seq, tm):
    rows = x2d.shape[0]
    n_tiles = rows // tm
    once = lambda shape: pl.BlockSpec(shape, lambda i: (0,) * len(shape), pipeline_mode=pl.Buffered(1))
    return pl.pallas_call(
        functools.partial(_inproj_kernel, tiles_per_seq=seq // tm),
        grid=(n_tiles,),
        in_specs=[
            pl.BlockSpec((tm, D_MODEL), lambda i: (i, 0)),
            once((1, D_MODEL)),
            once((D_MODEL, N_GROUPS * GROUP)),
            once((D_MODEL, LANES)),
            once((3 * HEADS, 4, LANES)),
            once((HEADS, 3, LANES)),
            once((1, LANES)), once((1, LANES)),
            once((N_HALO, SUBLANES, LANES)),
        ],
        out_specs=[
            pl.BlockSpec((3 * HEADS, tm, LANES), lambda i: (0, i, 0)),
            pl.BlockSpec((tm, GDN_WIDTH), lambda i: (i, 0)),
            pl.BlockSpec((tm, SC_WIDTH), lambda i: (i, 0)),
            pl.BlockSpec((tm, LANES), lambda i: (i, 0)),
            pl.BlockSpec((N_HALO, SUBLANES, LANES), lambda i: (0, 0, 0)),
        ],
        out_shape=[
            jax.ShapeDtypeStruct((3 * HEADS, rows, LANES), BF16),
            jax.ShapeDtypeStruct((rows, GDN_WIDTH), F32),
            jax.ShapeDtypeStruct((rows, SC_WIDTH), BF16),
            jax.ShapeDtypeStruct((rows, LANES), F32),
            jax.ShapeDtypeStruct((N_HALO, SUBLANES, LANES), F32),
        ],
        scratch_shapes=[pltpu.VMEM((N_HALO, SUBLANES, LANES), F32)],
        compiler_params=pltpu.CompilerParams(
            dimension_semantics=("arbitrary",), vmem_limit_bytes=VMEM_LIMIT),
        name="inproj",
    )(x2d, gain, w_main, w_logit, conv_w, conv_sc, alog_row, dtb_row, head_halo)


def _pair_diag(x0, x1):
    z = jnp.zeros_like(x0)
    return jnp.concatenate([jnp.concatenate([x0, z], axis=1), jnp.concatenate([z, x1], axis=1)], axis=0)


def _diag_block_inverse(a2s):
    nblk = CHUNK // INV_BLK
    lane = lax.broadcasted_iota(jnp.int32, (INV_BLK, LANES), 1)
    row = lax.broadcasted_iota(jnp.int32, (INV_BLK, LANES), 0)
    grp = (lane & (CHUNK - 1)) // INV_BLK
    base = (lane - (lane & (INV_BLK - 1)))[:SUBLANES]
    eye = ((lane & (INV_BLK - 1)) == row).astype(F32)
    zs, ds = [], []
    for a2 in a2s:
        z = jnp.zeros((INV_BLK, LANES), F32)
        for m in range(nblk):
            z = jnp.where(grp == m, a2[INV_BLK * m:INV_BLK * (m + 1), :], z)
        zs.append([z[:SUBLANES], z[SUBLANES:]])
        ds.append([eye[:SUBLANES], eye[SUBLANES:]])
    for j in range(INV_BLK - 1):
        idx = base + j
        for z, d in zip(zs, ds):
            pivot = d[j // SUBLANES][j % SUBLANES:j % SUBLANES + 1, :]
            for t in range(2):
                if SUBLANES * (t + 1) - 1 > j:
                    d[t] = d[t] - jnp.take_along_axis(z[t], idx, axis=1) * pivot
    out = []
    for d in ds:
        dz = jnp.concatenate(d, axis=0)
        out.append(jnp.concatenate([jnp.where(grp == m, dz, 0.0) for m in range(nblk)], axis=0))
    return out


def _chunk_factors(chunks):
    row = lax.broadcasted_iota(jnp.int32, (CHUNK, LANES), 0)
    lane = lax.broadcasted_iota(jnp.int32, (CHUNK, LANES), 1)
    col = lane & (CHUNK - 1)
    left = lane < CHUNK
    pick = lambda a, b: jnp.where(left, a, b)
    incl2, strict2 = row >= col, row > col
    tri2 = incl2.astype(BF16)
    eye2 = (row == col).astype(BF16)
    eye_stack = jnp.concatenate([eye2, eye2], axis=0)
    off_diag = row // INV_BLK != col // INV_BLK

    cums = [_dot(tri2, jnp.concatenate(_split_bf16(c[3]), axis=0)) for c in chunks]
    ys = [jnp.where(lane >= HEADS, cum, c[3]) for cum, c in zip(cums, chunks)]
    rows_all = [_dot_tn(jnp.concatenate(_split_bf16(y), axis=0), eye_stack) for y in ys]

    outs, items = [], []
    for (q16, k16, v16, _), y, rows in zip(chunks, ys, rows_all):
        beta_c = [jnp.broadcast_to(y[:, h:h + 1], (CHUNK, LANES)) for h in range(HEADS)]
        gc_c = [jnp.broadcast_to(y[:, HEADS + h:HEADS + h + 1], (CHUNK, LANES)) for h in range(HEADS)]
        out = dict(u=[], w=[], eg=[], dk=[], gl=[], qk=[])
        for h in range(HEADS):
            gc_last = gc_c[h][CHUNK - 1:CHUNK, :]
            out["eg"].append(jnp.exp(gc_c[h]))
            out["dk"].append(jnp.exp(gc_last - gc_c[h]))
            out["gl"].append(jnp.exp(gc_last))
        outs.append(out)
        for h0 in range(0, HEADS, 2):
            h1 = h0 + 1
            beta_r = pick(rows[h0:h0 + 1, :], rows[h1:h1 + 1, :])
            gc_r = pick(rows[HEADS + h0:HEADS + h0 + 1, :], rows[HEADS + h1:HEADS + h1 + 1, :])
            decay = jnp.where(incl2, jnp.exp(jnp.where(incl2, pick(gc_c[h0], gc_c[h1]) - gc_r, 0.0)), 0.0)
            items.append(dict(out=out, beta_r=beta_r, w_scale=beta_r * jnp.exp(gc_r), decay=decay,
                              beta_c=pick(beta_c[h0], beta_c[h1]),
                              k_cat=jnp.concatenate([k16[h0], k16[h1]], axis=1),
                              q_cat=jnp.concatenate([q16[h0], q16[h1]], axis=1),
                              k_bd=_pair_diag(k16[h0], k16[h1]), v_bd=_pair_diag(v16[h0], v16[h1])))

    for it in items:
        it["kk"] = _dot_nt(it["k_cat"], it["k_bd"])
    for it in items:
        qk = _dot_nt(it["q_cat"], it["k_bd"])
        it["out"]["qk"].append(jnp.where(incl2, qk * it["decay"], 0.0).astype(BF16))
    for it in items:
        it["a2"] = jnp.where(strict2, it["beta_c"] * it["kk"] * it["decay"], 0.0)

    for it, xd in zip(items, _diag_block_inverse([it["a2"] for it in items])):
        it["xd"] = xd
    for it in items:
        off = jnp.where(off_diag, it["a2"], 0.0)
        off_bd = jnp.concatenate([jnp.where(left, off, 0.0), jnp.where(left, 0.0, off)], axis=0)
        it["n16"] = _dot(it["xd"].astype(BF16), off_bd.astype(BF16)).astype(BF16)
    for it in items:
        u01 = _dot((it["xd"] * it["beta_r"]).astype(BF16), it["v_bd"])
        w01 = _dot((it["xd"] * it["w_scale"]).astype(BF16), it["k_bd"])
        it["y1"] = jnp.concatenate([u01[:, :LANES], w01[:, :LANES], u01[:, LANES:], w01[:, LANES:]], axis=1)
        it["t"] = it["y1"]
    for _ in range(CHUNK // INV_BLK - 1):
        for it in items:
            t16 = it["t"].astype(BF16)
            it["t"] = it["y1"] - _dot(it["n16"], _pair_diag(t16[:, :2 * LANES], t16[:, 2 * LANES:]))
    for it in items:
        for i in range(2):
            it["out"]["u"].append(it["t"][:, 2 * i * LANES:(2 * i + 1) * LANES])
            it["out"]["w"].append(it["t"][:, (2 * i + 1) * LANES:(2 * i + 2) * LANES].astype(BF16))
    return outs


def _chunk_recurrence(q16, k16, f, states, need_out):
    s16 = [s.astype(BF16) for s in states]
    v_new, q_s = [], []
    for h in range(HEADS):
        ws = _dot(jnp.concatenate([f["w"][h], q16[h]], axis=0), s16[h])
        v_new.append(f["u"][h] - ws[:CHUNK])
        q_s.append(ws[CHUNK:])
    vn16 = [v.astype(BF16) for v in v_new]
    outs = []
    if need_out:
        for h0 in range(0, HEADS, 2):
            intra = _dot(f["qk"][h0 // 2], _pair_diag(vn16[h0], vn16[h0 + 1]))
            for i in range(2):
                outs.append(f["eg"][h0 + i] * q_s[h0 + i] + intra[:, LANES * i:LANES * (i + 1)])
    new_states = [states[h] * f["gl"][h] + _dot_tn(k16[h], (v_new[h] * f["dk"][h]).astype(BF16))
                  for h in range(HEADS)]
    return outs, new_states


_FACTOR_KEYS = ("u", "w", "eg", "dk", "gl")


def _gdn_kernel(q_ref, k_ref, v_ref, gate_ref, s0_ref, o_ref,
                s_ref, u_ref, w_ref, eg_ref, dk_ref, gl_ref, qk_ref, *, n_chunks, unroll):
    f_refs = dict(u=u_ref, w=w_ref, eg=eg_ref, dk=dk_ref, gl=gl_ref)

    @pl.when(pl.program_id(1) == 0)
    def _sequence_start():
        s_ref[...] = s0_ref[...]

    def rows_of(ref, c):
        r0 = pl.multiple_of(c * CHUNK, CHUNK)
        return [ref[h, pl.ds(r0, CHUNK), :] for h in range(HEADS)]

    def factors(i, carry):
        cs = [i * unroll + j for j in range(unroll)]
        fs = _chunk_factors([(rows_of(q_ref, c), rows_of(k_ref, c), rows_of(v_ref, c),
                              gate_ref[pl.ds(pl.multiple_of(c * CHUNK, CHUNK), CHUNK), :]) for c in cs])
        for c, f in zip(cs, fs):
            for h in range(HEADS):
                for key in _FACTOR_KEYS:
                    val = f[key][h]
                    f_refs[key][c, h] = jnp.broadcast_to(val, f_refs[key].shape[2:]) if key == "gl" else val
            for p in range(HEADS // 2):
                qk_ref[c, p] = f["qk"][p]
        return carry

    def recurrence(c, carry):
        r0 = pl.multiple_of(c * CHUNK, CHUNK)
        f = {key: [f_refs[key][c, h] for h in range(HEADS)] for key in _FACTOR_KEYS}
        f["gl"] = [g[:1] for g in f["gl"]]
        f["qk"] = [qk_ref[c, p] for p in range(HEADS // 2)]
        outs, st = _chunk_recurrence(rows_of(q_ref, c), rows_of(k_ref, c), f,
                                     [s_ref[h] for h in range(HEADS)], True)
        for h in range(HEADS):
            o_ref[pl.ds(r0, CHUNK), LANES * h:LANES * (h + 1)] = outs[h]
            s_ref[h] = st[h]
        return carry

    lax.fori_loop(0, n_chunks // unroll, factors, 0)
    lax.fori_loop(0, n_chunks, recurrence, 0)


def _gdn(qkv, gates, state0, batch, seq, ts):
    ns = seq // ts
    nc = ts // CHUNK
    blk = lambda t: pl.BlockSpec((HEADS, ts, LANES), lambda b, s: (t, b * ns + s, 0))
    per_head = lambda rows, dtype: pltpu.VMEM((nc, HEADS, rows, LANES), dtype)
    return pl.pallas_call(
        functools.partial(_gdn_kernel, n_chunks=nc, unroll=min(FACTOR_UNROLL, nc)),
        grid=(batch, ns),
        in_specs=[
            blk(0), blk(1), blk(2),
            pl.BlockSpec((ts, LANES), lambda b, s: (b * ns + s, 0)),
            pl.BlockSpec((HEADS, HEAD_DIM, HEAD_DIM), lambda b, s: (0, 0, 0)),
        ],
        out_specs=pl.BlockSpec((ts, GDN_WIDTH), lambda b, s: (b * ns + s, 0)),
        out_shape=jax.ShapeDtypeStruct((batch * seq, GDN_WIDTH), F32),
        scratch_shapes=[
            pltpu.VMEM((HEADS, HEAD_DIM, HEAD_DIM), F32),
            per_head(CHUNK, F32), per_head(CHUNK, BF16), per_head(CHUNK, F32), per_head(CHUNK, F32),
            per_head(SUBLANES, F32),
            pltpu.VMEM((nc, HEADS // 2, CHUNK, LANES), BF16),
        ],
        compiler_params=pltpu.CompilerParams(
            dimension_semantics=("parallel", "arbitrary"), vmem_limit_bytes=VMEM_LIMIT),
        name="gdn",
    )(qkv, qkv, qkv, gates, state0)


def _gdn_meta_kernel(k_ref, v_ref, gate_ref, s_ref):
    pad = CHUNK - N_META
    front = lambda x: jnp.concatenate([jnp.zeros((pad, LANES), x.dtype), x], axis=0)
    k16 = [front(k_ref[h]) for h in range(HEADS)]
    v16 = [front(v_ref[h]) for h in range(HEADS)]
    f, = _chunk_factors([(k16, k16, v16, front(gate_ref[...]))])
    zero = [jnp.zeros((HEAD_DIM, HEAD_DIM), F32)] * HEADS
    _, st = _chunk_recurrence(k16, k16, f, zero, False)
    for h in range(HEADS):
        s_ref[h] = st[h]


def _gdn_meta(qkv_meta, gates_meta):
    blk = lambda t: pl.BlockSpec((HEADS, N_META, LANES), lambda i: (t, 0, 0))
    return pl.pallas_call(
        _gdn_meta_kernel,
        grid=(1,),
        in_specs=[blk(1), blk(2), pl.BlockSpec((N_META, LANES), lambda i: (0, 0))],
        out_specs=pl.BlockSpec((HEADS, HEAD_DIM, HEAD_DIM), lambda i: (0, 0, 0)),
        out_shape=jax.ShapeDtypeStruct((HEADS, HEAD_DIM, HEAD_DIM), F32),
        name="gdn_meta",
    )(qkv_meta, qkv_meta, gates_meta)


def _post_kernel(x_ref, o_ref, z_ref, ysc_ref, gn_ref, wo_ref, g_post_ref, g_pre_ref, wg_ref, wu_ref,
                 wd_ref, g_fpost_ref, out_ref, *, ff_tile):
    tm = x_ref.shape[0]
    gn = gn_ref[...]
    heads = []
    for h in range(HEADS):
        sl = slice(HEAD_DIM * h, HEAD_DIM * (h + 1))
        heads.append((_rms_norm(o_ref[:, sl], gn) * _silu(z_ref[:, sl])).astype(BF16))
    mixed = jnp.concatenate(heads + [ysc_ref[...]], axis=1)
    h1 = x_ref[...] + _rms_norm(_dot(mixed, wo_ref[...]), g_post_ref[...])
    u = _rms_norm(h1, g_pre_ref[...]).astype(BF16)
    ffn = jnp.zeros((tm, D_MODEL), F32)
    for j in range(D_FF // ff_tile):
        sl = slice(ff_tile * j, ff_tile * (j + 1))
        act = (_silu(_dot(u, wg_ref[:, sl])) * _dot(u, wu_ref[:, sl])).astype(BF16)
        ffn = ffn + _dot(act, wd_ref[sl, :])
    out_ref[...] = h1 + _rms_norm(ffn, g_fpost_ref[...])


def _post(x2d, o, z, ysc, gnorm, w_out, g_post, g_pre, w_gate, w_up, w_down, g_fpost, tm):
    rows = x2d.shape[0]
    row_blk = lambda width: pl.BlockSpec((tm, width), lambda i: (i, 0))
    once = lambda shape: pl.BlockSpec(shape, lambda i: (0, 0), pipeline_mode=pl.Buffered(1))
    return pl.pallas_call(
        functools.partial(_post_kernel, ff_tile=704),
        grid=(rows // tm,),
        in_specs=[
            row_blk(D_MODEL), row_blk(GDN_WIDTH), row_blk(GDN_WIDTH), row_blk(SC_WIDTH),
            once((1, HEAD_DIM)),
            once((D_MODEL, D_MODEL)), once((1, D_MODEL)), once((1, D_MODEL)),
            once((D_MODEL, D_FF)), once((D_MODEL, D_FF)), once((D_FF, D_MODEL)), once((1, D_MODEL)),
        ],
        out_specs=row_blk(D_MODEL),
        out_shape=jax.ShapeDtypeStruct((rows, D_MODEL), F32),
        compiler_params=pltpu.CompilerParams(
            dimension_semantics=("parallel",), vmem_limit_bytes=VMEM_LIMIT),
        name="post",
    )(x2d, o, z, ysc, gnorm, w_out, g_post, g_pre, w_gate, w_up, w_down, g_fpost)


def kernel(x, meta_tokens, mix_pre_norm, mix_post_norm, ffn_pre_norm, ffn_post_norm, w_in, conv_qkv,
           a_log, dt_bias, gdn_norm, conv_sc, w_out, w_gate, w_up, w_down):
    batch, seq, _ = x.shape
    assert mix_pre_norm.shape[0] == 1, "single-layer block"
    n_logit = 2 * HEADS
    cut = 4 * GDN_WIDTH
    w_in0 = w_in[0]
    w_main = jnp.concatenate([w_in0[:, :cut], w_in0[:, cut + n_logit:]], axis=1).astype(BF16)
    w_logit = jnp.pad(w_in0[:, cut:cut + n_logit], ((0, 0), (0, LANES - n_logit))).astype(BF16)
    lane_pad = lambda v: jnp.pad(v.reshape(1, HEADS), ((0, 0), (HEADS, LANES - 2 * HEADS)))
    conv_w = conv_qkv[0].reshape(4, 3 * HEADS, LANES).transpose(1, 0, 2)
    conv_s = conv_sc[0].reshape(3, HEADS, LANES).transpose(1, 0, 2)
    row = lambda v: v[0].reshape(1, -1)
    proj_args = (row(mix_pre_norm), w_main, w_logit, conv_w, conv_s, lane_pad(a_log[0]), lane_pad(dt_bias[0]))

    x2d = x.reshape(batch * seq, D_MODEL)
    tm = min(512, seq)
    no_halo = jnp.zeros((N_HALO, SUBLANES, LANES), F32)
    qkv_m, _, _, gates_m, tail_m = _inproj(meta_tokens, no_halo, *proj_args, N_META, N_META)
    state0 = _gdn_meta(qkv_m, gates_m)
    qkv, z, ysc, gates, _ = _inproj(x2d, tail_m, *proj_args, seq, tm)
    o = _gdn(qkv, gates, state0, batch, seq, min(512, seq))
    out = _post(x2d, o, z, ysc, row(gdn_norm), w_out[0].astype(BF16), row(mix_post_norm),
                row(ffn_pre_norm), w_gate[0].astype(BF16), w_up[0].astype(BF16),
                w_down[0].astype(BF16), row(ffn_post_norm), tm)
    return out.reshape(batch, seq, D_MODEL)
```

```python
import functools

import jax
import jax.numpy as jnp
from jax import lax
from jax.experimental import pallas as pl
from jax.experimental.pallas import tpu as pltpu

F32 = jnp.float32
BF16 = jnp.bfloat16

D_MODEL = 1024
N_META = 16
HEADS = 4
HEAD_DIM = 128
GDN_WIDTH = HEADS * HEAD_DIM
SC_WIDTH = D_MODEL - GDN_WIDTH
CHUNK = 64
D_FF = 2816
EPS = 1e-6
LANES = 128
SUBLANES = 8
GROUP = 4 * LANES
N_GROUPS = 7
N_HALO = 4 * HEADS
INV_BLK = 16
FACTOR_UNROLL = 8
VMEM_LIMIT = 56 * 1024 * 1024


def _sigmoid(x):
    return 1.0 / (1.0 + jnp.exp(-x))


def _silu(x):
    return x * _sigmoid(x)


def _softplus(x):
    return jnp.maximum(x, 0.0) + jnp.log1p(jnp.exp(-jnp.abs(x)))


def _rms_norm(x, gain):
    return x * lax.rsqrt(jnp.mean(x * x, axis=-1, keepdims=True) + EPS) * gain


def _l2_normalize(x):
    return x * lax.rsqrt(jnp.sum(x * x, axis=-1, keepdims=True) + EPS)


def _dot(a, b):
    return jnp.dot(a, b, preferred_element_type=F32)


def _dot_nt(a, b):
    return lax.dot_general(a, b, (((1,), (1,)), ((), ())), preferred_element_type=F32)


def _dot_tn(a, b):
    return lax.dot_general(a, b, (((0,), (0,)), ((), ())), preferred_element_type=F32)


def _split_bf16(x):
    hi = x.astype(BF16)
    return hi, (x - hi.astype(F32)).astype(BF16)


def _causal_conv(ext_ref, x, w):
    rows, taps = x.shape[0], w.shape[0]
    ext_ref[pl.ds(SUBLANES, rows), :] = x
    acc = x * w[taps - 1:taps]
    for i in range(taps - 1):
        acc = acc + ext_ref[pl.ds(SUBLANES - (taps - 1) + i, rows), :] * w[i:i + 1]
    ext_ref[pl.ds(0, SUBLANES), :] = x[rows - SUBLANES:rows]
    return acc


def _inproj_kernel(x_ref, g_ref, w_ref, wl_ref, cw_ref, csc_ref, alog_ref, dtb_ref, head_ref,
                   qkv_ref, z_ref, ysc_ref, gate_ref, tail_ref, ext_ref, *, tiles_per_seq):
    @pl.when(pl.program_id(0) % tiles_per_seq == 0)
    def _sequence_start():
        ext_ref[:, pl.ds(0, SUBLANES), :] = head_ref[...]

    xb = _rms_norm(x_ref[...], g_ref[...]).astype(BF16)
    group = lambda j: _dot(xb, w_ref[:, GROUP * j:GROUP * (j + 1)])
    blocks = lambda r: [r[:, LANES * c:LANES * (c + 1)] for c in range(HEADS)]

    conv_block = lambda slot, x, w: _causal_conv(ext_ref.at[slot], x, w)

    for t in range(3):
        for h, x in enumerate(blocks(group(t))):
            y = _silu(conv_block(HEADS * t + h, x, cw_ref[HEADS * t + h]))
            if t == 0:
                y = _l2_normalize(y) * (HEAD_DIM ** -0.5)
            elif t == 1:
                y = _l2_normalize(y)
            qkv_ref[HEADS * t + h] = y.astype(BF16)
    z_ref[...] = group(3)
    sc_x, sc_c = blocks(group(4)), blocks(group(6))
    conv = [conv_block(3 * HEADS + c, sc_c[c] * sc_x[c], csc_ref[c]) for c in range(HEADS)]
    sc_b = blocks(group(5))
    ysc_ref[...] = jnp.concatenate([sc_b[c] * conv[c] for c in range(HEADS)], axis=1).astype(BF16)
    logits = _dot(xb, wl_ref[...])
    lane = lax.broadcasted_iota(jnp.int32, logits.shape, 1)
    gate_ref[...] = jnp.where(lane < HEADS, _sigmoid(logits),
                              -jnp.exp(alog_ref[...]) * _softplus(logits + dtb_ref[...]))
    tail_ref[...] = ext_ref[:, pl.ds(0, SUBLANES), :]


def _inproj(x2d, head_halo, gain, w_main, w_logit, conv_w, conv_sc, alog_row, dtb_row, seq, tm):
    rows = x2d.shape[0]
    n_tiles = rows // tm
    once = lambda shape: pl.BlockSpec(shape, lambda i: (0,) * len(shape), pipeline_mode=pl.Buffered(1))
    return pl.pallas_call(
        functools.partial(_inproj_kernel, tiles_per_seq=seq // tm),
        grid=(n_tiles,),
        in_specs=[
            pl.BlockSpec((tm, D_MODEL), lambda i: (i, 0)),
            once((1, D_MODEL)),
            once((D_MODEL, N_GROUPS * GROUP)),
            once((D_MODEL, LANES)),
            once((3 * HEADS, 4, LANES)),
            once((HEADS, 3, LANES)),
            once((1, LANES)), once((1, LANES)),
            once((N_HALO, SUBLANES, LANES)),
        ],
        out_specs=[
            pl.BlockSpec((3 * HEADS, tm, LANES), lambda i: (0, i, 0)),
            pl.BlockSpec((tm, GDN_WIDTH), lambda i: (i, 0)),
            pl.BlockSpec((tm, SC_WIDTH), lambda i: (i, 0)),
            pl.BlockSpec((tm, LANES), lambda i: (i, 0)),
            pl.BlockSpec((N_HALO, SUBLANES, LANES), lambda i: (0, 0, 0)),
        ],
        out_shape=[
            jax.ShapeDtypeStruct((3 * HEADS, rows, LANES), BF16),
            jax.ShapeDtypeStruct((rows, GDN_WIDTH), F32),
            jax.ShapeDtypeStruct((rows, SC_WIDTH), BF16),
            jax.ShapeDtypeStruct((rows, LANES), F32),
            jax.ShapeDtypeStruct((N_HALO, SUBLANES, LANES), F32),
        ],
        scratch_shapes=[pltpu.VMEM((N_HALO, SUBLANES + tm, LANES), F32)],
        compiler_params=pltpu.CompilerParams(
            dimension_semantics=("arbitrary",), vmem_limit_bytes=VMEM_LIMIT),
        name="inproj",
    )(x2d, gain, w_main, w_logit, conv_w, conv_sc, alog_row, dtb_row, head_halo)


def _pair_diag(x0, x1):
    z = jnp.zeros_like(x0)
    return jnp.concatenate([jnp.concatenate([x0, z], axis=1), jnp.concatenate([z, x1], axis=1)], axis=0)


def _diag_block_inverse(a2s):
    nblk = CHUNK // INV_BLK
    lane = lax.broadcasted_iota(jnp.int32, (INV_BLK, LANES), 1)
    row = lax.broadcasted_iota(jnp.int32, (INV_BLK, LANES), 0)
    grp = (lane & (CHUNK - 1)) // INV_BLK
    base = (lane - (lane & (INV_BLK - 1)))[:SUBLANES]
    eye = ((lane & (INV_BLK - 1)) == row).astype(F32)
    zs, ds = [], []
    for a2 in a2s:
        z = jnp.zeros((INV_BLK, LANES), F32)
        for m in range(nblk):
            z = jnp.where(grp == m, a2[INV_BLK * m:INV_BLK * (m + 1), :], z)
        zs.append([z[:SUBLANES], z[SUBLANES:]])
        ds.append([eye[:SUBLANES], eye[SUBLANES:]])
    for j in range(INV_BLK - 1):
        idx = base + j
        for z, d in zip(zs, ds):
            pivot = d[j // SUBLANES][j % SUBLANES:j % SUBLANES + 1, :]
            for t in range(2):
                if SUBLANES * (t + 1) - 1 > j:
                    d[t] = d[t] - jnp.take_along_axis(z[t], idx, axis=1) * pivot
    out = []
    for d in ds:
        dz = jnp.concatenate(d, axis=0)
        out.append(jnp.concatenate([jnp.where(grp == m, dz, 0.0) for m in range(nblk)], axis=0))
    return out


def _chunk_factors(chunks):
    row = lax.broadcasted_iota(jnp.int32, (CHUNK, LANES), 0)
    lane = lax.broadcasted_iota(jnp.int32, (CHUNK, LANES), 1)
    col = lane & (CHUNK - 1)
    left = lane < CHUNK
    pick = lambda a, b: jnp.where(left, a, b)
    incl2, strict2 = row >= col, row > col
    tri2 = incl2.astype(BF16)
    eye2 = (row == col).astype(BF16)
    eye_stack = jnp.concatenate([eye2, eye2], axis=0)
    off_diag = row // INV_BLK != col // INV_BLK

    cums = [_dot(tri2, jnp.concatenate(_split_bf16(c[3]), axis=0)) for c in chunks]
    ys = [jnp.where(lane >= HEADS, cum, c[3]) for cum, c in zip(cums, chunks)]
    rows_all = [_dot_tn(jnp.concatenate(_split_bf16(y), axis=0), eye_stack) for y in ys]

    outs, items = [], []
    for (q16, k16, v16, _), y, rows in zip(chunks, ys, rows_all):
        beta_c = [jnp.broadcast_to(y[:, h:h + 1], (CHUNK, LANES)) for h in range(HEADS)]
        gc_c = [jnp.broadcast_to(y[:, HEADS + h:HEADS + h + 1], (CHUNK, LANES)) for h in range(HEADS)]
        out = dict(u=[], w=[], eg=[], dk=[], gl=[], qk=[])
        for h in range(HEADS):
            gc_last = gc_c[h][CHUNK - 1:CHUNK, :]
            out["eg"].append(jnp.exp(gc_c[h]))
            out["dk"].append(jnp.exp(gc_last - gc_c[h]))
            out["gl"].append(jnp.exp(gc_last))
        outs.append(out)
        for h0 in range(0, HEADS, 2):
            h1 = h0 + 1
            beta_r = pick(rows[h0:h0 + 1, :], rows[h1:h1 + 1, :])
            gc_r = pick(rows[HEADS + h0:HEADS + h0 + 1, :], rows[HEADS + h1:HEADS + h1 + 1, :])
            decay = jnp.where(incl2, jnp.exp(jnp.where(incl2, pick(gc_c[h0], gc_c[h1]) - gc_r, 0.0)), 0.0)
            items.append(dict(out=out, beta_r=beta_r, w_scale=beta_r * jnp.exp(gc_r), decay=decay,
                              beta_c=pick(beta_c[h0], beta_c[h1]),
                              k_cat=jnp.concatenate([k16[h0], k16[h1]], axis=1),
                              q_cat=jnp.concatenate([q16[h0], q16[h1]], axis=1),
                              k_bd=_pair_diag(k16[h0], k16[h1]), v_bd=_pair_diag(v16[h0], v16[h1])))

    halves_diag = lambda x: jnp.concatenate([jnp.where(left, x, 0.0), jnp.where(left, 0.0, x)], axis=0)

    for it in items:
        kq = _dot_nt(jnp.concatenate([it["k_cat"], it["q_cat"]], axis=0), it["k_bd"])
        it["a2"] = jnp.where(strict2, it["beta_c"] * kq[:CHUNK] * it["decay"], 0.0)
        it["out"]["qk"].append(jnp.where(incl2, kq[CHUNK:] * it["decay"], 0.0).astype(BF16))

    for it, xd in zip(items, _diag_block_inverse([it["a2"] for it in items])):
        it["xd"] = xd
    for it in items:
        off_bd = halves_diag(jnp.where(off_diag, it["a2"], 0.0))
        it["n16"] = _dot(it["xd"].astype(BF16), off_bd.astype(BF16)).astype(BF16)
        it["t"] = it["xd"]
    for _ in range(CHUNK // INV_BLK - 1):
        for it in items:
            it["t"] = it["xd"] - _dot(it["n16"], halves_diag(it["t"]).astype(BF16))
    for it in items:
        u01 = _dot((it["t"] * it["beta_r"]).astype(BF16), it["v_bd"])
        w01 = _dot((it["t"] * it["w_scale"]).astype(BF16), it["k_bd"])
        for i in range(2):
            it["out"]["u"].append(u01[:, LANES * i:LANES * (i + 1)])
            it["out"]["w"].append(w01[:, LANES * i:LANES * (i + 1)].astype(BF16))
    return outs


def _chunk_recurrence(q16, k16, f, states, need_out):
    s16 = [s.astype(BF16) for s in states]
    v_new, q_s = [], []
    for h in range(HEADS):
        ws = _dot(jnp.concatenate([f["w"][h], q16[h]], axis=0), s16[h])
        v_new.append(f["u"][h] - ws[:CHUNK])
        q_s.append(ws[CHUNK:])
    vn16 = [v.astype(BF16) for v in v_new]
    outs = []
    if need_out:
        for h0 in range(0, HEADS, 2):
            intra = _dot(f["qk"][h0 // 2], _pair_diag(vn16[h0], vn16[h0 + 1]))
            for i in range(2):
                outs.append(f["eg"][h0 + i] * q_s[h0 + i] + intra[:, LANES * i:LANES * (i + 1)])
    new_states = [states[h] * f["gl"][h] + _dot_tn(k16[h], (v_new[h] * f["dk"][h]).astype(BF16))
                  for h in range(HEADS)]
    return outs, new_states


_FACTOR_KEYS = ("u", "w", "eg", "dk", "gl")


def _gdn_kernel(q_ref, k_ref, v_ref, gate_ref, s0_ref, o_ref,
                s_ref, u_ref, w_ref, eg_ref, dk_ref, gl_ref, qk_ref, *, n_chunks, unroll):
    f_refs = dict(u=u_ref, w=w_ref, eg=eg_ref, dk=dk_ref, gl=gl_ref)

    @pl.when(pl.program_id(1) == 0)
    def _sequence_start():
        s_ref[...] = s0_ref[...]

    def rows_of(ref, c):
        r0 = pl.multiple_of(c * CHUNK, CHUNK)
        return [ref[h, pl.ds(r0, CHUNK), :] for h in range(HEADS)]

    def factors(i, carry):
        cs = [i * unroll + j for j in range(unroll)]
        fs = _chunk_factors([(rows_of(q_ref, c), rows_of(k_ref, c), rows_of(v_ref, c),
                              gate_ref[pl.ds(pl.multiple_of(c * CHUNK, CHUNK), CHUNK), :]) for c in cs])
        for c, f in zip(cs, fs):
            for h in range(HEADS):
                for key in _FACTOR_KEYS:
                    val = f[key][h]
                    f_refs[key][c, h] = jnp.broadcast_to(val, f_refs[key].shape[2:]) if key == "gl" else val
            for p in range(HEADS // 2):
                qk_ref[c, p] = f["qk"][p]
        return carry

    def recurrence(c, carry):
        r0 = pl.multiple_of(c * CHUNK, CHUNK)
        f = {key: [f_refs[key][c, h] for h in range(HEADS)] for key in _FACTOR_KEYS}
        f["gl"] = [g[:1] for g in f["gl"]]
        f["qk"] = [qk_ref[c, p] for p in range(HEADS // 2)]
        outs, st = _chunk_recurrence(rows_of(q_ref, c), rows_of(k_ref, c), f,
                                     [s_ref[h] for h in range(HEADS)], True)
        for h in range(HEADS):
            o_ref[pl.ds(r0, CHUNK), LANES * h:LANES * (h + 1)] = outs[h]
            s_ref[h] = st[h]
        return carry

    lax.fori_loop(0, n_chunks // unroll, factors, 0)
    lax.fori_loop(0, n_chunks, recurrence, 0)


def _gdn(qkv, gates, state0, batch, seq, ts):
    ns = seq // ts
    nc = ts // CHUNK
    blk = lambda t: pl.BlockSpec((HEADS, ts, LANES), lambda b, s: (t, b * ns + s, 0))
    per_head = lambda rows, dtype: pltpu.VMEM((nc, HEADS, rows, LANES), dtype)
    return pl.pallas_call(
        functools.partial(_gdn_kernel, n_chunks=nc, unroll=min(FACTOR_UNROLL, nc)),
        grid=(batch, ns),
        in_specs=[
            blk(0), blk(1), blk(2),
            pl.BlockSpec((ts, LANES), lambda b, s: (b * ns + s, 0)),
            pl.BlockSpec((HEADS, HEAD_DIM, HEAD_DIM), lambda b, s: (0, 0, 0)),
        ],
        out_specs=pl.BlockSpec((ts, GDN_WIDTH), lambda b, s: (b * ns + s, 0)),
        out_shape=jax.ShapeDtypeStruct((batch * seq, GDN_WIDTH), F32),
        scratch_shapes=[
            pltpu.VMEM((HEADS, HEAD_DIM, HEAD_DIM), F32),
            per_head(CHUNK, F32), per_head(CHUNK, BF16), per_head(CHUNK, F32), per_head(CHUNK, F32),
            per_head(SUBLANES, F32),
            pltpu.VMEM((nc, HEADS // 2, CHUNK, LANES), BF16),
        ],
        compiler_params=pltpu.CompilerParams(
            dimension_semantics=("parallel", "arbitrary"), vmem_limit_bytes=VMEM_LIMIT),
        name="gdn",
    )(qkv, qkv, qkv, gates, state0)


def _gdn_meta_kernel(k_ref, v_ref, gate_ref, s_ref):
    pad = CHUNK - N_META
    front = lambda x: jnp.concatenate([jnp.zeros((pad, LANES), x.dtype), x], axis=0)
    k16 = [front(k_ref[h]) for h in range(HEADS)]
    v16 = [front(v_ref[h]) for h in range(HEADS)]
    f, = _chunk_factors([(k16, k16, v16, front(gate_ref[...]))])
    zero = [jnp.zeros((HEAD_DIM, HEAD_DIM), F32)] * HEADS
    _, st = _chunk_recurrence(k16, k16, f, zero, False)
    for h in range(HEADS):
        s_ref[h] = st[h]


def _gdn_meta(qkv_meta, gates_meta):
    blk = lambda t: pl.BlockSpec((HEADS, N_META, LANES), lambda i: (t, 0, 0))
    return pl.pallas_call(
        _gdn_meta_kernel,
        grid=(1,),
        in_specs=[blk(1), blk(2), pl.BlockSpec((N_META, LANES), lambda i: (0, 0))],
        out_specs=pl.BlockSpec((HEADS, HEAD_DIM, HEAD_DIM), lambda i: (0, 0, 0)),
        out_shape=jax.ShapeDtypeStruct((HEADS, HEAD_DIM, HEAD_DIM), F32),
        name="gdn_meta",
    )(qkv_meta, qkv_meta, gates_meta)


def _post_kernel(x_ref, o_ref, z_ref, ysc_ref, gn_ref, wo_ref, g_post_ref, g_pre_ref, wg_ref, wu_ref,
                 wd_ref, g_fpost_ref, out_ref, *, ff_tile):
    tm = x_ref.shape[0]
    gn = gn_ref[...]
    heads = []
    for h in range(HEADS):
        sl = slice(HEAD_DIM * h, HEAD_DIM * (h + 1))
        heads.append((_rms_norm(o_ref[:, sl], gn) * _silu(z_ref[:, sl])).astype(BF16))
    mixed = jnp.concatenate(heads + [ysc_ref[...]], axis=1)
    h1 = x_ref[...] + _rms_norm(_dot(mixed, wo_ref[...]), g_post_ref[...])
    u = _rms_norm(h1, g_pre_ref[...]).astype(BF16)
    ffn = jnp.zeros((tm, D_MODEL), F32)
    for j in range(D_FF // ff_tile):
        sl = slice(ff_tile * j, ff_tile * (j + 1))
        act = (_silu(_dot(u, wg_ref[:, sl])) * _dot(u, wu_ref[:, sl])).astype(BF16)
        ffn = ffn + _dot(act, wd_ref[sl, :])
    out_ref[...] = h1 + _rms_norm(ffn, g_fpost_ref[...])


def _post(x2d, o, z, ysc, gnorm, w_out, g_post, g_pre, w_gate, w_up, w_down, g_fpost, tm):
    rows = x2d.shape[0]
    row_blk = lambda width: pl.BlockSpec((tm, width), lambda i: (i, 0))
    once = lambda shape: pl.BlockSpec(shape, lambda i: (0, 0), pipeline_mode=pl.Buffered(1))
    return pl.pallas_call(
        functools.partial(_post_kernel, ff_tile=704),
        grid=(rows // tm,),
        in_specs=[
            row_blk(D_MODEL), row_blk(GDN_WIDTH), row_blk(GDN_WIDTH), row_blk(SC_WIDTH),
            once((1, HEAD_DIM)),
            once((D_MODEL, D_MODEL)), once((1, D_MODEL)), once((1, D_MODEL)),
            once((D_MODEL, D_FF)), once((D_MODEL, D_FF)), once((D_FF, D_MODEL)), once((1, D_MODEL)),
        ],
        out_specs=row_blk(D_MODEL),
        out_shape=jax.ShapeDtypeStruct((rows, D_MODEL), F32),
        compiler_params=pltpu.CompilerParams(
            dimension_semantics=("parallel",), vmem_limit_bytes=VMEM_LIMIT),
        name="post",
    )(x2d, o, z, ysc, gnorm, w_out, g_post, g_pre, w_gate, w_up, w_down, g_fpost)


def kernel(x, meta_tokens, mix_pre_norm, mix_post_norm, ffn_pre_norm, ffn_post_norm, w_in, conv_qkv,
           a_log, dt_bias, gdn_norm, conv_sc, w_out, w_gate, w_up, w_down):
    batch, seq, _ = x.shape
    assert mix_pre_norm.shape[0] == 1, "single-layer block"
    n_logit = 2 * HEADS
    cut = 4 * GDN_WIDTH
    w_in0 = w_in[0]
    w_main = jnp.concatenate([w_in0[:, :cut], w_in0[:, cut + n_logit:]], axis=1).astype(BF16)
    w_logit = jnp.pad(w_in0[:, cut:cut + n_logit], ((0, 0), (0, LANES - n_logit))).astype(BF16)
    lane_pad = lambda v: jnp.pad(v.reshape(1, HEADS), ((0, 0), (HEADS, LANES - 2 * HEADS)))
    conv_w = conv_qkv[0].reshape(4, 3 * HEADS, LANES).transpose(1, 0, 2)
    conv_s = conv_sc[0].reshape(3, HEADS, LANES).transpose(1, 0, 2)
    row = lambda v: v[0].reshape(1, -1)
    proj_args = (row(mix_pre_norm), w_main, w_logit, conv_w, conv_s, lane_pad(a_log[0]), lane_pad(dt_bias[0]))

    x2d = x.reshape(batch * seq, D_MODEL)
    tm = min(512, seq)
    no_halo = jnp.zeros((N_HALO, SUBLANES, LANES), F32)
    qkv_m, _, _, gates_m, tail_m = _inproj(meta_tokens, no_halo, *proj_args, N_META, N_META)
    state0 = _gdn_meta(qkv_m, gates_m)
    qkv, z, ysc, gates, _ = _inproj(x2d, tail_m, *proj_args, seq, tm)
    o = _gdn(qkv, gates, state0, batch, seq, min(512, seq))
    out = _post(x2d, o, z, ysc, row(gdn_norm), w_out[0].astype(BF16), row(mix_post_norm),
                row(ffn_pre_norm), w_gate[0].astype(BF16), w_up[0].astype(BF16),
                w_down[0].astype(BF16), row(ffn_post_norm), tm)
    return out.reshape(batch, seq, D_MODEL)
```

```python
import functools

import jax
import jax.numpy as jnp
from jax import lax
from jax.experimental import pallas as pl
from jax.experimental.pallas import tpu as pltpu

F32 = jnp.float32
BF16 = jnp.bfloat16

D_MODEL = 1024
N_META = 16
HEADS = 4
HEAD_DIM = 128
GDN_WIDTH = HEADS * HEAD_DIM
SC_WIDTH = D_MODEL - GDN_WIDTH
CHUNK = 64
D_FF = 2816
EPS = 1e-6
LANES = 128
SUBLANES = 8
GROUP = 4 * LANES
N_GROUPS = 7
N_HALO = 4 * HEADS
INV_BLK = 16
FF_TILE = 768
GDN_STREAMS = 2
VMEM_LIMIT = 56 * 1024 * 1024


def _sigmoid(x):
    return 1.0 / (1.0 + jnp.exp(-x))


def _silu(x):
    return x * _sigmoid(x)


def _softplus(x):
    return jnp.maximum(x, 0.0) + jnp.log1p(jnp.exp(-jnp.abs(x)))


def _rms_norm(x, gain):
    return x * lax.rsqrt(jnp.mean(x * x, axis=-1, keepdims=True) + EPS) * gain


def _l2_normalize(x):
    return x * lax.rsqrt(jnp.sum(x * x, axis=-1, keepdims=True) + EPS)


def _dot(a, b):
    return jnp.dot(a, b, preferred_element_type=F32)


def _dot_nt(a, b):
    return lax.dot_general(a, b, (((1,), (1,)), ((), ())), preferred_element_type=F32)


def _dot_tn(a, b):
    return lax.dot_general(a, b, (((0,), (0,)), ((), ())), preferred_element_type=F32)


def _split_bf16(x):
    hi = x.astype(BF16)
    return hi, (x - hi.astype(F32)).astype(BF16)


def _causal_conv(ext_ref, x, w):
    rows, taps = x.shape[0], w.shape[0]
    ext_ref[pl.ds(SUBLANES, rows), :] = x
    acc = x * w[taps - 1:taps]
    for i in range(taps - 1):
        acc = acc + ext_ref[pl.ds(SUBLANES - (taps - 1) + i, rows), :] * w[i:i + 1]
    ext_ref[pl.ds(0, SUBLANES), :] = x[rows - SUBLANES:rows]
    return acc


def _inproj_kernel(x_ref, g_ref, w_ref, wl_ref, cw_ref, csc_ref, alog_ref, dtb_ref, head_ref,
                   qkv_ref, z_ref, ysc_ref, gate_ref, tail_ref, ext_ref, *, tiles_per_seq):
    @pl.when(pl.program_id(0) % tiles_per_seq == 0)
    def _sequence_start():
        ext_ref[:, pl.ds(0, SUBLANES), :] = head_ref[...]

    xb = _rms_norm(x_ref[...], g_ref[...]).astype(BF16)
    group = lambda j: _dot(xb, w_ref[:, GROUP * j:GROUP * (j + 1)])
    blocks = lambda r: [r[:, LANES * c:LANES * (c + 1)] for c in range(HEADS)]

    conv_block = lambda slot, x, w: _causal_conv(ext_ref.at[slot], x, w)

    for t in range(3):
        for h, x in enumerate(blocks(group(t))):
            y = _silu(conv_block(HEADS * t + h, x, cw_ref[HEADS * t + h]))
            if t == 0:
                y = _l2_normalize(y) * (HEAD_DIM ** -0.5)
            elif t == 1:
                y = _l2_normalize(y)
            qkv_ref[HEADS * t + h] = y.astype(BF16)
    z_ref[...] = group(3)
    sc_x, sc_c = blocks(group(4)), blocks(group(6))
    conv = [conv_block(3 * HEADS + c, sc_c[c] * sc_x[c], csc_ref[c]) for c in range(HEADS)]
    sc_b = blocks(group(5))
    ysc_ref[...] = jnp.concatenate([sc_b[c] * conv[c] for c in range(HEADS)], axis=1).astype(BF16)
    logits = _dot(xb, wl_ref[...])
    lane = lax.broadcasted_iota(jnp.int32, logits.shape, 1)
    gate_ref[...] = jnp.where(lane < HEADS, _sigmoid(logits),
                              -jnp.exp(alog_ref[...]) * _softplus(logits + dtb_ref[...]))
    tail_ref[...] = ext_ref[:, pl.ds(0, SUBLANES), :]


def _inproj(x2d, head_halo, gain, w_main, w_logit, conv_w, conv_sc, alog_row, dtb_row, seq, tm):
    rows = x2d.shape[0]
    n_tiles = rows // tm
    once = lambda shape: pl.BlockSpec(shape, lambda i: (0,) * len(shape), pipeline_mode=pl.Buffered(1))
    return pl.pallas_call(
        functools.partial(_inproj_kernel, tiles_per_seq=seq // tm),
        grid=(n_tiles,),
        in_specs=[
            pl.BlockSpec((tm, D_MODEL), lambda i: (i, 0)),
            once((1, D_MODEL)),
            once((D_MODEL, N_GROUPS * GROUP)),
            once((D_MODEL, LANES)),
            once((3 * HEADS, 4, LANES)),
            once((HEADS, 3, LANES)),
            once((1, LANES)), once((1, LANES)),
            once((N_HALO, SUBLANES, LANES)),
        ],
        out_specs=[
            pl.BlockSpec((3 * HEADS, tm, LANES), lambda i: (0, i, 0)),
            pl.BlockSpec((tm, GDN_WIDTH), lambda i: (i, 0)),
            pl.BlockSpec((tm, SC_WIDTH), lambda i: (i, 0)),
            pl.BlockSpec((tm, LANES), lambda i: (i, 0)),
            pl.BlockSpec((N_HALO, SUBLANES, LANES), lambda i: (0, 0, 0)),
        ],
        out_shape=[
            jax.ShapeDtypeStruct((3 * HEADS, rows, LANES), BF16),
            jax.ShapeDtypeStruct((rows, GDN_WIDTH), F32),
            jax.ShapeDtypeStruct((rows, SC_WIDTH), BF16),
            jax.ShapeDtypeStruct((rows, LANES), F32),
            jax.ShapeDtypeStruct((N_HALO, SUBLANES, LANES), F32),
        ],
        scratch_shapes=[pltpu.VMEM((N_HALO, SUBLANES + tm, LANES), F32)],
        compiler_params=pltpu.CompilerParams(
            dimension_semantics=("arbitrary",), vmem_limit_bytes=VMEM_LIMIT),
        name="inproj",
    )(x2d, gain, w_main, w_logit, conv_w, conv_sc, alog_row, dtb_row, head_halo)


def _pair_diag(x0, x1):
    z = jnp.zeros_like(x0)
    return jnp.concatenate([jnp.concatenate([x0, z], axis=1), jnp.concatenate([z, x1], axis=1)], axis=0)


def _diag_block_inverse(a2s):
    nblk = CHUNK // INV_BLK
    lane = lax.broadcasted_iota(jnp.int32, (INV_BLK, LANES), 1)
    row = lax.broadcasted_iota(jnp.int32, (INV_BLK, LANES), 0)
    grp = (lane & (CHUNK - 1)) // INV_BLK
    base = (lane - (lane & (INV_BLK - 1)))[:SUBLANES]
    eye = ((lane & (INV_BLK - 1)) == row).astype(F32)
    zs, ds = [], []
    for a2 in a2s:
        z = jnp.zeros((INV_BLK, LANES), F32)
        for m in range(nblk):
            z = jnp.where(grp == m, a2[INV_BLK * m:INV_BLK * (m + 1), :], z)
        zs.append([z[:SUBLANES], z[SUBLANES:]])
        ds.append([eye[:SUBLANES], eye[SUBLANES:]])
    for j in range(INV_BLK - 1):
        idx = base + j
        for z, d in zip(zs, ds):
            pivot = d[j // SUBLANES][j % SUBLANES:j % SUBLANES + 1, :]
            for t in range(2):
                if SUBLANES * (t + 1) - 1 > j:
                    d[t] = d[t] - jnp.take_along_axis(z[t], idx, axis=1) * pivot
    out = []
    for d in ds:
        dz = jnp.concatenate(d, axis=0)
        out.append(jnp.concatenate([jnp.where(grp == m, dz, 0.0) for m in range(nblk)], axis=0))
    return out


def _chunk_factors(chunks):
    row = lax.broadcasted_iota(jnp.int32, (CHUNK, LANES), 0)
    lane = lax.broadcasted_iota(jnp.int32, (CHUNK, LANES), 1)
    col = lane & (CHUNK - 1)
    left = lane < CHUNK
    pick = lambda a, b: jnp.where(left, a, b)
    incl2, strict2 = row >= col, row > col
    tri2 = incl2.astype(BF16)
    eye2 = (row == col).astype(BF16)
    eye_stack = jnp.concatenate([eye2, eye2], axis=0)
    off_diag = row // INV_BLK != col // INV_BLK

    cums = [_dot(tri2, jnp.concatenate(_split_bf16(c[3]), axis=0)) for c in chunks]
    ys = [jnp.where(lane >= HEADS, cum, c[3]) for cum, c in zip(cums, chunks)]
    rows_all = [_dot_tn(jnp.concatenate(_split_bf16(y), axis=0), eye_stack) for y in ys]

    outs, items = [], []
    for (q16, k16, v16, _), y, rows in zip(chunks, ys, rows_all):
        beta_c = [jnp.broadcast_to(y[:, h:h + 1], (CHUNK, LANES)) for h in range(HEADS)]
        gc_c = [jnp.broadcast_to(y[:, HEADS + h:HEADS + h + 1], (CHUNK, LANES)) for h in range(HEADS)]
        out = dict(u=[], w=[], eg=[], dk=[], gl=[], qk=[])
        for h in range(HEADS):
            gc_last = gc_c[h][CHUNK - 1:CHUNK, :]
            out["eg"].append(jnp.exp(gc_c[h]))
            out["dk"].append(jnp.exp(gc_last - gc_c[h]))
            out["gl"].append(jnp.exp(gc_last))
        outs.append(out)
        for h0 in range(0, HEADS, 2):
            h1 = h0 + 1
            beta_r = pick(rows[h0:h0 + 1, :], rows[h1:h1 + 1, :])
            gc_r = pick(rows[HEADS + h0:HEADS + h0 + 1, :], rows[HEADS + h1:HEADS + h1 + 1, :])
            decay = jnp.where(incl2, jnp.exp(jnp.where(incl2, pick(gc_c[h0], gc_c[h1]) - gc_r, 0.0)), 0.0)
            items.append(dict(out=out, beta_r=beta_r, w_scale=beta_r * jnp.exp(gc_r), decay=decay,
                              beta_c=pick(beta_c[h0], beta_c[h1]),
                              k_cat=jnp.concatenate([k16[h0], k16[h1]], axis=1),
                              q_cat=jnp.concatenate([q16[h0], q16[h1]], axis=1),
                              k_bd=_pair_diag(k16[h0], k16[h1]), v_bd=_pair_diag(v16[h0], v16[h1])))

    halves_diag = lambda x: jnp.concatenate([jnp.where(left, x, 0.0), jnp.where(left, 0.0, x)], axis=0)

    for it in items:
        kq = _dot_nt(jnp.concatenate([it["k_cat"], it["q_cat"]], axis=0), it["k_bd"])
        it["a2"] = jnp.where(strict2, it["beta_c"] * kq[:CHUNK] * it["decay"], 0.0)
        it["out"]["qk"].append(jnp.where(incl2, kq[CHUNK:] * it["decay"], 0.0).astype(BF16))

    for it, xd in zip(items, _diag_block_inverse([it["a2"] for it in items])):
        it["xd"] = xd
    for it in items:
        off_bd = halves_diag(jnp.where(off_diag, it["a2"], 0.0))
        it["n16"] = _dot(it["xd"].astype(BF16), off_bd.astype(BF16)).astype(BF16)
        it["t"] = it["xd"]
    for _ in range(CHUNK // INV_BLK - 1):
        for it in items:
            it["t"] = it["xd"] - _dot(it["n16"], halves_diag(it["t"]).astype(BF16))
    for it in items:
        u01 = _dot((it["t"] * it["beta_r"]).astype(BF16), it["v_bd"])
        w01 = _dot((it["t"] * it["w_scale"]).astype(BF16), it["k_bd"])
        for i in range(2):
            it["out"]["u"].append(u01[:, LANES * i:LANES * (i + 1)])
            it["out"]["w"].append(w01[:, LANES * i:LANES * (i + 1)].astype(BF16))
    return outs


def _chunk_recurrence(streams, need_out):
    work = []
    for q16, k16, f, states in streams:
        work.append(dict(q=q16, k=k16, f=f, s=states, s16=[x.astype(BF16) for x in states],
                         v_new=[], q_s=[], outs=[]))
    for t in work:
        for h in range(HEADS):
            ws = _dot(jnp.concatenate([t["f"]["w"][h], t["q"][h]], axis=0), t["s16"][h])
            t["v_new"].append(t["f"]["u"][h] - ws[:CHUNK])
            t["q_s"].append(ws[CHUNK:])
    for t in work:
        t["vn16"] = [v.astype(BF16) for v in t["v_new"]]
        t["vd16"] = [(v * d).astype(BF16) for v, d in zip(t["v_new"], t["f"]["dk"])]
    for t in work:
        t["kv"] = [_dot_tn(t["k"][h], t["vd16"][h]) for h in range(HEADS)]
    if need_out:
        for t in work:
            for h0 in range(0, HEADS, 2):
                intra = _dot(t["f"]["qk"][h0 // 2], _pair_diag(t["vn16"][h0], t["vn16"][h0 + 1]))
                for i in range(2):
                    t["outs"].append(t["f"]["eg"][h0 + i] * t["q_s"][h0 + i]
                                     + intra[:, LANES * i:LANES * (i + 1)])
    return [(t["outs"], [t["s"][h] * t["f"]["gl"][h] + t["kv"][h] for h in range(HEADS)]) for t in work]


_FACTOR_KEYS = ("u", "w", "eg", "dk", "gl")


def _gdn_kernel(q_ref, k_ref, v_ref, gate_ref, s0_ref, o_ref,
                s_ref, u_ref, w_ref, eg_ref, dk_ref, gl_ref, qk_ref, *, n_chunks):
    f_refs = dict(u=u_ref, w=w_ref, eg=eg_ref, dk=dk_ref, gl=gl_ref)

    @pl.when(pl.program_id(1) == 0)
    def _sequence_start():
        for b in range(GDN_STREAMS):
            s_ref[b] = s0_ref[...]

    def rows_of(ref, b, c):
        r0 = pl.multiple_of(c * CHUNK, CHUNK)
        return [ref[h, b, pl.ds(r0, CHUNK), :] for h in range(HEADS)]

    def factors(b, carry):
        fs = _chunk_factors([(rows_of(q_ref, b, c), rows_of(k_ref, b, c), rows_of(v_ref, b, c),
                              gate_ref[b, pl.ds(c * CHUNK, CHUNK), :]) for c in range(n_chunks)])
        for c, f in enumerate(fs):
            slot = b * n_chunks + c
            for h in range(HEADS):
                for key in _FACTOR_KEYS:
                    val = f[key][h]
                    f_refs[key][slot, h] = jnp.broadcast_to(val, f_refs[key].shape[2:]) if key == "gl" else val
            for p in range(HEADS // 2):
                qk_ref[slot, p] = f["qk"][p]
        return carry

    def recurrence(c, carry):
        r0 = pl.multiple_of(c * CHUNK, CHUNK)
        streams = []
        for b in range(GDN_STREAMS):
            slot = b * n_chunks + c
            f = {key: [f_refs[key][slot, h] for h in range(HEADS)] for key in _FACTOR_KEYS}
            f["gl"] = [g[:1] for g in f["gl"]]
            f["qk"] = [qk_ref[slot, p] for p in range(HEADS // 2)]
            streams.append((rows_of(q_ref, b, c), rows_of(k_ref, b, c), f,
                            [s_ref[b, h] for h in range(HEADS)]))
        for b, (outs, st) in enumerate(_chunk_recurrence(streams, True)):
            for h in range(HEADS):
                o_ref[b, pl.ds(r0, CHUNK), LANES * h:LANES * (h + 1)] = outs[h]
                s_ref[b, h] = st[h]
        return carry

    lax.fori_loop(0, GDN_STREAMS, factors, 0)
    lax.fori_loop(0, n_chunks, recurrence, 0)


def _gdn(qkv, gates, state0, batch, seq, ts):
    ns = seq // ts
    nc = ts // CHUNK
    nb = GDN_STREAMS
    blk = lambda t: pl.BlockSpec((HEADS, nb, ts, LANES), lambda b, s: (t, b, s, 0))
    per_head = lambda rows, dtype: pltpu.VMEM((nb * nc, HEADS, rows, LANES), dtype)
    o = pl.pallas_call(
        functools.partial(_gdn_kernel, n_chunks=nc),
        grid=(batch // nb, ns),
        in_specs=[
            blk(0), blk(1), blk(2),
            pl.BlockSpec((nb, ts, LANES), lambda b, s: (b, s, 0)),
            pl.BlockSpec((HEADS, HEAD_DIM, HEAD_DIM), lambda b, s: (0, 0, 0)),
        ],
        out_specs=pl.BlockSpec((nb, ts, GDN_WIDTH), lambda b, s: (b, s, 0)),
        out_shape=jax.ShapeDtypeStruct((batch, seq, GDN_WIDTH), F32),
        scratch_shapes=[
            pltpu.VMEM((nb, HEADS, HEAD_DIM, HEAD_DIM), F32),
            per_head(CHUNK, F32), per_head(CHUNK, BF16), per_head(CHUNK, F32), per_head(CHUNK, F32),
            per_head(SUBLANES, F32),
            pltpu.VMEM((nb * nc, HEADS // 2, CHUNK, LANES), BF16),
        ],
        compiler_params=pltpu.CompilerParams(
            dimension_semantics=("parallel", "arbitrary"), vmem_limit_bytes=VMEM_LIMIT),
        name="gdn",
    )(*([qkv.reshape(3 * HEADS, batch, seq, LANES)] * 3), gates.reshape(batch, seq, LANES), state0)
    return o.reshape(batch * seq, GDN_WIDTH)


def _gdn_meta_kernel(k_ref, v_ref, gate_ref, s_ref):
    pad = CHUNK - N_META
    front = lambda x: jnp.concatenate([jnp.zeros((pad, LANES), x.dtype), x], axis=0)
    k16 = [front(k_ref[h]) for h in range(HEADS)]
    v16 = [front(v_ref[h]) for h in range(HEADS)]
    f, = _chunk_factors([(k16, k16, v16, front(gate_ref[...]))])
    zero = [jnp.zeros((HEAD_DIM, HEAD_DIM), F32)] * HEADS
    (_, st), = _chunk_recurrence([(k16, k16, f, zero)], False)
    for h in range(HEADS):
        s_ref[h] = st[h]


def _gdn_meta(qkv_meta, gates_meta):
    blk = lambda t: pl.BlockSpec((HEADS, N_META, LANES), lambda i: (t, 0, 0))
    return pl.pallas_call(
        _gdn_meta_kernel,
        grid=(1,),
        in_specs=[blk(1), blk(2), pl.BlockSpec((N_META, LANES), lambda i: (0, 0))],
        out_specs=pl.BlockSpec((HEADS, HEAD_DIM, HEAD_DIM), lambda i: (0, 0, 0)),
        out_shape=jax.ShapeDtypeStruct((HEADS, HEAD_DIM, HEAD_DIM), F32),
        name="gdn_meta",
    )(qkv_meta, qkv_meta, gates_meta)


def _post_kernel(x_ref, o_ref, z_ref, ysc_ref, gn_ref, wo_ref, g_post_ref, g_pre_ref, wg_ref, wu_ref,
                 wd_ref, g_fpost_ref, out_ref):
    tm = x_ref.shape[0]
    gn = gn_ref[...]
    heads = []
    for h in range(HEADS):
        sl = slice(HEAD_DIM * h, HEAD_DIM * (h + 1))
        heads.append((_rms_norm(o_ref[:, sl], gn) * _silu(z_ref[:, sl])).astype(BF16))
    mixed = jnp.concatenate(heads + [ysc_ref[...]], axis=1)
    h1 = x_ref[...] + _rms_norm(_dot(mixed, wo_ref[...]), g_post_ref[...])
    u = _rms_norm(h1, g_pre_ref[...]).astype(BF16)
    ffn = jnp.zeros((tm, D_MODEL), F32)
    for lo in range(0, D_FF, FF_TILE):
        sl = slice(lo, min(lo + FF_TILE, D_FF))
        act = (_silu(_dot(u, wg_ref[:, sl])) * _dot(u, wu_ref[:, sl])).astype(BF16)
        ffn = ffn + _dot(act, wd_ref[sl, :])
    out_ref[...] = h1 + _rms_norm(ffn, g_fpost_ref[...])


def _post(x2d, o, z, ysc, gnorm, w_out, g_post, g_pre, w_gate, w_up, w_down, g_fpost, tm):
    rows = x2d.shape[0]
    row_blk = lambda width: pl.BlockSpec((tm, width), lambda i: (i, 0))
    once = lambda shape: pl.BlockSpec(shape, lambda i: (0, 0), pipeline_mode=pl.Buffered(1))
    return pl.pallas_call(
        _post_kernel,
        grid=(rows // tm,),
        in_specs=[
            row_blk(D_MODEL), row_blk(GDN_WIDTH), row_blk(GDN_WIDTH), row_blk(SC_WIDTH),
            once((1, HEAD_DIM)),
            once((D_MODEL, D_MODEL)), once((1, D_MODEL)), once((1, D_MODEL)),
            once((D_MODEL, D_FF)), once((D_MODEL, D_FF)), once((D_FF, D_MODEL)), once((1, D_MODEL)),
        ],
        out_specs=row_blk(D_MODEL),
        out_shape=jax.ShapeDtypeStruct((rows, D_MODEL), F32),
        compiler_params=pltpu.CompilerParams(
            dimension_semantics=("parallel",), vmem_limit_bytes=VMEM_LIMIT),
        name="post",
    )(x2d, o, z, ysc, gnorm, w_out, g_post, g_pre, w_gate, w_up, w_down, g_fpost)


def kernel(x, meta_tokens, mix_pre_norm, mix_post_norm, ffn_pre_norm, ffn_post_norm, w_in, conv_qkv,
           a_log, dt_bias, gdn_norm, conv_sc, w_out, w_gate, w_up, w_down):
    batch, seq, _ = x.shape
    assert mix_pre_norm.shape[0] == 1, "single-layer block"
    assert batch % GDN_STREAMS == 0
    n_logit = 2 * HEADS
    cut = 4 * GDN_WIDTH
    w_in0 = w_in[0]
    w_main = jnp.concatenate([w_in0[:, :cut], w_in0[:, cut + n_logit:]], axis=1).astype(BF16)
    w_logit = jnp.pad(w_in0[:, cut:cut + n_logit], ((0, 0), (0, LANES - n_logit))).astype(BF16)
    lane_pad = lambda v: jnp.pad(v.reshape(1, HEADS), ((0, 0), (HEADS, LANES - 2 * HEADS)))
    conv_w = conv_qkv[0].reshape(4, 3 * HEADS, LANES).transpose(1, 0, 2)
    conv_s = conv_sc[0].reshape(3, HEADS, LANES).transpose(1, 0, 2)
    row = lambda v: v[0].reshape(1, -1)
    proj_args = (row(mix_pre_norm), w_main, w_logit, conv_w, conv_s, lane_pad(a_log[0]), lane_pad(dt_bias[0]))

    x2d = x.reshape(batch * seq, D_MODEL)
    tm = min(512, seq)
    no_halo = jnp.zeros((N_HALO, SUBLANES, LANES), F32)
    qkv_m, _, _, gates_m, tail_m = _inproj(meta_tokens, no_halo, *proj_args, N_META, N_META)
    state0 = _gdn_meta(qkv_m, gates_m)
    qkv, z, ysc, gates, _ = _inproj(x2d, tail_m, *proj_args, seq, tm)
    o = _gdn(qkv, gates, state0, batch, seq, min(512, seq))
    out = _post(x2d, o, z, ysc, row(gdn_norm), w_out[0].astype(BF16), row(mix_post_norm),
                row(ffn_pre_norm), w_gate[0].astype(BF16), w_up[0].astype(BF16),
                w_down[0].astype(BF16), row(ffn_post_norm), tm)
    return out.reshape(batch, seq, D_MODEL)
```

```python
import functools

import jax
import jax.numpy as jnp
from jax import lax
from jax.experimental import pallas as pl
from jax.experimental.pallas import tpu as pltpu

F32 = jnp.float32
BF16 = jnp.bfloat16

D_MODEL = 1024
N_META = 16
HEADS = 4
HEAD_DIM = 128
GDN_WIDTH = HEADS * HEAD_DIM
SC_WIDTH = D_MODEL - GDN_WIDTH
CHUNK = 64
D_FF = 2816
EPS = 1e-6
LANES = 128
SUBLANES = 8
GROUP = 4 * LANES
N_GROUPS = 7
N_HALO = 4 * HEADS
INV_BLK = 16
FF_TILE = 768
INPROJ_ROWS = 1024
GDN_ROWS = 512
POST_ROWS = 512
INPROJ_SPLIT = 2
POST_SPLIT = 2
GDN_STREAMS = 2
VMEM_LIMIT = 56 * 1024 * 1024


def _sigmoid(x):
    return 1.0 / (1.0 + jnp.exp(-x))


def _silu(x):
    return x * _sigmoid(x)


def _softplus(x):
    return jnp.maximum(x, 0.0) + jnp.log1p(jnp.exp(-jnp.abs(x)))


def _rms_norm(x, gain):
    return x * lax.rsqrt(jnp.mean(x * x, axis=-1, keepdims=True) + EPS) * gain


def _l2_normalize(x):
    return x * lax.rsqrt(jnp.sum(x * x, axis=-1, keepdims=True) + EPS)


def _dot(a, b):
    return jnp.dot(a, b, preferred_element_type=F32)


def _dot_nt(a, b):
    return lax.dot_general(a, b, (((1,), (1,)), ((), ())), preferred_element_type=F32)


def _dot_tn(a, b):
    return lax.dot_general(a, b, (((0,), (0,)), ((), ())), preferred_element_type=F32)


def _split_bf16(x):
    hi = x.astype(BF16)
    return hi, (x - hi.astype(F32)).astype(BF16)


def _causal_conv(ext_ref, x, w):
    rows, taps = x.shape[0], w.shape[0]
    ext_ref[pl.ds(SUBLANES, rows), :] = x
    acc = x * w[taps - 1:taps]
    for i in range(taps - 1):
        acc = acc + ext_ref[pl.ds(SUBLANES - (taps - 1) + i, rows), :] * w[i:i + 1]
    ext_ref[pl.ds(0, SUBLANES), :] = x[rows - SUBLANES:rows]
    return acc


def _inproj_kernel(x_ref, g_ref, w_ref, wl_ref, cw_ref, csc_ref, alog_ref, dtb_ref, head_ref,
                   qkv_ref, z_ref, ysc_ref, gate_ref, tail_ref, ext_ref, *, tiles_per_seq):
    @pl.when(pl.program_id(0) % tiles_per_seq == 0)
    def _sequence_start():
        ext_ref[:, pl.ds(0, SUBLANES), :] = head_ref[...]

    rows_per = x_ref.shape[0] // min(INPROJ_SPLIT, x_ref.shape[0] // SUBLANES)
    parts = [pl.ds(r, rows_per) for r in range(0, x_ref.shape[0], rows_per)]
    xb = [_rms_norm(x_ref[rows, :], g_ref[...]).astype(BF16) for rows in parts]
    group = lambda i, j: _dot(xb[i], w_ref[:, GROUP * j:GROUP * (j + 1)])
    blocks = lambda r: [r[:, LANES * c:LANES * (c + 1)] for c in range(HEADS)]

    conv_block = lambda slot, x, w: _causal_conv(ext_ref.at[slot], x, w)

    for t in range(3):
        for i, rows in enumerate(parts):
            for h, x in enumerate(blocks(group(i, t))):
                y = _silu(conv_block(HEADS * t + h, x, cw_ref[HEADS * t + h]))
                if t == 0:
                    y = _l2_normalize(y) * (HEAD_DIM ** -0.5)
                elif t == 1:
                    y = _l2_normalize(y)
                qkv_ref[HEADS * t + h, rows, :] = y.astype(BF16)
    for i, rows in enumerate(parts):
        sc_x, sc_c = blocks(group(i, 4)), blocks(group(i, 6))
        conv = [conv_block(3 * HEADS + c, sc_c[c] * sc_x[c], csc_ref[c]) for c in range(HEADS)]
        sc_b = blocks(group(i, 5))
        ysc_ref[rows, :] = jnp.concatenate([sc_b[c] * conv[c] for c in range(HEADS)], axis=1).astype(BF16)
    for i, rows in enumerate(parts):
        logits = _dot(xb[i], wl_ref[...])
        lane = lax.broadcasted_iota(jnp.int32, logits.shape, 1)
        gate_ref[rows, :] = jnp.where(lane < HEADS, _sigmoid(logits),
                                      -jnp.exp(alog_ref[...]) * _softplus(logits + dtb_ref[...]))
    for i, rows in enumerate(parts):
        z_ref[rows, :] = group(i, 3)
    tail_ref[...] = ext_ref[:, pl.ds(0, SUBLANES), :]


def _inproj(x2d, head_halo, gain, w_main, w_logit, conv_w, conv_sc, alog_row, dtb_row, seq, tm):
    rows = x2d.shape[0]
    n_tiles = rows // tm
    once = lambda shape: pl.BlockSpec(shape, lambda i: (0,) * len(shape), pipeline_mode=pl.Buffered(1))
    return pl.pallas_call(
        functools.partial(_inproj_kernel, tiles_per_seq=seq // tm),
        grid=(n_tiles,),
        in_specs=[
            pl.BlockSpec((tm, D_MODEL), lambda i: (i, 0)),
            once((1, D_MODEL)),
            once((D_MODEL, N_GROUPS * GROUP)),
            once((D_MODEL, LANES)),
            once((3 * HEADS, 4, LANES)),
            once((HEADS, 3, LANES)),
            once((1, LANES)), once((1, LANES)),
            once((N_HALO, SUBLANES, LANES)),
        ],
        out_specs=[
            pl.BlockSpec((3 * HEADS, tm, LANES), lambda i: (0, i, 0)),
            pl.BlockSpec((tm, GDN_WIDTH), lambda i: (i, 0)),
            pl.BlockSpec((tm, SC_WIDTH), lambda i: (i, 0)),
            pl.BlockSpec((tm, LANES), lambda i: (i, 0)),
            pl.BlockSpec((N_HALO, SUBLANES, LANES), lambda i: (0, 0, 0)),
        ],
        out_shape=[
            jax.ShapeDtypeStruct((3 * HEADS, rows, LANES), BF16),
            jax.ShapeDtypeStruct((rows, GDN_WIDTH), F32),
            jax.ShapeDtypeStruct((rows, SC_WIDTH), BF16),
            jax.ShapeDtypeStruct((rows, LANES), F32),
            jax.ShapeDtypeStruct((N_HALO, SUBLANES, LANES), F32),
        ],
        scratch_shapes=[pltpu.VMEM((N_HALO, SUBLANES + tm // min(INPROJ_SPLIT, tm // SUBLANES), LANES), F32)],
        compiler_params=pltpu.CompilerParams(
            dimension_semantics=("arbitrary",), vmem_limit_bytes=VMEM_LIMIT),
        name="inproj",
    )(x2d, gain, w_main, w_logit, conv_w, conv_sc, alog_row, dtb_row, head_halo)


def _pair_diag(x0, x1):
    z = jnp.zeros_like(x0)
    return jnp.concatenate([jnp.concatenate([x0, z], axis=1), jnp.concatenate([z, x1], axis=1)], axis=0)


def _diag_block_inverse(a2s):
    nblk = CHUNK // INV_BLK
    lane = lax.broadcasted_iota(jnp.int32, (INV_BLK, LANES), 1)
    row = lax.broadcasted_iota(jnp.int32, (INV_BLK, LANES), 0)
    grp = (lane & (CHUNK - 1)) // INV_BLK
    base = (lane - (lane & (INV_BLK - 1)))[:SUBLANES]
    eye = ((lane & (INV_BLK - 1)) == row).astype(F32)
    zs, ds = [], []
    for a2 in a2s:
        z = jnp.zeros((INV_BLK, LANES), F32)
        for m in range(nblk):
            z = jnp.where(grp == m, a2[INV_BLK * m:INV_BLK * (m + 1), :], z)
        zs.append([z[:SUBLANES], z[SUBLANES:]])
        ds.append([eye[:SUBLANES], eye[SUBLANES:]])
    for j in range(INV_BLK - 1):
        idx = base + j
        for z, d in zip(zs, ds):
            pivot = d[j // SUBLANES][j % SUBLANES:j % SUBLANES + 1, :]
            for t in range(2):
                if SUBLANES * (t + 1) - 1 > j:
                    d[t] = d[t] - jnp.take_along_axis(z[t], idx, axis=1) * pivot
    out = []
    for d in ds:
        dz = jnp.concatenate(d, axis=0)
        out.append(jnp.concatenate([jnp.where(grp == m, dz, 0.0) for m in range(nblk)], axis=0))
    return out


def _chunk_factors(chunks):
    row = lax.broadcasted_iota(jnp.int32, (CHUNK, LANES), 0)
    lane = lax.broadcasted_iota(jnp.int32, (CHUNK, LANES), 1)
    col = lane & (CHUNK - 1)
    left = lane < CHUNK
    pick = lambda a, b: jnp.where(left, a, b)
    incl2, strict2 = row >= col, row > col
    tri2 = incl2.astype(BF16)
    eye2 = (row == col).astype(BF16)
    eye_stack = jnp.concatenate([eye2, eye2], axis=0)
    off_diag = row // INV_BLK != col // INV_BLK

    cums = [_dot(tri2, jnp.concatenate(_split_bf16(c[3]), axis=0)) for c in chunks]
    ys = [jnp.where(lane >= HEADS, cum, c[3]) for cum, c in zip(cums, chunks)]
    rows_all = [_dot_tn(jnp.concatenate(_split_bf16(y), axis=0), eye_stack) for y in ys]

    outs, items = [], []
    for (q16, k16, v16, _), y, rows in zip(chunks, ys, rows_all):
        beta_c = [jnp.broadcast_to(y[:, h:h + 1], (CHUNK, LANES)) for h in range(HEADS)]
        gc_c = [jnp.broadcast_to(y[:, HEADS + h:HEADS + h + 1], (CHUNK, LANES)) for h in range(HEADS)]
        out = dict(u=[], w=[], eg=[], dk=[], gl=[], qk=[])
        for h in range(HEADS):
            gc_last = gc_c[h][CHUNK - 1:CHUNK, :]
            out["eg"].append(jnp.exp(gc_c[h]))
            out["dk"].append(jnp.exp(gc_last - gc_c[h]))
            out["gl"].append(jnp.exp(gc_last))
        outs.append(out)
        for h0 in range(0, HEADS, 2):
            h1 = h0 + 1
            beta_r = pick(rows[h0:h0 + 1, :], rows[h1:h1 + 1, :])
            gc_r = pick(rows[HEADS + h0:HEADS + h0 + 1, :], rows[HEADS + h1:HEADS + h1 + 1, :])
            decay = jnp.where(incl2, jnp.exp(jnp.where(incl2, pick(gc_c[h0], gc_c[h1]) - gc_r, 0.0)), 0.0)
            items.append(dict(out=out, beta_r=beta_r, w_scale=beta_r * jnp.exp(gc_r), decay=decay,
                              beta_c=pick(beta_c[h0], beta_c[h1]),
                              k_cat=jnp.concatenate([k16[h0], k16[h1]], axis=1),
                              q_cat=jnp.concatenate([q16[h0], q16[h1]], axis=1),
                              k_bd=_pair_diag(k16[h0], k16[h1]), v_bd=_pair_diag(v16[h0], v16[h1])))

    halves_diag = lambda x: jnp.concatenate([jnp.where(left, x, 0.0), jnp.where(left, 0.0, x)], axis=0)

    for it in items:
        kq = _dot_nt(jnp.concatenate([it["k_cat"], it["q_cat"]], axis=0), it["k_bd"])
        it["a2"] = jnp.where(strict2, it["beta_c"] * kq[:CHUNK] * it["decay"], 0.0)
        it["out"]["qk"].append(jnp.where(incl2, kq[CHUNK:] * it["decay"], 0.0).astype(BF16))

    for it, xd in zip(items, _diag_block_inverse([it["a2"] for it in items])):
        it["xd"] = xd
    for it in items:
        off_bd = halves_diag(jnp.where(off_diag, it["a2"], 0.0))
        it["n16"] = _dot(it["xd"].astype(BF16), off_bd.astype(BF16)).astype(BF16)
        it["t"] = it["xd"]
    for _ in range(CHUNK // INV_BLK - 1):
        for it in items:
            it["t"] = it["xd"] - _dot(it["n16"], halves_diag(it["t"]).astype(BF16))
    for it in items:
        u01 = _dot((it["t"] * it["beta_r"]).astype(BF16), it["v_bd"])
        w01 = _dot((it["t"] * it["w_scale"]).astype(BF16), it["k_bd"])
        for i in range(2):
            it["out"]["u"].append(u01[:, LANES * i:LANES * (i + 1)])
            it["out"]["w"].append(w01[:, LANES * i:LANES * (i + 1)].astype(BF16))
    return outs


def _chunk_recurrence(streams, need_out):
    work = []
    for q16, k16, f, states in streams:
        work.append(dict(q=q16, k=k16, f=f, s=states, s16=[x.astype(BF16) for x in states],
                         v_new=[], q_s=[], outs=[]))
    for t in work:
        for h in range(HEADS):
            ws = _dot(jnp.concatenate([t["f"]["w"][h], t["q"][h]], axis=0), t["s16"][h])
            t["v_new"].append(t["f"]["u"][h] - ws[:CHUNK])
            t["q_s"].append(ws[CHUNK:])
    for t in work:
        t["vn16"] = [v.astype(BF16) for v in t["v_new"]]
        t["vd16"] = [(v * d).astype(BF16) for v, d in zip(t["v_new"], t["f"]["dk"])]
    for t in work:
        t["kv"] = [_dot_tn(t["k"][h], t["vd16"][h]) for h in range(HEADS)]
    if need_out:
        for t in work:
            for h0 in range(0, HEADS, 2):
                intra = _dot(t["f"]["qk"][h0 // 2], _pair_diag(t["vn16"][h0], t["vn16"][h0 + 1]))
                for i in range(2):
                    t["outs"].append(t["f"]["eg"][h0 + i] * t["q_s"][h0 + i]
                                     + intra[:, LANES * i:LANES * (i + 1)])
    return [(t["outs"], [t["s"][h] * t["f"]["gl"][h] + t["kv"][h] for h in range(HEADS)]) for t in work]


_FACTOR_KEYS = ("u", "w", "eg", "dk", "gl")


def _gdn_kernel(q_ref, k_ref, v_ref, gate_ref, s0_ref, o_ref,
                s_ref, u_ref, w_ref, eg_ref, dk_ref, gl_ref, qk_ref, *, n_chunks):
    f_refs = dict(u=u_ref, w=w_ref, eg=eg_ref, dk=dk_ref, gl=gl_ref)

    @pl.when(pl.program_id(1) == 0)
    def _sequence_start():
        for b in range(GDN_STREAMS):
            s_ref[b] = s0_ref[...]

    def rows_of(ref, b, c):
        r0 = pl.multiple_of(c * CHUNK, CHUNK)
        return [ref[h, b, pl.ds(r0, CHUNK), :] for h in range(HEADS)]

    def factors(b, carry):
        fs = _chunk_factors([(rows_of(q_ref, b, c), rows_of(k_ref, b, c), rows_of(v_ref, b, c),
                              gate_ref[b, pl.ds(c * CHUNK, CHUNK), :]) for c in range(n_chunks)])
        for c, f in enumerate(fs):
            slot = b * n_chunks + c
            for h in range(HEADS):
                for key in _FACTOR_KEYS:
                    val = f[key][h]
                    f_refs[key][slot, h] = jnp.broadcast_to(val, f_refs[key].shape[2:]) if key == "gl" else val
            for p in range(HEADS // 2):
                qk_ref[slot, p] = f["qk"][p]
        return carry

    def recurrence(c, carry):
        r0 = pl.multiple_of(c * CHUNK, CHUNK)
        streams = []
        for b in range(GDN_STREAMS):
            slot = b * n_chunks + c
            f = {key: [f_refs[key][slot, h] for h in range(HEADS)] for key in _FACTOR_KEYS}
            f["gl"] = [g[:1] for g in f["gl"]]
            f["qk"] = [qk_ref[slot, p] for p in range(HEADS // 2)]
            streams.append((rows_of(q_ref, b, c), rows_of(k_ref, b, c), f,
                            [s_ref[b, h] for h in range(HEADS)]))
        for b, (outs, st) in enumerate(_chunk_recurrence(streams, True)):
            for h in range(HEADS):
                o_ref[b, pl.ds(r0, CHUNK), LANES * h:LANES * (h + 1)] = outs[h]
                s_ref[b, h] = st[h]
        return carry

    lax.fori_loop(0, GDN_STREAMS, factors, 0)
    lax.fori_loop(0, n_chunks, recurrence, 0)


def _gdn(qkv, gates, state0, batch, seq, ts):
    ns = seq // ts
    nc = ts // CHUNK
    nb = GDN_STREAMS
    blk = lambda t: pl.BlockSpec((HEADS, nb, ts, LANES), lambda b, s: (t, b, s, 0))
    per_head = lambda rows, dtype: pltpu.VMEM((nb * nc, HEADS, rows, LANES), dtype)
    o = pl.pallas_call(
        functools.partial(_gdn_kernel, n_chunks=nc),
        grid=(batch // nb, ns),
        in_specs=[
            blk(0), blk(1), blk(2),
            pl.BlockSpec((nb, ts, LANES), lambda b, s: (b, s, 0)),
            pl.BlockSpec((HEADS, HEAD_DIM, HEAD_DIM), lambda b, s: (0, 0, 0)),
        ],
        out_specs=pl.BlockSpec((nb, ts, GDN_WIDTH), lambda b, s: (b, s, 0)),
        out_shape=jax.ShapeDtypeStruct((batch, seq, GDN_WIDTH), F32),
        scratch_shapes=[
            pltpu.VMEM((nb, HEADS, HEAD_DIM, HEAD_DIM), F32),
            per_head(CHUNK, F32), per_head(CHUNK, BF16), per_head(CHUNK, F32), per_head(CHUNK, F32),
            per_head(SUBLANES, F32),
            pltpu.VMEM((nb * nc, HEADS // 2, CHUNK, LANES), BF16),
        ],
        compiler_params=pltpu.CompilerParams(
            dimension_semantics=("parallel", "arbitrary"), vmem_limit_bytes=VMEM_LIMIT),
        name="gdn",
    )(*([qkv.reshape(3 * HEADS, batch, seq, LANES)] * 3), gates.reshape(batch, seq, LANES), state0)
    return o.reshape(batch * seq, GDN_WIDTH)


def _gdn_meta_kernel(k_ref, v_ref, gate_ref, s_ref):
    pad = CHUNK - N_META
    front = lambda x: jnp.concatenate([jnp.zeros((pad, LANES), x.dtype), x], axis=0)
    k16 = [front(k_ref[h]) for h in range(HEADS)]
    v16 = [front(v_ref[h]) for h in range(HEADS)]
    f, = _chunk_factors([(k16, k16, v16, front(gate_ref[...]))])
    zero = [jnp.zeros((HEAD_DIM, HEAD_DIM), F32)] * HEADS
    (_, st), = _chunk_recurrence([(k16, k16, f, zero)], False)
    for h in range(HEADS):
        s_ref[h] = st[h]


def _gdn_meta(qkv_meta, gates_meta):
    blk = lambda t: pl.BlockSpec((HEADS, N_META, LANES), lambda i: (t, 0, 0))
    return pl.pallas_call(
        _gdn_meta_kernel,
        grid=(1,),
        in_specs=[blk(1), blk(2), pl.BlockSpec((N_META, LANES), lambda i: (0, 0))],
        out_specs=pl.BlockSpec((HEADS, HEAD_DIM, HEAD_DIM), lambda i: (0, 0, 0)),
        out_shape=jax.ShapeDtypeStruct((HEADS, HEAD_DIM, HEAD_DIM), F32),
        name="gdn_meta",
    )(qkv_meta, qkv_meta, gates_meta)


def _post_kernel(x_ref, o_ref, z_ref, ysc_ref, gn_ref, wo_ref, g_post_ref, g_pre_ref, wg_ref, wu_ref,
                 wd_ref, g_fpost_ref, out_ref):
    rows_per = x_ref.shape[0] // POST_SPLIT
    parts = [pl.ds(i * rows_per, rows_per) for i in range(POST_SPLIT)]
    gn = gn_ref[...]
    mixes = []
    for rows in parts:
        heads = []
        for h in range(HEADS):
            sl = slice(HEAD_DIM * h, HEAD_DIM * (h + 1))
            heads.append((_rms_norm(o_ref[rows, sl], gn) * _silu(z_ref[rows, sl])).astype(BF16))
        mixes.append(_dot(jnp.concatenate(heads + [ysc_ref[rows, :]], axis=1), wo_ref[...]))
    h1 = [x_ref[rows, :] + _rms_norm(mix, g_post_ref[...]) for rows, mix in zip(parts, mixes)]
    u = [_rms_norm(h, g_pre_ref[...]).astype(BF16) for h in h1]
    ffn = [jnp.zeros((rows_per, D_MODEL), F32)] * POST_SPLIT
    for lo in range(0, D_FF, FF_TILE):
        sl = slice(lo, min(lo + FF_TILE, D_FF))
        for i in range(POST_SPLIT):
            act = (_silu(_dot(u[i], wg_ref[:, sl])) * _dot(u[i], wu_ref[:, sl])).astype(BF16)
            ffn[i] = ffn[i] + _dot(act, wd_ref[sl, :])
    for i, rows in enumerate(parts):
        out_ref[rows, :] = h1[i] + _rms_norm(ffn[i], g_fpost_ref[...])


def _post(x2d, o, z, ysc, gnorm, w_out, g_post, g_pre, w_gate, w_up, w_down, g_fpost, tm):
    rows = x2d.shape[0]
    row_blk = lambda width: pl.BlockSpec((tm, width), lambda i: (i, 0))
    once = lambda shape: pl.BlockSpec(shape, lambda i: (0, 0), pipeline_mode=pl.Buffered(1))
    return pl.pallas_call(
        _post_kernel,
        grid=(rows // tm,),
        in_specs=[
            row_blk(D_MODEL), row_blk(GDN_WIDTH), row_blk(GDN_WIDTH), row_blk(SC_WIDTH),
            once((1, HEAD_DIM)),
            once((D_MODEL, D_MODEL)), once((1, D_MODEL)), once((1, D_MODEL)),
            once((D_MODEL, D_FF)), once((D_MODEL, D_FF)), once((D_FF, D_MODEL)), once((1, D_MODEL)),
        ],
        out_specs=row_blk(D_MODEL),
        out_shape=jax.ShapeDtypeStruct((rows, D_MODEL), F32),
        compiler_params=pltpu.CompilerParams(
            dimension_semantics=("parallel",), vmem_limit_bytes=VMEM_LIMIT),
        name="post",
    )(x2d, o, z, ysc, gnorm, w_out, g_post, g_pre, w_gate, w_up, w_down, g_fpost)


def kernel(x, meta_tokens, mix_pre_norm, mix_post_norm, ffn_pre_norm, ffn_post_norm, w_in, conv_qkv,
           a_log, dt_bias, gdn_norm, conv_sc, w_out, w_gate, w_up, w_down):
    batch, seq, _ = x.shape
    assert mix_pre_norm.shape[0] == 1, "single-layer block"
    assert batch % GDN_STREAMS == 0
    n_logit = 2 * HEADS
    cut = 4 * GDN_WIDTH
    w_in0 = w_in[0]
    w_main = jnp.concatenate([w_in0[:, :cut], w_in0[:, cut + n_logit:]], axis=1).astype(BF16)
    w_logit = jnp.pad(w_in0[:, cut:cut + n_logit], ((0, 0), (0, LANES - n_logit))).astype(BF16)
    lane_pad = lambda v: jnp.pad(v.reshape(1, HEADS), ((0, 0), (HEADS, LANES - 2 * HEADS)))
    conv_w = conv_qkv[0].reshape(4, 3 * HEADS, LANES).transpose(1, 0, 2)
    conv_s = conv_sc[0].reshape(3, HEADS, LANES).transpose(1, 0, 2)
    row = lambda v: v[0].reshape(1, -1)
    proj_args = (row(mix_pre_norm), w_main, w_logit, conv_w, conv_s, lane_pad(a_log[0]), lane_pad(dt_bias[0]))

    x2d = x.reshape(batch * seq, D_MODEL)
    no_halo = jnp.zeros((N_HALO, SUBLANES, LANES), F32)
    qkv_m, _, _, gates_m, tail_m = _inproj(meta_tokens, no_halo, *proj_args, N_META, N_META)
    state0 = _gdn_meta(qkv_m, gates_m)
    qkv, z, ysc, gates, _ = _inproj(x2d, tail_m, *proj_args, seq, min(INPROJ_ROWS, seq))
    o = _gdn(qkv, gates, state0, batch, seq, min(GDN_ROWS, seq))
    out = _post(x2d, o, z, ysc, row(gdn_norm), w_out[0].astype(BF16), row(mix_post_norm),
                row(ffn_pre_norm), w_gate[0].astype(BF16), w_up[0].astype(BF16),
                w_down[0].astype(BF16), row(ffn_post_norm), min(POST_ROWS, seq))
    return out.reshape(batch, seq, D_MODEL)
```

```python
import functools

import jax
import jax.numpy as jnp
from jax import lax
from jax.experimental import pallas as pl
from jax.experimental.pallas import tpu as pltpu

F32 = jnp.float32
BF16 = jnp.bfloat16

D_MODEL = 1024
N_META = 16
HEADS = 4
HEAD_DIM = 128
GDN_WIDTH = HEADS * HEAD_DIM
SC_WIDTH = D_MODEL - GDN_WIDTH
CHUNK = 64
D_FF = 2816
EPS = 1e-6
LANES = 128
SUBLANES = 8
GROUP = 4 * LANES
N_GROUPS = 7
N_HALO = 4 * HEADS
INV_BLK = 16
FF_TILE = 768
INPROJ_ROWS = 1024
GDN_ROWS = 512
POST_ROWS = 512
INPROJ_SPLIT = 2
POST_SPLIT = 2
GDN_STREAMS = 4
VMEM_LIMIT = 56 * 1024 * 1024


def _sigmoid(x):
    return 1.0 / (1.0 + jnp.exp(-x))


def _silu(x):
    return x * _sigmoid(x)


def _softplus(x):
    return jnp.maximum(x, 0.0) + jnp.log1p(jnp.exp(-jnp.abs(x)))


def _rms_norm(x, gain):
    return x * lax.rsqrt(jnp.mean(x * x, axis=-1, keepdims=True) + EPS) * gain


def _l2_normalize(x):
    return x * lax.rsqrt(jnp.sum(x * x, axis=-1, keepdims=True) + EPS)


def _dot(a, b):
    return jnp.dot(a, b, preferred_element_type=F32)


def _dot_nt(a, b):
    return lax.dot_general(a, b, (((1,), (1,)), ((), ())), preferred_element_type=F32)


def _dot_tn(a, b):
    return lax.dot_general(a, b, (((0,), (0,)), ((), ())), preferred_element_type=F32)


def _split_bf16(x):
    hi = x.astype(BF16)
    return hi, (x - hi.astype(F32)).astype(BF16)


def _causal_conv(ext_ref, x, w):
    rows, taps = x.shape[0], w.shape[0]
    ext_ref[pl.ds(SUBLANES, rows), :] = x
    acc = x * w[taps - 1:taps]
    for i in range(taps - 1):
        acc = acc + ext_ref[pl.ds(SUBLANES - (taps - 1) + i, rows), :] * w[i:i + 1]
    ext_ref[pl.ds(0, SUBLANES), :] = x[rows - SUBLANES:rows]
    return acc


def _inproj_kernel(x_ref, g_ref, w_ref, wl_ref, cw_ref, csc_ref, alog_ref, dtb_ref, head_ref,
                   qkv_ref, z_ref, ysc_ref, gate_ref, tail_ref, ext_ref, *, tiles_per_seq):
    @pl.when(pl.program_id(0) % tiles_per_seq == 0)
    def _sequence_start():
        ext_ref[:, pl.ds(0, SUBLANES), :] = head_ref[...]

    rows_per = x_ref.shape[0] // min(INPROJ_SPLIT, x_ref.shape[0] // SUBLANES)
    parts = [pl.ds(r, rows_per) for r in range(0, x_ref.shape[0], rows_per)]
    xb = [_rms_norm(x_ref[rows, :], g_ref[...]).astype(BF16) for rows in parts]
    group = lambda i, j: _dot(xb[i], w_ref[:, GROUP * j:GROUP * (j + 1)])
    blocks = lambda r: [r[:, LANES * c:LANES * (c + 1)] for c in range(HEADS)]

    conv_block = lambda slot, x, w: _causal_conv(ext_ref.at[slot], x, w)

    for t in range(3):
        for i, rows in enumerate(parts):
            for h, x in enumerate(blocks(group(i, t))):
                y = _silu(conv_block(HEADS * t + h, x, cw_ref[HEADS * t + h]))
                if t == 0:
                    y = _l2_normalize(y) * (HEAD_DIM ** -0.5)
                elif t == 1:
                    y = _l2_normalize(y)
                qkv_ref[HEADS * t + h, rows, :] = y.astype(BF16)
    for i, rows in enumerate(parts):
        sc_x, sc_c = blocks(group(i, 4)), blocks(group(i, 6))
        conv = [conv_block(3 * HEADS + c, sc_c[c] * sc_x[c], csc_ref[c]) for c in range(HEADS)]
        sc_b = blocks(group(i, 5))
        ysc_ref[rows, :] = jnp.concatenate([sc_b[c] * conv[c] for c in range(HEADS)], axis=1).astype(BF16)
    for i, rows in enumerate(parts):
        logits = _dot(xb[i], wl_ref[...])
        lane = lax.broadcasted_iota(jnp.int32, logits.shape, 1)
        gate_ref[rows, :] = jnp.where(lane < HEADS, _sigmoid(logits),
                                      -jnp.exp(alog_ref[...]) * _softplus(logits + dtb_ref[...]))
    for i, rows in enumerate(parts):
        z_ref[rows, :] = group(i, 3)
    tail_ref[...] = ext_ref[:, pl.ds(0, SUBLANES), :]


def _inproj(x2d, head_halo, gain, w_main, w_logit, conv_w, conv_sc, alog_row, dtb_row, seq, tm):
    rows = x2d.shape[0]
    n_tiles = rows // tm
    once = lambda shape: pl.BlockSpec(shape, lambda i: (0,) * len(shape), pipeline_mode=pl.Buffered(1))
    return pl.pallas_call(
        functools.partial(_inproj_kernel, tiles_per_seq=seq // tm),
        grid=(n_tiles,),
        in_specs=[
            pl.BlockSpec((tm, D_MODEL), lambda i: (i, 0)),
            once((1, D_MODEL)),
            once((D_MODEL, N_GROUPS * GROUP)),
            once((D_MODEL, LANES)),
            once((3 * HEADS, 4, LANES)),
            once((HEADS, 3, LANES)),
            once((1, LANES)), once((1, LANES)),
            once((N_HALO, SUBLANES, LANES)),
        ],
        out_specs=[
            pl.BlockSpec((3 * HEADS, tm, LANES), lambda i: (0, i, 0)),
            pl.BlockSpec((tm, GDN_WIDTH), lambda i: (i, 0)),
            pl.BlockSpec((tm, SC_WIDTH), lambda i: (i, 0)),
            pl.BlockSpec((tm, LANES), lambda i: (i, 0)),
            pl.BlockSpec((N_HALO, SUBLANES, LANES), lambda i: (0, 0, 0)),
        ],
        out_shape=[
            jax.ShapeDtypeStruct((3 * HEADS, rows, LANES), BF16),
            jax.ShapeDtypeStruct((rows, GDN_WIDTH), F32),
            jax.ShapeDtypeStruct((rows, SC_WIDTH), BF16),
            jax.ShapeDtypeStruct((rows, LANES), F32),
            jax.ShapeDtypeStruct((N_HALO, SUBLANES, LANES), F32),
        ],
        scratch_shapes=[pltpu.VMEM((N_HALO, SUBLANES + tm // min(INPROJ_SPLIT, tm // SUBLANES), LANES), F32)],
        compiler_params=pltpu.CompilerParams(
            dimension_semantics=("arbitrary",), vmem_limit_bytes=VMEM_LIMIT),
        name="inproj",
    )(x2d, gain, w_main, w_logit, conv_w, conv_sc, alog_row, dtb_row, head_halo)


def _pair_diag(x0, x1):
    z = jnp.zeros_like(x0)
    return jnp.concatenate([jnp.concatenate([x0, z], axis=1), jnp.concatenate([z, x1], axis=1)], axis=0)


def _diag_block_inverse(a2s):
    nblk = CHUNK // INV_BLK
    lane = lax.broadcasted_iota(jnp.int32, (INV_BLK, LANES), 1)
    row = lax.broadcasted_iota(jnp.int32, (INV_BLK, LANES), 0)
    grp = (lane & (CHUNK - 1)) // INV_BLK
    base = (lane - (lane & (INV_BLK - 1)))[:SUBLANES]
    eye = ((lane & (INV_BLK - 1)) == row).astype(F32)
    zs, ds = [], []
    for a2 in a2s:
        z = jnp.zeros((INV_BLK, LANES), F32)
        for m in range(nblk):
            z = jnp.where(grp == m, a2[INV_BLK * m:INV_BLK * (m + 1), :], z)
        zs.append([z[:SUBLANES], z[SUBLANES:]])
        ds.append([eye[:SUBLANES], eye[SUBLANES:]])
    for j in range(INV_BLK - 1):
        idx = base + j
        for z, d in zip(zs, ds):
            pivot = d[j // SUBLANES][j % SUBLANES:j % SUBLANES + 1, :]
            for t in range(2):
                if SUBLANES * (t + 1) - 1 > j:
                    d[t] = d[t] - jnp.take_along_axis(z[t], idx, axis=1) * pivot
    out = []
    for d in ds:
        dz = jnp.concatenate(d, axis=0)
        out.append(jnp.concatenate([jnp.where(grp == m, dz, 0.0) for m in range(nblk)], axis=0))
    return out


def _chunk_factors(chunks):
    row = lax.broadcasted_iota(jnp.int32, (CHUNK, LANES), 0)
    lane = lax.broadcasted_iota(jnp.int32, (CHUNK, LANES), 1)
    col = lane & (CHUNK - 1)
    left = lane < CHUNK
    pick = lambda a, b: jnp.where(left, a, b)
    incl2, strict2 = row >= col, row > col
    tri2 = incl2.astype(BF16)
    eye2 = (row == col).astype(BF16)
    eye_stack = jnp.concatenate([eye2, eye2], axis=0)
    off_diag = row // INV_BLK != col // INV_BLK

    cums = [_dot(tri2, jnp.concatenate(_split_bf16(c[3]), axis=0)) for c in chunks]
    ys = [jnp.where(lane >= HEADS, cum, c[3]) for cum, c in zip(cums, chunks)]
    rows_all = [_dot_tn(jnp.concatenate(_split_bf16(y), axis=0), eye_stack) for y in ys]

    outs, items = [], []
    for (q16, k16, v16, _), y, rows in zip(chunks, ys, rows_all):
        beta_c = [jnp.broadcast_to(y[:, h:h + 1], (CHUNK, LANES)) for h in range(HEADS)]
        gc_c = [jnp.broadcast_to(y[:, HEADS + h:HEADS + h + 1], (CHUNK, LANES)) for h in range(HEADS)]
        out = dict(u=[], w=[], eg=[], dk=[], gl=[], qk=[])
        for h in range(HEADS):
            gc_last = gc_c[h][CHUNK - 1:CHUNK, :]
            out["eg"].append(jnp.exp(gc_c[h]))
            out["dk"].append(jnp.exp(gc_last - gc_c[h]))
            out["gl"].append(jnp.exp(gc_last))
        outs.append(out)
        for h0 in range(0, HEADS, 2):
            h1 = h0 + 1
            beta_r = pick(rows[h0:h0 + 1, :], rows[h1:h1 + 1, :])
            gc_r = pick(rows[HEADS + h0:HEADS + h0 + 1, :], rows[HEADS + h1:HEADS + h1 + 1, :])
            decay = jnp.where(incl2, jnp.exp(jnp.where(incl2, pick(gc_c[h0], gc_c[h1]) - gc_r, 0.0)), 0.0)
            items.append(dict(out=out, beta_r=beta_r, w_scale=beta_r * jnp.exp(gc_r), decay=decay,
                              beta_c=pick(beta_c[h0], beta_c[h1]),
                              k_cat=jnp.concatenate([k16[h0], k16[h1]], axis=1),
                              q_cat=jnp.concatenate([q16[h0], q16[h1]], axis=1),
                              k_bd=_pair_diag(k16[h0], k16[h1]), v_bd=_pair_diag(v16[h0], v16[h1])))

    halves_diag = lambda x: jnp.concatenate([jnp.where(left, x, 0.0), jnp.where(left, 0.0, x)], axis=0)

    for it in items:
        kq = _dot_nt(jnp.concatenate([it["k_cat"], it["q_cat"]], axis=0), it["k_bd"])
        it["a2"] = jnp.where(strict2, it["beta_c"] * kq[:CHUNK] * it["decay"], 0.0)
        it["out"]["qk"].append(jnp.where(incl2, kq[CHUNK:] * it["decay"], 0.0).astype(BF16))

    for it, xd in zip(items, _diag_block_inverse([it["a2"] for it in items])):
        it["xd"] = xd
    for it in items:
        off_bd = halves_diag(jnp.where(off_diag, it["a2"], 0.0))
        it["n16"] = _dot(it["xd"].astype(BF16), off_bd.astype(BF16)).astype(BF16)
        it["t"] = it["xd"]
    for _ in range(CHUNK // INV_BLK - 1):
        for it in items:
            it["t"] = it["xd"] - _dot(it["n16"], halves_diag(it["t"]).astype(BF16))
    for it in items:
        u01 = _dot((it["t"] * it["beta_r"]).astype(BF16), it["v_bd"])
        w01 = _dot((it["t"] * it["w_scale"]).astype(BF16), it["k_bd"])
        for i in range(2):
            it["out"]["u"].append(u01[:, LANES * i:LANES * (i + 1)])
            it["out"]["w"].append(w01[:, LANES * i:LANES * (i + 1)].astype(BF16))
    return outs


def _chunk_recurrence(streams, need_out):
    work = []
    for q16, k16, f, states in streams:
        work.append(dict(q=q16, k=k16, f=f, s=states, s16=[x.astype(BF16) for x in states],
                         v_new=[], q_s=[], outs=[]))
    for t in work:
        for h in range(HEADS):
            ws = _dot(jnp.concatenate([t["f"]["w"][h], t["q"][h]], axis=0), t["s16"][h])
            t["v_new"].append(t["f"]["u"][h] - ws[:CHUNK])
            t["q_s"].append(ws[CHUNK:])
    for t in work:
        t["vn16"] = [v.astype(BF16) for v in t["v_new"]]
        t["vd16"] = [(v * d).astype(BF16) for v, d in zip(t["v_new"], t["f"]["dk"])]
    for t in work:
        t["kv"] = [_dot_tn(t["k"][h], t["vd16"][h]) for h in range(HEADS)]
    if need_out:
        for t in work:
            for h0 in range(0, HEADS, 2):
                intra = _dot(t["f"]["qk"][h0 // 2], _pair_diag(t["vn16"][h0], t["vn16"][h0 + 1]))
                for i in range(2):
                    t["outs"].append(t["f"]["eg"][h0 + i] * t["q_s"][h0 + i]
                                     + intra[:, LANES * i:LANES * (i + 1)])
    return [(t["outs"], [t["s"][h] * t["f"]["gl"][h] + t["kv"][h] for h in range(HEADS)]) for t in work]


_FACTOR_KEYS = ("u", "w", "eg", "dk", "gl")


def _gdn_kernel(q_ref, k_ref, v_ref, gate_ref, s0_ref, o_ref,
                s_ref, u_ref, w_ref, eg_ref, dk_ref, gl_ref, qk_ref, *, n_chunks):
    f_refs = dict(u=u_ref, w=w_ref, eg=eg_ref, dk=dk_ref, gl=gl_ref)

    @pl.when(pl.program_id(1) == 0)
    def _sequence_start():
        for b in range(GDN_STREAMS):
            s_ref[b] = s0_ref[...]

    def rows_of(ref, b, c):
        r0 = pl.multiple_of(c * CHUNK, CHUNK)
        return [ref[h, b, pl.ds(r0, CHUNK), :] for h in range(HEADS)]

    def factors(b, carry):
        fs = _chunk_factors([(rows_of(q_ref, b, c), rows_of(k_ref, b, c), rows_of(v_ref, b, c),
                              gate_ref[b, pl.ds(c * CHUNK, CHUNK), :]) for c in range(n_chunks)])
        for c, f in enumerate(fs):
            slot = b * n_chunks + c
            for h in range(HEADS):
                for key in _FACTOR_KEYS:
                    val = f[key][h]
                    f_refs[key][slot, h] = jnp.broadcast_to(val, f_refs[key].shape[2:]) if key == "gl" else val
            for p in range(HEADS // 2):
                qk_ref[slot, p] = f["qk"][p]
        return carry

    def recurrence(c, carry):
        r0 = pl.multiple_of(c * CHUNK, CHUNK)
        streams = []
        for b in range(GDN_STREAMS):
            slot = b * n_chunks + c
            f = {key: [f_refs[key][slot, h] for h in range(HEADS)] for key in _FACTOR_KEYS}
            f["gl"] = [g[:1] for g in f["gl"]]
            f["qk"] = [qk_ref[slot, p] for p in range(HEADS // 2)]
            streams.append((rows_of(q_ref, b, c), rows_of(k_ref, b, c), f,
                            [s_ref[b, h] for h in range(HEADS)]))
        for b, (outs, st) in enumerate(_chunk_recurrence(streams, True)):
            for h in range(HEADS):
                o_ref[b, pl.ds(r0, CHUNK), LANES * h:LANES * (h + 1)] = outs[h]
                s_ref[b, h] = st[h]
        return carry

    lax.fori_loop(0, GDN_STREAMS, factors, 0)
    lax.fori_loop(0, n_chunks, recurrence, 0)


def _gdn(qkv, gates, state0, batch, seq, ts):
    ns = seq // ts
    nc = ts // CHUNK
    nb = GDN_STREAMS
    blk = lambda t: pl.BlockSpec((HEADS, nb, ts, LANES), lambda b, s: (t, b, s, 0))
    per_head = lambda rows, dtype: pltpu.VMEM((nb * nc, HEADS, rows, LANES), dtype)
    o = pl.pallas_call(
        functools.partial(_gdn_kernel, n_chunks=nc),
        grid=(batch // nb, ns),
        in_specs=[
            blk(0), blk(1), blk(2),
            pl.BlockSpec((nb, ts, LANES), lambda b, s: (b, s, 0)),
            pl.BlockSpec((HEADS, HEAD_DIM, HEAD_DIM), lambda b, s: (0, 0, 0)),
        ],
        out_specs=pl.BlockSpec((nb, ts, GDN_WIDTH), lambda b, s: (b, s, 0)),
        out_shape=jax.ShapeDtypeStruct((batch, seq, GDN_WIDTH), F32),
        scratch_shapes=[
            pltpu.VMEM((nb, HEADS, HEAD_DIM, HEAD_DIM), F32),
            per_head(CHUNK, F32), per_head(CHUNK, BF16), per_head(CHUNK, F32), per_head(CHUNK, F32),
            per_head(SUBLANES, F32),
            pltpu.VMEM((nb * nc, HEADS // 2, CHUNK, LANES), BF16),
        ],
        compiler_params=pltpu.CompilerParams(
            dimension_semantics=("parallel", "arbitrary"), vmem_limit_bytes=VMEM_LIMIT),
        name="gdn",
    )(*([qkv.reshape(3 * HEADS, batch, seq, LANES)] * 3), gates.reshape(batch, seq, LANES), state0)
    return o.reshape(batch * seq, GDN_WIDTH)


def _gdn_meta_kernel(k_ref, v_ref, gate_ref, s_ref):
    pad = CHUNK - N_META
    front = lambda x: jnp.concatenate([jnp.zeros((pad, LANES), x.dtype), x], axis=0)
    k16 = [front(k_ref[h]) for h in range(HEADS)]
    v16 = [front(v_ref[h]) for h in range(HEADS)]
    f, = _chunk_factors([(k16, k16, v16, front(gate_ref[...]))])
    zero = [jnp.zeros((HEAD_DIM, HEAD_DIM), F32)] * HEADS
    (_, st), = _chunk_recurrence([(k16, k16, f, zero)], False)
    for h in range(HEADS):
        s_ref[h] = st[h]


def _gdn_meta(qkv_meta, gates_meta):
    blk = lambda t: pl.BlockSpec((HEADS, N_META, LANES), lambda i: (t, 0, 0))
    return pl.pallas_call(
        _gdn_meta_kernel,
        grid=(1,),
        in_specs=[blk(1), blk(2), pl.BlockSpec((N_META, LANES), lambda i: (0, 0))],
        out_specs=pl.BlockSpec((HEADS, HEAD_DIM, HEAD_DIM), lambda i: (0, 0, 0)),
        out_shape=jax.ShapeDtypeStruct((HEADS, HEAD_DIM, HEAD_DIM), F32),
        name="gdn_meta",
    )(qkv_meta, qkv_meta, gates_meta)


def _post_kernel(x_ref, o_ref, z_ref, ysc_ref, gn_ref, wo_ref, g_post_ref, g_pre_ref, wg_ref, wu_ref,
                 wd_ref, g_fpost_ref, out_ref):
    rows_per = x_ref.shape[0] // POST_SPLIT
    parts = [pl.ds(i * rows_per, rows_per) for i in range(POST_SPLIT)]
    gn = gn_ref[...]
    mixes = []
    for rows in parts:
        heads = []
        for h in range(HEADS):
            sl = slice(HEAD_DIM * h, HEAD_DIM * (h + 1))
            heads.append((_rms_norm(o_ref[rows, sl], gn) * _silu(z_ref[rows, sl])).astype(BF16))
        mixes.append(_dot(jnp.concatenate(heads + [ysc_ref[rows, :]], axis=1), wo_ref[...]))
    h1 = [x_ref[rows, :] + _rms_norm(mix, g_post_ref[...]) for rows, mix in zip(parts, mixes)]
    u = [_rms_norm(h, g_pre_ref[...]).astype(BF16) for h in h1]
    ffn = [jnp.zeros((rows_per, D_MODEL), F32)] * POST_SPLIT
    for lo in range(0, D_FF, FF_TILE):
        sl = slice(lo, min(lo + FF_TILE, D_FF))
        for i in range(POST_SPLIT):
            act = (_silu(_dot(u[i], wg_ref[:, sl])) * _dot(u[i], wu_ref[:, sl])).astype(BF16)
            ffn[i] = ffn[i] + _dot(act, wd_ref[sl, :])
    for i, rows in enumerate(parts):
        out_ref[rows, :] = h1[i] + _rms_norm(ffn[i], g_fpost_ref[...])


def _post(x2d, o, z, ysc, gnorm, w_out, g_post, g_pre, w_gate, w_up, w_down, g_fpost, tm):
    rows = x2d.shape[0]
    row_blk = lambda width: pl.BlockSpec((tm, width), lambda i: (i, 0))
    once = lambda shape: pl.BlockSpec(shape, lambda i: (0, 0), pipeline_mode=pl.Buffered(1))
    return pl.pallas_call(
        _post_kernel,
        grid=(rows // tm,),
        in_specs=[
            row_blk(D_MODEL), row_blk(GDN_WIDTH), row_blk(GDN_WIDTH), row_blk(SC_WIDTH),
            once((1, HEAD_DIM)),
            once((D_MODEL, D_MODEL)), once((1, D_MODEL)), once((1, D_MODEL)),
            once((D_MODEL, D_FF)), once((D_MODEL, D_FF)), once((D_FF, D_MODEL)), once((1, D_MODEL)),
        ],
        out_specs=row_blk(D_MODEL),
        out_shape=jax.ShapeDtypeStruct((rows, D_MODEL), F32),
        compiler_params=pltpu.CompilerParams(
            dimension_semantics=("parallel",), vmem_limit_bytes=VMEM_LIMIT),
        name="post",
    )(x2d, o, z, ysc, gnorm, w_out, g_post, g_pre, w_gate, w_up, w_down, g_fpost)


def kernel(x, meta_tokens, mix_pre_norm, mix_post_norm, ffn_pre_norm, ffn_post_norm, w_in, conv_qkv,
           a_log, dt_bias, gdn_norm, conv_sc, w_out, w_gate, w_up, w_down):
    batch, seq, _ = x.shape
    assert mix_pre_norm.shape[0] == 1, "single-layer block"
    assert batch % GDN_STREAMS == 0
    n_logit = 2 * HEADS
    cut = 4 * GDN_WIDTH
    w_in0 = w_in[0]
    w_main = jnp.concatenate([w_in0[:, :cut], w_in0[:, cut + n_logit:]], axis=1).astype(BF16)
    w_logit = jnp.pad(w_in0[:, cut:cut + n_logit], ((0, 0), (0, LANES - n_logit))).astype(BF16)
    lane_pad = lambda v: jnp.pad(v.reshape(1, HEADS), ((0, 0), (HEADS, LANES - 2 * HEADS)))
    conv_w = conv_qkv[0].reshape(4, 3 * HEADS, LANES).transpose(1, 0, 2)
    conv_s = conv_sc[0].reshape(3, HEADS, LANES).transpose(1, 0, 2)
    row = lambda v: v[0].reshape(1, -1)
    proj_args = (row(mix_pre_norm), w_main, w_logit, conv_w, conv_s, lane_pad(a_log[0]), lane_pad(dt_bias[0]))

    x2d = x.reshape(batch * seq, D_MODEL)
    no_halo = jnp.zeros((N_HALO, SUBLANES, LANES), F32)
    qkv_m, _, _, gates_m, tail_m = _inproj(meta_tokens, no_halo, *proj_args, N_META, N_META)
    state0 = _gdn_meta(qkv_m, gates_m)
    qkv, z, ysc, gates, _ = _inproj(x2d, tail_m, *proj_args, seq, min(INPROJ_ROWS, seq))
    o = _gdn(qkv, gates, state0, batch, seq, min(GDN_ROWS, seq))
    out = _post(x2d, o, z, ysc, row(gdn_norm), w_out[0].astype(BF16), row(mix_post_norm),
                row(ffn_pre_norm), w_gate[0].astype(BF16), w_up[0].astype(BF16),
                w_down[0].astype(BF16), row(ffn_post_norm), min(POST_ROWS, seq))
    return out.reshape(batch, seq, D_MODEL)
```

```python
import functools

import jax
import jax.numpy as jnp
from jax import lax
from jax.experimental import pallas as pl
from jax.experimental.pallas import tpu as pltpu

F32 = jnp.float32
BF16 = jnp.bfloat16

D_MODEL = 1024
N_META = 16
HEADS = 4
HEAD_DIM = 128
GDN_WIDTH = HEADS * HEAD_DIM
SC_WIDTH = D_MODEL - GDN_WIDTH
CHUNK = 64
D_FF = 2816
EPS = 1e-6
LANES = 128
SUBLANES = 8
GROUP = 4 * LANES
N_GROUPS = 7
N_HALO = 4 * HEADS
INV_BLK = 16
FF_TILE = 768
INPROJ_ROWS = 1024
GDN_ROWS = 512
POST_ROWS = 512
INPROJ_SPLIT = 2
POST_SPLIT = 2
GDN_STREAMS = 4
FACTOR_STREAMS = 2
VMEM_LIMIT = 56 * 1024 * 1024


def _sigmoid(x):
    return 1.0 / (1.0 + jnp.exp(-x))


def _silu(x):
    return x * _sigmoid(x)


def _softplus(x):
    return jnp.maximum(x, 0.0) + jnp.log1p(jnp.exp(-jnp.abs(x)))


def _rms_norm(x, gain):
    return x * lax.rsqrt(jnp.mean(x * x, axis=-1, keepdims=True) + EPS) * gain


def _l2_normalize(x):
    return x * lax.rsqrt(jnp.sum(x * x, axis=-1, keepdims=True) + EPS)


def _dot(a, b):
    return jnp.dot(a, b, preferred_element_type=F32)


def _dot_nt(a, b):
    return lax.dot_general(a, b, (((1,), (1,)), ((), ())), preferred_element_type=F32)


def _dot_tn(a, b):
    return lax.dot_general(a, b, (((0,), (0,)), ((), ())), preferred_element_type=F32)


def _split_bf16(x):
    hi = x.astype(BF16)
    return hi, (x - hi.astype(F32)).astype(BF16)


def _causal_conv(ext_ref, x, w):
    rows, taps = x.shape[0], w.shape[0]
    ext_ref[pl.ds(SUBLANES, rows), :] = x
    acc = x * w[taps - 1:taps]
    for i in range(taps - 1):
        acc = acc + ext_ref[pl.ds(SUBLANES - (taps - 1) + i, rows), :] * w[i:i + 1]
    ext_ref[pl.ds(0, SUBLANES), :] = x[rows - SUBLANES:rows]
    return acc


def _inproj_kernel(x_ref, g_ref, wa_ref, wsc_ref, cw_ref, csc_ref, alog_ref, dtb_ref, head_ref,
                   qkv_ref, z_ref, ysc_ref, gate_ref, tail_ref, ext_ref, *, tiles_per_seq):
    @pl.when(pl.program_id(0) % tiles_per_seq == 0)
    def _sequence_start():
        ext_ref[:, pl.ds(0, SUBLANES), :] = head_ref[...]

    rows_per = x_ref.shape[0] // min(INPROJ_SPLIT, x_ref.shape[0] // SUBLANES)
    parts = [pl.ds(r, rows_per) for r in range(0, x_ref.shape[0], rows_per)]
    xb = [_rms_norm(x_ref[rows, :], g_ref[...]).astype(BF16) for rows in parts]
    def group(i, j):
        w_ref, j0 = (wa_ref, 0) if j < 4 else (wsc_ref, 4)
        return _dot(xb[i], w_ref[:, GROUP * (j - j0):GROUP * (j - j0 + 1)])

    blocks = lambda r: [r[:, LANES * c:LANES * (c + 1)] for c in range(HEADS)]

    conv_block = lambda slot, x, w: _causal_conv(ext_ref.at[slot], x, w)

    for t in range(3):
        for i, rows in enumerate(parts):
            for h, x in enumerate(blocks(group(i, t))):
                y = _silu(conv_block(HEADS * t + h, x, cw_ref[HEADS * t + h]))
                if t == 0:
                    y = _l2_normalize(y) * (HEAD_DIM ** -0.5)
                elif t == 1:
                    y = _l2_normalize(y)
                qkv_ref[HEADS * t + h, rows, :] = y.astype(BF16)
    for i, rows in enumerate(parts):
        sc_x, sc_c = blocks(group(i, 4)), blocks(group(i, 6))
        conv = [conv_block(3 * HEADS + c, sc_c[c] * sc_x[c], csc_ref[c]) for c in range(HEADS)]
        sc_b = blocks(group(i, 5))
        ysc_ref[rows, :] = jnp.concatenate([sc_b[c] * conv[c] for c in range(HEADS)], axis=1).astype(BF16)
    for i, rows in enumerate(parts):
        logits = _dot(xb[i], wa_ref[:, 4 * GROUP:4 * GROUP + LANES])
        lane = lax.broadcasted_iota(jnp.int32, logits.shape, 1)
        gate_ref[rows, :] = jnp.where(lane < HEADS, _sigmoid(logits),
                                      -jnp.exp(alog_ref[...]) * _softplus(logits + dtb_ref[...]))
    for i, rows in enumerate(parts):
        z_ref[rows, :] = group(i, 3)
    tail_ref[...] = ext_ref[:, pl.ds(0, SUBLANES), :]


def _inproj(x2d, head_halo, gain, w_a, w_sc, conv_w, conv_sc, alog_row, dtb_row, seq, tm):
    rows = x2d.shape[0]
    n_tiles = rows // tm
    once = lambda shape: pl.BlockSpec(shape, lambda i: (0,) * len(shape), pipeline_mode=pl.Buffered(1))
    return pl.pallas_call(
        functools.partial(_inproj_kernel, tiles_per_seq=seq // tm),
        grid=(n_tiles,),
        in_specs=[
            pl.BlockSpec((tm, D_MODEL), lambda i: (i, 0)),
            once((1, D_MODEL)),
            once((D_MODEL, 4 * GROUP + LANES)),
            once((D_MODEL, 3 * GROUP)),
            once((3 * HEADS, 4, LANES)),
            once((HEADS, 3, LANES)),
            once((1, LANES)), once((1, LANES)),
            once((N_HALO, SUBLANES, LANES)),
        ],
        out_specs=[
            pl.BlockSpec((3 * HEADS, tm, LANES), lambda i: (0, i, 0)),
            pl.BlockSpec((tm, GDN_WIDTH), lambda i: (i, 0)),
            pl.BlockSpec((tm, SC_WIDTH), lambda i: (i, 0)),
            pl.BlockSpec((tm, LANES), lambda i: (i, 0)),
            pl.BlockSpec((N_HALO, SUBLANES, LANES), lambda i: (0, 0, 0)),
        ],
        out_shape=[
            jax.ShapeDtypeStruct((3 * HEADS, rows, LANES), BF16),
            jax.ShapeDtypeStruct((rows, GDN_WIDTH), F32),
            jax.ShapeDtypeStruct((rows, SC_WIDTH), BF16),
            jax.ShapeDtypeStruct((rows, LANES), F32),
            jax.ShapeDtypeStruct((N_HALO, SUBLANES, LANES), F32),
        ],
        scratch_shapes=[pltpu.VMEM((N_HALO, SUBLANES + tm // min(INPROJ_SPLIT, tm // SUBLANES), LANES), F32)],
        compiler_params=pltpu.CompilerParams(
            dimension_semantics=("arbitrary",), vmem_limit_bytes=VMEM_LIMIT),
        name="inproj",
    )(x2d, gain, w_a, w_sc, conv_w, conv_sc, alog_row, dtb_row, head_halo)


def _pair_diag(x0, x1):
    z = jnp.zeros_like(x0)
    return jnp.concatenate([jnp.concatenate([x0, z], axis=1), jnp.concatenate([z, x1], axis=1)], axis=0)


def _diag_block_inverse(a2s):
    nblk = CHUNK // INV_BLK
    lane = lax.broadcasted_iota(jnp.int32, (INV_BLK, LANES), 1)
    row = lax.broadcasted_iota(jnp.int32, (INV_BLK, LANES), 0)
    grp = (lane & (CHUNK - 1)) // INV_BLK
    base = (lane - (lane & (INV_BLK - 1)))[:SUBLANES]
    eye = ((lane & (INV_BLK - 1)) == row).astype(F32)
    zs, ds = [], []
    for a2 in a2s:
        z = jnp.zeros((INV_BLK, LANES), F32)
        for m in range(nblk):
            z = jnp.where(grp == m, a2[INV_BLK * m:INV_BLK * (m + 1), :], z)
        zs.append([z[SUBLANES * t:SUBLANES * (t + 1)] for t in range(INV_BLK // SUBLANES)])
        ds.append([eye[SUBLANES * t:SUBLANES * (t + 1)] for t in range(INV_BLK // SUBLANES)])
    for j in range(INV_BLK - 1):
        idx = base + j
        for z, d in zip(zs, ds):
            pivot = d[j // SUBLANES][j % SUBLANES:j % SUBLANES + 1, :]
            for t in range(INV_BLK // SUBLANES):
                if SUBLANES * (t + 1) - 1 > j:
                    d[t] = d[t] - jnp.take_along_axis(z[t], idx, axis=1) * pivot
    out = []
    for d in ds:
        dz = jnp.concatenate(d, axis=0)
        out.append(jnp.concatenate([jnp.where(grp == m, dz, 0.0) for m in range(nblk)], axis=0))
    return out


def _chunk_factors(chunks):
    row = lax.broadcasted_iota(jnp.int32, (CHUNK, LANES), 0)
    lane = lax.broadcasted_iota(jnp.int32, (CHUNK, LANES), 1)
    col = lane & (CHUNK - 1)
    left = lane < CHUNK
    pick = lambda a, b: jnp.where(left, a, b)
    incl2, strict2 = row >= col, row > col
    tri2 = incl2.astype(BF16)
    eye2 = (row == col).astype(BF16)
    eye_stack = jnp.concatenate([eye2, eye2], axis=0)
    off_diag = row // INV_BLK != col // INV_BLK

    cums = [_dot(tri2, jnp.concatenate(_split_bf16(c[3]), axis=0)) for c in chunks]
    ys = [jnp.where(lane >= HEADS, cum, c[3]) for cum, c in zip(cums, chunks)]
    rows_all = [_dot_tn(jnp.concatenate(_split_bf16(y), axis=0), eye_stack) for y in ys]

    outs, items = [], []
    for (q16, k16, v16, _), y, rows in zip(chunks, ys, rows_all):
        beta_c = [jnp.broadcast_to(y[:, h:h + 1], (CHUNK, LANES)) for h in range(HEADS)]
        gc_c = [jnp.broadcast_to(y[:, HEADS + h:HEADS + h + 1], (CHUNK, LANES)) for h in range(HEADS)]
        out = dict(u=[], w=[], eg=[], dk=[], gl=[], qk=[])
        for h in range(HEADS):
            gc_last = gc_c[h][CHUNK - 1:CHUNK, :]
            out["eg"].append(jnp.exp(gc_c[h]))
            out["dk"].append(jnp.exp(gc_last - gc_c[h]))
            out["gl"].append(jnp.exp(gc_last))
        outs.append(out)
        for h0 in range(0, HEADS, 2):
            h1 = h0 + 1
            beta_r = pick(rows[h0:h0 + 1, :], rows[h1:h1 + 1, :])
            gc_r = pick(rows[HEADS + h0:HEADS + h0 + 1, :], rows[HEADS + h1:HEADS + h1 + 1, :])
            decay = jnp.where(incl2, jnp.exp(jnp.where(incl2, pick(gc_c[h0], gc_c[h1]) - gc_r, 0.0)), 0.0)
            items.append(dict(out=out, beta_r=beta_r, w_scale=beta_r * jnp.exp(gc_r), decay=decay,
                              beta_c=pick(beta_c[h0], beta_c[h1]),
                              k_cat=jnp.concatenate([k16[h0], k16[h1]], axis=1),
                              q_cat=jnp.concatenate([q16[h0], q16[h1]], axis=1),
                              k_bd=_pair_diag(k16[h0], k16[h1]), v_bd=_pair_diag(v16[h0], v16[h1])))

    halves_diag = lambda x: jnp.concatenate([jnp.where(left, x, 0.0), jnp.where(left, 0.0, x)], axis=0)

    for it in items:
        kq = _dot_nt(jnp.concatenate([it["k_cat"], it["q_cat"]], axis=0), it["k_bd"])
        it["a2"] = jnp.where(strict2, it["beta_c"] * kq[:CHUNK] * it["decay"], 0.0)
        it["out"]["qk"].append(jnp.where(incl2, kq[CHUNK:] * it["decay"], 0.0).astype(BF16))

    for it, xd in zip(items, _diag_block_inverse([it["a2"] for it in items])):
        it["xd"] = xd
    for it in items:
        off_bd = halves_diag(jnp.where(off_diag, it["a2"], 0.0))
        it["n16"] = _dot(it["xd"].astype(BF16), off_bd.astype(BF16)).astype(BF16)
        it["t"] = it["xd"]
    for _ in range(CHUNK // INV_BLK - 1):
        for it in items:
            it["t"] = it["xd"] - _dot(it["n16"], halves_diag(it["t"]).astype(BF16))
    for it in items:
        u01 = _dot((it["t"] * it["beta_r"]).astype(BF16), it["v_bd"])
        w01 = _dot((it["t"] * it["w_scale"]).astype(BF16), it["k_bd"])
        for i in range(2):
            it["out"]["u"].append(u01[:, LANES * i:LANES * (i + 1)])
            it["out"]["w"].append(w01[:, LANES * i:LANES * (i + 1)].astype(BF16))
    return outs


def _chunk_recurrence(streams, need_out):
    work = []
    for q16, k16, f, states in streams:
        work.append(dict(q=q16, k=k16, f=f, s=states, s16=[x.astype(BF16) for x in states],
                         v_new=[], q_s=[], outs=[]))
    for t in work:
        for h in range(HEADS):
            ws = _dot(jnp.concatenate([t["f"]["w"][h], t["q"][h]], axis=0), t["s16"][h])
            t["v_new"].append(t["f"]["u"][h] - ws[:CHUNK])
            t["q_s"].append(ws[CHUNK:])
    for t in work:
        t["vn16"] = [v.astype(BF16) for v in t["v_new"]]
        t["vd16"] = [(v * d).astype(BF16) for v, d in zip(t["v_new"], t["f"]["dk"])]
    for t in work:
        t["kv"] = [_dot_tn(t["k"][h], t["vd16"][h]) for h in range(HEADS)]
    if need_out:
        for t in work:
            for h0 in range(0, HEADS, 2):
                intra = _dot(t["f"]["qk"][h0 // 2], _pair_diag(t["vn16"][h0], t["vn16"][h0 + 1]))
                for i in range(2):
                    t["outs"].append(t["f"]["eg"][h0 + i] * t["q_s"][h0 + i]
                                     + intra[:, LANES * i:LANES * (i + 1)])
    return [(t["outs"], [t["s"][h] * t["f"]["gl"][h] + t["kv"][h] for h in range(HEADS)]) for t in work]


_FACTOR_KEYS = ("u", "w", "eg", "dk", "gl")


def _gdn_kernel(q_ref, k_ref, v_ref, gate_ref, s0_ref, o_ref,
                s_ref, u_ref, w_ref, eg_ref, dk_ref, gl_ref, qk_ref, *, n_chunks):
    f_refs = dict(u=u_ref, w=w_ref, eg=eg_ref, dk=dk_ref, gl=gl_ref)

    @pl.when(pl.program_id(1) == 0)
    def _sequence_start():
        for b in range(GDN_STREAMS):
            s_ref[b] = s0_ref[...]

    def rows_of(ref, b, c):
        r0 = pl.multiple_of(c * CHUNK, CHUNK)
        return [ref[h, b, pl.ds(r0, CHUNK), :] for h in range(HEADS)]

    def factors(i, carry):
        bcs = [(i * FACTOR_STREAMS + j, c) for j in range(FACTOR_STREAMS) for c in range(n_chunks)]
        fs = _chunk_factors([(rows_of(q_ref, b, c), rows_of(k_ref, b, c), rows_of(v_ref, b, c),
                              gate_ref[b, pl.ds(c * CHUNK, CHUNK), :]) for b, c in bcs])
        for (b, c), f in zip(bcs, fs):
            slot = b * n_chunks + c
            for h in range(HEADS):
                for key in _FACTOR_KEYS:
                    val = f[key][h]
                    f_refs[key][slot, h] = jnp.broadcast_to(val, f_refs[key].shape[2:]) if key == "gl" else val
            for p in range(HEADS // 2):
                qk_ref[slot, p] = f["qk"][p]
        return carry

    def recurrence(c, carry):
        r0 = pl.multiple_of(c * CHUNK, CHUNK)
        streams = []
        for b in range(GDN_STREAMS):
            slot = b * n_chunks + c
            f = {key: [f_refs[key][slot, h] for h in range(HEADS)] for key in _FACTOR_KEYS}
            f["gl"] = [g[:1] for g in f["gl"]]
            f["qk"] = [qk_ref[slot, p] for p in range(HEADS // 2)]
            streams.append((rows_of(q_ref, b, c), rows_of(k_ref, b, c), f,
                            [s_ref[b, h] for h in range(HEADS)]))
        for b, (outs, st) in enumerate(_chunk_recurrence(streams, True)):
            for h in range(HEADS):
                o_ref[b, pl.ds(r0, CHUNK), LANES * h:LANES * (h + 1)] = outs[h]
                s_ref[b, h] = st[h]
        return carry

    lax.fori_loop(0, GDN_STREAMS // FACTOR_STREAMS, factors, 0)
    lax.fori_loop(0, n_chunks, recurrence, 0)


def _gdn(qkv, gates, state0, batch, seq, ts):
    ns = seq // ts
    nc = ts // CHUNK
    nb = GDN_STREAMS
    blk = lambda t: pl.BlockSpec((HEADS, nb, ts, LANES), lambda b, s: (t, b, s, 0))
    per_head = lambda rows, dtype: pltpu.VMEM((nb * nc, HEADS, rows, LANES), dtype)
    o = pl.pallas_call(
        functools.partial(_gdn_kernel, n_chunks=nc),
        grid=(batch // nb, ns),
        in_specs=[
            blk(0), blk(1), blk(2),
            pl.BlockSpec((nb, ts, LANES), lambda b, s: (b, s, 0)),
            pl.BlockSpec((HEADS, HEAD_DIM, HEAD_DIM), lambda b, s: (0, 0, 0)),
        ],
        out_specs=pl.BlockSpec((nb, ts, GDN_WIDTH), lambda b, s: (b, s, 0)),
        out_shape=jax.ShapeDtypeStruct((batch, seq, GDN_WIDTH), F32),
        scratch_shapes=[
            pltpu.VMEM((nb, HEADS, HEAD_DIM, HEAD_DIM), F32),
            per_head(CHUNK, F32), per_head(CHUNK, BF16), per_head(CHUNK, F32), per_head(CHUNK, F32),
            per_head(SUBLANES, F32),
            pltpu.VMEM((nb * nc, HEADS // 2, CHUNK, LANES), BF16),
        ],
        compiler_params=pltpu.CompilerParams(
            dimension_semantics=("parallel", "arbitrary"), vmem_limit_bytes=VMEM_LIMIT),
        name="gdn",
    )(*([qkv.reshape(3 * HEADS, batch, seq, LANES)] * 3), gates.reshape(batch, seq, LANES), state0)
    return o.reshape(batch * seq, GDN_WIDTH)


def _gdn_meta_kernel(k_ref, v_ref, gate_ref, s_ref):
    pad = CHUNK - N_META
    front = lambda x: jnp.concatenate([jnp.zeros((pad, LANES), x.dtype), x], axis=0)
    k16 = [front(k_ref[h]) for h in range(HEADS)]
    v16 = [front(v_ref[h]) for h in range(HEADS)]
    f, = _chunk_factors([(k16, k16, v16, front(gate_ref[...]))])
    zero = [jnp.zeros((HEAD_DIM, HEAD_DIM), F32)] * HEADS
    (_, st), = _chunk_recurrence([(k16, k16, f, zero)], False)
    for h in range(HEADS):
        s_ref[h] = st[h]


def _gdn_meta(qkv_meta, gates_meta):
    blk = lambda t: pl.BlockSpec((HEADS, N_META, LANES), lambda i: (t, 0, 0))
    return pl.pallas_call(
        _gdn_meta_kernel,
        grid=(1,),
        in_specs=[blk(1), blk(2), pl.BlockSpec((N_META, LANES), lambda i: (0, 0))],
        out_specs=pl.BlockSpec((HEADS, HEAD_DIM, HEAD_DIM), lambda i: (0, 0, 0)),
        out_shape=jax.ShapeDtypeStruct((HEADS, HEAD_DIM, HEAD_DIM), F32),
        name="gdn_meta",
    )(qkv_meta, qkv_meta, gates_meta)


def _post_kernel(x_ref, o_ref, z_ref, ysc_ref, gn_ref, wo_ref, g_post_ref, g_pre_ref, wg_ref, wu_ref,
                 wd_ref, g_fpost_ref, out_ref):
    rows_per = x_ref.shape[0] // POST_SPLIT
    parts = [pl.ds(i * rows_per, rows_per) for i in range(POST_SPLIT)]
    gn = gn_ref[...]
    mixes = []
    for rows in parts:
        heads = []
        for h in range(HEADS):
            sl = slice(HEAD_DIM * h, HEAD_DIM * (h + 1))
            heads.append((_rms_norm(o_ref[rows, sl], gn) * _silu(z_ref[rows, sl])).astype(BF16))
        mixes.append(_dot(jnp.concatenate(heads + [ysc_ref[rows, :]], axis=1), wo_ref[...]))
    h1 = [x_ref[rows, :] + _rms_norm(mix, g_post_ref[...]) for rows, mix in zip(parts, mixes)]
    u = [_rms_norm(h, g_pre_ref[...]).astype(BF16) for h in h1]
    ffn = [jnp.zeros((rows_per, D_MODEL), F32)] * POST_SPLIT
    for lo in range(0, D_FF, FF_TILE):
        sl = slice(lo, min(lo + FF_TILE, D_FF))
        for i in range(POST_SPLIT):
            act = (_silu(_dot(u[i], wg_ref[:, sl])) * _dot(u[i], wu_ref[:, sl])).astype(BF16)
            ffn[i] = ffn[i] + _dot(act, wd_ref[sl, :])
    for i, rows in enumerate(parts):
        out_ref[rows, :] = h1[i] + _rms_norm(ffn[i], g_fpost_ref[...])


def _post(x2d, o, z, ysc, gnorm, w_out, g_post, g_pre, w_gate, w_up, w_down, g_fpost, tm):
    rows = x2d.shape[0]
    row_blk = lambda width: pl.BlockSpec((tm, width), lambda i: (i, 0))
    once = lambda shape: pl.BlockSpec(shape, lambda i: (0, 0), pipeline_mode=pl.Buffered(1))
    return pl.pallas_call(
        _post_kernel,
        grid=(rows // tm,),
        in_specs=[
            row_blk(D_MODEL), row_blk(GDN_WIDTH), row_blk(GDN_WIDTH), row_blk(SC_WIDTH),
            once((1, HEAD_DIM)),
            once((D_MODEL, D_MODEL)), once((1, D_MODEL)), once((1, D_MODEL)),
            once((D_MODEL, D_FF)), once((D_MODEL, D_FF)), once((D_FF, D_MODEL)), once((1, D_MODEL)),
        ],
        out_specs=row_blk(D_MODEL),
        out_shape=jax.ShapeDtypeStruct((rows, D_MODEL), F32),
        compiler_params=pltpu.CompilerParams(
            dimension_semantics=("parallel",), vmem_limit_bytes=VMEM_LIMIT),
        name="post",
    )(x2d, o, z, ysc, gnorm, w_out, g_post, g_pre, w_gate, w_up, w_down, g_fpost)


def kernel(x, meta_tokens, mix_pre_norm, mix_post_norm, ffn_pre_norm, ffn_post_norm, w_in, conv_qkv,
           a_log, dt_bias, gdn_norm, conv_sc, w_out, w_gate, w_up, w_down):
    batch, seq, _ = x.shape
    assert mix_pre_norm.shape[0] == 1, "single-layer block"
    assert batch % GDN_STREAMS == 0
    n_logit = 2 * HEADS
    cut = 4 * GDN_WIDTH
    w_in0 = w_in[0]
    w_a = w_in0[:, :cut + LANES].astype(BF16)
    w_sc = w_in0[:, cut + n_logit:].astype(BF16)
    lane_pad = lambda v: jnp.pad(v.reshape(1, HEADS), ((0, 0), (HEADS, LANES - 2 * HEADS)))
    conv_w = conv_qkv[0].reshape(4, 3 * HEADS, LANES).transpose(1, 0, 2)
    conv_s = conv_sc[0].reshape(3, HEADS, LANES).transpose(1, 0, 2)
    row = lambda v: v[0].reshape(1, -1)
    proj_args = (row(mix_pre_norm), w_a, w_sc, conv_w, conv_s, lane_pad(a_log[0]), lane_pad(dt_bias[0]))

    x2d = x.reshape(batch * seq, D_MODEL)
    no_halo = jnp.zeros((N_HALO, SUBLANES, LANES), F32)
    qkv_m, _, _, gates_m, tail_m = _inproj(meta_tokens, no_halo, *proj_args, N_META, N_META)
    state0 = _gdn_meta(qkv_m, gates_m)
    qkv, z, ysc, gates, _ = _inproj(x2d, tail_m, *proj_args, seq, min(INPROJ_ROWS, seq))
    o = _gdn(qkv, gates, state0, batch, seq, min(GDN_ROWS, seq))
    out = _post(x2d, o, z, ysc, row(gdn_norm), w_out[0].astype(BF16), row(mix_post_norm),
                row(ffn_pre_norm), w_gate[0].astype(BF16), w_up[0].astype(BF16),
                w_down[0].astype(BF16), row(ffn_post_norm), min(POST_ROWS, seq))
    return out.reshape(batch, seq, D_MODEL)
```

```python
import functools

import jax
import jax.numpy as jnp
from jax import lax
from jax.experimental import pallas as pl
from jax.experimental.pallas import tpu as pltpu

F32 = jnp.float32
BF16 = jnp.bfloat16

D_MODEL = 1024
N_META = 16
HEADS = 4
HEAD_DIM = 128
GDN_WIDTH = HEADS * HEAD_DIM
SC_WIDTH = D_MODEL - GDN_WIDTH
CHUNK = 64
D_FF = 2816
EPS = 1e-6
LANES = 128
SUBLANES = 8
GROUP = 4 * LANES
N_GROUPS = 7
N_HALO = 4 * HEADS
INV_BLK = 16
FF_TILE = 768
INPROJ_ROWS = 1024
GDN_ROWS = 512
POST_ROWS = 512
INPROJ_SPLIT = 2
POST_SPLIT = 2
GDN_STREAMS = 4
FACTOR_STREAMS = 2
VMEM_LIMIT = 56 * 1024 * 1024


def _sigmoid(x):
    return 1.0 / (1.0 + jnp.exp(-x))


def _silu(x):
    return x * _sigmoid(x)


def _softplus(x):
    return jnp.maximum(x, 0.0) + jnp.log1p(jnp.exp(-jnp.abs(x)))


def _rms_norm(x, gain):
    return x * lax.rsqrt(jnp.mean(x * x, axis=-1, keepdims=True) + EPS) * gain


def _l2_normalize(x):
    return x * lax.rsqrt(jnp.sum(x * x, axis=-1, keepdims=True) + EPS)


def _dot(a, b):
    return jnp.dot(a, b, preferred_element_type=F32)


def _dot_nt(a, b):
    return lax.dot_general(a, b, (((1,), (1,)), ((), ())), preferred_element_type=F32)


def _dot_tn(a, b):
    return lax.dot_general(a, b, (((0,), (0,)), ((), ())), preferred_element_type=F32)


def _split_bf16(x):
    hi = x.astype(BF16)
    return hi, (x - hi.astype(F32)).astype(BF16)


def _causal_conv(ext_ref, x, w):
    rows, taps = x.shape[0], w.shape[0]
    ext_ref[pl.ds(SUBLANES, rows), :] = x
    acc = x * w[taps - 1:taps]
    for i in range(taps - 1):
        acc = acc + ext_ref[pl.ds(SUBLANES - (taps - 1) + i, rows), :] * w[i:i + 1]
    ext_ref[pl.ds(0, SUBLANES), :] = x[rows - SUBLANES:rows]
    return acc


def _inproj_kernel(x_ref, g_ref, wa_ref, wsc_ref, cw_ref, csc_ref, alog_ref, dtb_ref, head_ref,
                   qkv_ref, z_ref, ysc_ref, gate_ref, tail_ref, ext_ref, *, tiles_per_seq):
    @pl.when(pl.program_id(0) % tiles_per_seq == 0)
    def _sequence_start():
        ext_ref[:, pl.ds(0, SUBLANES), :] = head_ref[...]

    rows_per = x_ref.shape[0] // min(INPROJ_SPLIT, x_ref.shape[0] // SUBLANES)
    parts = [pl.ds(r, rows_per) for r in range(0, x_ref.shape[0], rows_per)]
    xb = [_rms_norm(x_ref[rows, :], g_ref[...]).astype(BF16) for rows in parts]
    def group(i, j):
        w_ref, j0 = (wa_ref, 0) if j < 4 else (wsc_ref, 4)
        return _dot(xb[i], w_ref[:, GROUP * (j - j0):GROUP * (j - j0 + 1)])

    blocks = lambda r: [r[:, LANES * c:LANES * (c + 1)] for c in range(HEADS)]

    conv_block = lambda slot, x, w: _causal_conv(ext_ref.at[slot], x, w)

    for t in range(3):
        for i, rows in enumerate(parts):
            for h, x in enumerate(blocks(group(i, t))):
                y = _silu(conv_block(HEADS * t + h, x, cw_ref[HEADS * t + h]))
                if t == 0:
                    y = _l2_normalize(y) * (HEAD_DIM ** -0.5)
                elif t == 1:
                    y = _l2_normalize(y)
                qkv_ref[HEADS * t + h, rows, :] = y.astype(BF16)
    for i, rows in enumerate(parts):
        sc_x, sc_c = blocks(group(i, 4)), blocks(group(i, 6))
        conv = [conv_block(3 * HEADS + c, sc_c[c] * sc_x[c], csc_ref[c]) for c in range(HEADS)]
        sc_b = blocks(group(i, 5))
        ysc_ref[rows, :] = jnp.concatenate([sc_b[c] * conv[c] for c in range(HEADS)], axis=1).astype(BF16)
    for i, rows in enumerate(parts):
        logits = _dot(xb[i], wa_ref[:, 4 * GROUP:4 * GROUP + LANES])
        lane = lax.broadcasted_iota(jnp.int32, logits.shape, 1)
        gate_ref[rows, :] = jnp.where(lane < HEADS, _sigmoid(logits),
                                      -jnp.exp(alog_ref[...]) * _softplus(logits + dtb_ref[...]))
    for i, rows in enumerate(parts):
        z_ref[rows, :] = group(i, 3)
    tail_ref[...] = ext_ref[:, pl.ds(0, SUBLANES), :]


def _inproj(x2d, head_halo, gain, w_a, w_sc, conv_w, conv_sc, alog_row, dtb_row, seq, tm):
    rows = x2d.shape[0]
    n_tiles = rows // tm
    once = lambda shape: pl.BlockSpec(shape, lambda i: (0,) * len(shape), pipeline_mode=pl.Buffered(1))
    return pl.pallas_call(
        functools.partial(_inproj_kernel, tiles_per_seq=seq // tm),
        grid=(n_tiles,),
        in_specs=[
            pl.BlockSpec((tm, D_MODEL), lambda i: (i, 0)),
            once((1, D_MODEL)),
            once((D_MODEL, 4 * GROUP + LANES)),
            once((D_MODEL, 3 * GROUP)),
            once((3 * HEADS, 4, LANES)),
            once((HEADS, 3, LANES)),
            once((1, LANES)), once((1, LANES)),
            once((N_HALO, SUBLANES, LANES)),
        ],
        out_specs=[
            pl.BlockSpec((3 * HEADS, tm, LANES), lambda i: (0, i, 0)),
            pl.BlockSpec((tm, GDN_WIDTH), lambda i: (i, 0)),
            pl.BlockSpec((tm, SC_WIDTH), lambda i: (i, 0)),
            pl.BlockSpec((tm, LANES), lambda i: (i, 0)),
            pl.BlockSpec((N_HALO, SUBLANES, LANES), lambda i: (0, 0, 0)),
        ],
        out_shape=[
            jax.ShapeDtypeStruct((3 * HEADS, rows, LANES), BF16),
            jax.ShapeDtypeStruct((rows, GDN_WIDTH), F32),
            jax.ShapeDtypeStruct((rows, SC_WIDTH), BF16),
            jax.ShapeDtypeStruct((rows, LANES), F32),
            jax.ShapeDtypeStruct((N_HALO, SUBLANES, LANES), F32),
        ],
        scratch_shapes=[pltpu.VMEM((N_HALO, SUBLANES + tm // min(INPROJ_SPLIT, tm // SUBLANES), LANES), F32)],
        compiler_params=pltpu.CompilerParams(
            dimension_semantics=("arbitrary",), vmem_limit_bytes=VMEM_LIMIT),
        name="inproj",
    )(x2d, gain, w_a, w_sc, conv_w, conv_sc, alog_row, dtb_row, head_halo)


def _pair_diag(x0, x1):
    z = jnp.zeros_like(x0)
    return jnp.concatenate([jnp.concatenate([x0, z], axis=1), jnp.concatenate([z, x1], axis=1)], axis=0)


def _diag_block_inverse(a2s):
    nblk = CHUNK // INV_BLK
    lane = lax.broadcasted_iota(jnp.int32, (INV_BLK, LANES), 1)
    row = lax.broadcasted_iota(jnp.int32, (INV_BLK, LANES), 0)
    grp = (lane & (CHUNK - 1)) // INV_BLK
    base = (lane - (lane & (INV_BLK - 1)))[:SUBLANES]
    eye = ((lane & (INV_BLK - 1)) == row).astype(F32)
    zs, ds = [], []
    for a2 in a2s:
        z = jnp.zeros((INV_BLK, LANES), F32)
        for m in range(nblk):
            z = jnp.where(grp == m, a2[INV_BLK * m:INV_BLK * (m + 1), :], z)
        zs.append([z[SUBLANES * t:SUBLANES * (t + 1)] for t in range(INV_BLK // SUBLANES)])
        ds.append([eye[SUBLANES * t:SUBLANES * (t + 1)] for t in range(INV_BLK // SUBLANES)])
    for j in range(INV_BLK - 1):
        idx = base + j
        for z, d in zip(zs, ds):
            pivot = d[j // SUBLANES][j % SUBLANES:j % SUBLANES + 1, :]
            for t in range(INV_BLK // SUBLANES):
                if SUBLANES * (t + 1) - 1 > j:
                    d[t] = d[t] - jnp.take_along_axis(z[t], idx, axis=1) * pivot
    out = []
    for d in ds:
        dz = jnp.concatenate(d, axis=0)
        out.append(jnp.concatenate([jnp.where(grp == m, dz, 0.0) for m in range(nblk)], axis=0))
    return out


def _chunk_factors(chunks):
    row = lax.broadcasted_iota(jnp.int32, (CHUNK, LANES), 0)
    lane = lax.broadcasted_iota(jnp.int32, (CHUNK, LANES), 1)
    col = lane & (CHUNK - 1)
    left = lane < CHUNK
    pick = lambda a, b: jnp.where(left, a, b)
    incl2, strict2 = row >= col, row > col
    tri2 = incl2.astype(BF16)
    eye2 = (row == col).astype(BF16)
    eye_stack = jnp.concatenate([eye2, eye2], axis=0)
    off_diag = row // INV_BLK != col // INV_BLK

    cums = [_dot(tri2, jnp.concatenate(_split_bf16(c[3]), axis=0)) for c in chunks]
    ys = [jnp.where(lane >= HEADS, cum, c[3]) for cum, c in zip(cums, chunks)]
    rows_all = [_dot_tn(jnp.concatenate(_split_bf16(y), axis=0), eye_stack) for y in ys]

    outs, items = [], []
    for (q16, k16, v16, _), y, rows in zip(chunks, ys, rows_all):
        beta_c = [jnp.broadcast_to(y[:, h:h + 1], (CHUNK, LANES)) for h in range(HEADS)]
        gc_c = [jnp.broadcast_to(y[:, HEADS + h:HEADS + h + 1], (CHUNK, LANES)) for h in range(HEADS)]
        out = dict(u=[], w=[], eg=[], dk=[], gl=[], qk=[])
        for h in range(HEADS):
            gc_last = gc_c[h][CHUNK - 1:CHUNK, :]
            out["eg"].append(jnp.exp(gc_c[h]))
            out["dk"].append(jnp.exp(gc_last - gc_c[h]))
            out["gl"].append(jnp.exp(gc_last))
        outs.append(out)
        for h0 in range(0, HEADS, 2):
            h1 = h0 + 1
            beta_r = pick(rows[h0:h0 + 1, :], rows[h1:h1 + 1, :])
            gc_r = pick(rows[HEADS + h0:HEADS + h0 + 1, :], rows[HEADS + h1:HEADS + h1 + 1, :])
            decay = jnp.where(incl2, jnp.exp(jnp.where(incl2, pick(gc_c[h0], gc_c[h1]) - gc_r, 0.0)), 0.0)
            items.append(dict(out=out, beta_r=beta_r, w_scale=beta_r * jnp.exp(gc_r), decay=decay,
                              beta_c=pick(beta_c[h0], beta_c[h1]),
                              k_cat=jnp.concatenate([k16[h0], k16[h1]], axis=1),
                              q_cat=jnp.concatenate([q16[h0], q16[h1]], axis=1),
                              k_bd=_pair_diag(k16[h0], k16[h1]), v_bd=_pair_diag(v16[h0], v16[h1])))

    halves_diag = lambda x: jnp.concatenate([jnp.where(left, x, 0.0), jnp.where(left, 0.0, x)], axis=0)

    for it in items:
        kq = _dot_nt(jnp.concatenate([it["k_cat"], it["q_cat"]], axis=0), it["k_bd"])
        it["a2"] = jnp.where(strict2, it["beta_c"] * kq[:CHUNK] * it["decay"], 0.0)
        it["out"]["qk"].append(jnp.where(incl2, kq[CHUNK:] * it["decay"], 0.0).astype(BF16))

    for it, xd in zip(items, _diag_block_inverse([it["a2"] for it in items])):
        it["xd"] = xd
    for it in items:
        off_bd = halves_diag(jnp.where(off_diag, it["a2"], 0.0))
        it["n16"] = _dot(it["xd"].astype(BF16), off_bd.astype(BF16)).astype(BF16)
        it["t"] = it["xd"]
    for _ in range(CHUNK // INV_BLK - 1):
        for it in items:
            it["t"] = it["xd"] - _dot(it["n16"], halves_diag(it["t"]).astype(BF16))
    for it in items:
        u01 = _dot((it["t"] * it["beta_r"]).astype(BF16), it["v_bd"])
        w01 = _dot((it["t"] * it["w_scale"]).astype(BF16), it["k_bd"])
        for i in range(2):
            it["out"]["u"].append(u01[:, LANES * i:LANES * (i + 1)])
            it["out"]["w"].append(w01[:, LANES * i:LANES * (i + 1)].astype(BF16))
    return outs


def _chunk_recurrence(streams, need_out):
    work = []
    for q16, k16, f, states in streams:
        work.append(dict(q=q16, k=k16, f=f, s=states, s16=[x.astype(BF16) for x in states],
                         v_new=[], q_s=[], outs=[]))
    for t in work:
        for h in range(HEADS):
            ws = _dot(jnp.concatenate([t["f"]["w"][h], t["q"][h]], axis=0), t["s16"][h])
            t["v_new"].append(t["f"]["u"][h] - ws[:CHUNK])
            t["q_s"].append(ws[CHUNK:])
    for t in work:
        t["vn16"] = [v.astype(BF16) for v in t["v_new"]]
        t["vd16"] = [(v * d).astype(BF16) for v, d in zip(t["v_new"], t["f"]["dk"])]
    for t in work:
        t["kv"] = [_dot_tn(t["k"][h], t["vd16"][h]) for h in range(HEADS)]
    if need_out:
        for t in work:
            for h0 in range(0, HEADS, 2):
                intra = _dot(t["f"]["qk"][h0 // 2], _pair_diag(t["vn16"][h0], t["vn16"][h0 + 1]))
                for i in range(2):
                    t["outs"].append(t["f"]["eg"][h0 + i] * t["q_s"][h0 + i]
                                     + intra[:, LANES * i:LANES * (i + 1)])
    return [(t["outs"], [t["s"][h] * t["f"]["gl"][h] + t["kv"][h] for h in range(HEADS)]) for t in work]


_FACTOR_KEYS = ("u", "w", "eg", "dk", "gl")


def _gdn_kernel(*refs, n_chunks, n_cast):
    q_ref, k_ref, v_ref, gate_ref, s0_ref = refs[:5]
    cast_in, o_ref, cast_out = refs[5:5 + n_cast], refs[5 + n_cast], refs[6 + n_cast:6 + 2 * n_cast]
    s_ref, u_ref, w_ref, eg_ref, dk_ref, gl_ref, qk_ref = refs[6 + 2 * n_cast:]
    f_refs = dict(u=u_ref, w=w_ref, eg=eg_ref, dk=dk_ref, gl=gl_ref)
    for src, dst in zip(cast_in, cast_out):
        dst[...] = src[...].astype(BF16)

    @pl.when(pl.program_id(1) == 0)
    def _sequence_start():
        for b in range(GDN_STREAMS):
            s_ref[b] = s0_ref[...]

    def rows_of(ref, b, c):
        r0 = pl.multiple_of(c * CHUNK, CHUNK)
        return [ref[h, b, pl.ds(r0, CHUNK), :] for h in range(HEADS)]

    def factors(i, carry):
        bcs = [(i * FACTOR_STREAMS + j, c) for j in range(FACTOR_STREAMS) for c in range(n_chunks)]
        fs = _chunk_factors([(rows_of(q_ref, b, c), rows_of(k_ref, b, c), rows_of(v_ref, b, c),
                              gate_ref[b, pl.ds(c * CHUNK, CHUNK), :]) for b, c in bcs])
        for (b, c), f in zip(bcs, fs):
            slot = b * n_chunks + c
            for h in range(HEADS):
                for key in _FACTOR_KEYS:
                    val = f[key][h]
                    f_refs[key][slot, h] = jnp.broadcast_to(val, f_refs[key].shape[2:]) if key == "gl" else val
            for p in range(HEADS // 2):
                qk_ref[slot, p] = f["qk"][p]
        return carry

    def recurrence(c, carry):
        r0 = pl.multiple_of(c * CHUNK, CHUNK)
        streams = []
        for b in range(GDN_STREAMS):
            slot = b * n_chunks + c
            f = {key: [f_refs[key][slot, h] for h in range(HEADS)] for key in _FACTOR_KEYS}
            f["gl"] = [g[:1] for g in f["gl"]]
            f["qk"] = [qk_ref[slot, p] for p in range(HEADS // 2)]
            streams.append((rows_of(q_ref, b, c), rows_of(k_ref, b, c), f,
                            [s_ref[b, h] for h in range(HEADS)]))
        for b, (outs, st) in enumerate(_chunk_recurrence(streams, True)):
            for h in range(HEADS):
                o_ref[b, pl.ds(r0, CHUNK), LANES * h:LANES * (h + 1)] = outs[h]
                s_ref[b, h] = st[h]
        return carry

    lax.fori_loop(0, GDN_STREAMS // FACTOR_STREAMS, factors, 0)
    lax.fori_loop(0, n_chunks, recurrence, 0)


def _gdn(qkv, gates, state0, cast_weights, batch, seq, ts):
    ns = seq // ts
    nc = ts // CHUNK
    nb = GDN_STREAMS
    n_steps = (batch // nb) * ns
    blk = lambda t: pl.BlockSpec((HEADS, nb, ts, LANES), lambda b, s: (t, b, s, 0))
    per_head = lambda rows, dtype: pltpu.VMEM((nb * nc, HEADS, rows, LANES), dtype)
    slab_specs = []
    for w in cast_weights:
        slab = w.shape[0] // n_steps
        assert slab * n_steps == w.shape[0] and slab % (2 * SUBLANES) == 0, (w.shape, n_steps)
        slab_specs.append(pl.BlockSpec((slab, w.shape[1]), lambda b, s: (b * ns + s, 0)))
    o, *cast = pl.pallas_call(
        functools.partial(_gdn_kernel, n_chunks=nc, n_cast=len(cast_weights)),
        grid=(batch // nb, ns),
        in_specs=[
            blk(0), blk(1), blk(2),
            pl.BlockSpec((nb, ts, LANES), lambda b, s: (b, s, 0)),
            pl.BlockSpec((HEADS, HEAD_DIM, HEAD_DIM), lambda b, s: (0, 0, 0)),
        ] + slab_specs,
        out_specs=[pl.BlockSpec((nb, ts, GDN_WIDTH), lambda b, s: (b, s, 0))] + slab_specs,
        out_shape=[jax.ShapeDtypeStruct((batch, seq, GDN_WIDTH), F32)]
        + [jax.ShapeDtypeStruct(w.shape, BF16) for w in cast_weights],
        scratch_shapes=[
            pltpu.VMEM((nb, HEADS, HEAD_DIM, HEAD_DIM), F32),
            per_head(CHUNK, F32), per_head(CHUNK, BF16), per_head(CHUNK, F32), per_head(CHUNK, F32),
            per_head(SUBLANES, F32),
            pltpu.VMEM((nb * nc, HEADS // 2, CHUNK, LANES), BF16),
        ],
        compiler_params=pltpu.CompilerParams(
            dimension_semantics=("parallel", "arbitrary"), vmem_limit_bytes=VMEM_LIMIT),
        name="gdn",
    )(*([qkv.reshape(3 * HEADS, batch, seq, LANES)] * 3), gates.reshape(batch, seq, LANES), state0,
      *cast_weights)
    return o.reshape(batch * seq, GDN_WIDTH), cast


def _gdn_meta_kernel(k_ref, v_ref, gate_ref, s_ref):
    pad = CHUNK - N_META
    front = lambda x: jnp.concatenate([jnp.zeros((pad, LANES), x.dtype), x], axis=0)
    k16 = [front(k_ref[h]) for h in range(HEADS)]
    v16 = [front(v_ref[h]) for h in range(HEADS)]
    f, = _chunk_factors([(k16, k16, v16, front(gate_ref[...]))])
    zero = [jnp.zeros((HEAD_DIM, HEAD_DIM), F32)] * HEADS
    (_, st), = _chunk_recurrence([(k16, k16, f, zero)], False)
    for h in range(HEADS):
        s_ref[h] = st[h]


def _gdn_meta(qkv_meta, gates_meta):
    blk = lambda t: pl.BlockSpec((HEADS, N_META, LANES), lambda i: (t, 0, 0))
    return pl.pallas_call(
        _gdn_meta_kernel,
        grid=(1,),
        in_specs=[blk(1), blk(2), pl.BlockSpec((N_META, LANES), lambda i: (0, 0))],
        out_specs=pl.BlockSpec((HEADS, HEAD_DIM, HEAD_DIM), lambda i: (0, 0, 0)),
        out_shape=jax.ShapeDtypeStruct((HEADS, HEAD_DIM, HEAD_DIM), F32),
        name="gdn_meta",
    )(qkv_meta, qkv_meta, gates_meta)


def _post_kernel(x_ref, o_ref, z_ref, ysc_ref, gn_ref, wo_ref, g_post_ref, g_pre_ref, wg_ref, wu_ref,
                 wd_ref, g_fpost_ref, out_ref):
    rows_per = x_ref.shape[0] // POST_SPLIT
    parts = [pl.ds(i * rows_per, rows_per) for i in range(POST_SPLIT)]
    gn = gn_ref[...]
    mixes = []
    for rows in parts:
        heads = []
        for h in range(HEADS):
            sl = slice(HEAD_DIM * h, HEAD_DIM * (h + 1))
            heads.append((_rms_norm(o_ref[rows, sl], gn) * _silu(z_ref[rows, sl])).astype(BF16))
        mixes.append(_dot(jnp.concatenate(heads + [ysc_ref[rows, :]], axis=1), wo_ref[...]))
    h1 = [x_ref[rows, :] + _rms_norm(mix, g_post_ref[...]) for rows, mix in zip(parts, mixes)]
    u = [_rms_norm(h, g_pre_ref[...]).astype(BF16) for h in h1]
    ffn = [jnp.zeros((rows_per, D_MODEL), F32)] * POST_SPLIT
    for lo in range(0, D_FF, FF_TILE):
        sl = slice(lo, min(lo + FF_TILE, D_FF))
        for i in range(POST_SPLIT):
            act = (_silu(_dot(u[i], wg_ref[:, sl])) * _dot(u[i], wu_ref[:, sl])).astype(BF16)
            ffn[i] = ffn[i] + _dot(act, wd_ref[sl, :])
    for i, rows in enumerate(parts):
        out_ref[rows, :] = h1[i] + _rms_norm(ffn[i], g_fpost_ref[...])


def _post(x2d, o, z, ysc, gnorm, w_out, g_post, g_pre, w_gate, w_up, w_down, g_fpost, tm):
    rows = x2d.shape[0]
    row_blk = lambda width: pl.BlockSpec((tm, width), lambda i: (i, 0))
    once = lambda shape: pl.BlockSpec(shape, lambda i: (0, 0), pipeline_mode=pl.Buffered(1))
    return pl.pallas_call(
        _post_kernel,
        grid=(rows // tm,),
        in_specs=[
            row_blk(D_MODEL), row_blk(GDN_WIDTH), row_blk(GDN_WIDTH), row_blk(SC_WIDTH),
            once((1, HEAD_DIM)),
            once((D_MODEL, D_MODEL)), once((1, D_MODEL)), once((1, D_MODEL)),
            once((D_MODEL, D_FF)), once((D_MODEL, D_FF)), once((D_FF, D_MODEL)), once((1, D_MODEL)),
        ],
        out_specs=row_blk(D_MODEL),
        out_shape=jax.ShapeDtypeStruct((rows, D_MODEL), F32),
        compiler_params=pltpu.CompilerParams(
            dimension_semantics=("parallel",), vmem_limit_bytes=VMEM_LIMIT),
        name="post",
    )(x2d, o, z, ysc, gnorm, w_out, g_post, g_pre, w_gate, w_up, w_down, g_fpost)


def kernel(x, meta_tokens, mix_pre_norm, mix_post_norm, ffn_pre_norm, ffn_post_norm, w_in, conv_qkv,
           a_log, dt_bias, gdn_norm, conv_sc, w_out, w_gate, w_up, w_down):
    batch, seq, _ = x.shape
    assert mix_pre_norm.shape[0] == 1, "single-layer block"
    assert batch % GDN_STREAMS == 0
    n_logit = 2 * HEADS
    cut = 4 * GDN_WIDTH
    w_in0 = w_in[0]
    w_a = w_in0[:, :cut + LANES].astype(BF16)
    w_sc = w_in0[:, cut + n_logit:].astype(BF16)
    lane_pad = lambda v: jnp.pad(v.reshape(1, HEADS), ((0, 0), (HEADS, LANES - 2 * HEADS)))
    conv_w = conv_qkv[0].reshape(4, 3 * HEADS, LANES).transpose(1, 0, 2)
    conv_s = conv_sc[0].reshape(3, HEADS, LANES).transpose(1, 0, 2)
    row = lambda v: v[0].reshape(1, -1)
    proj_args = (row(mix_pre_norm), w_a, w_sc, conv_w, conv_s, lane_pad(a_log[0]), lane_pad(dt_bias[0]))

    x2d = x.reshape(batch * seq, D_MODEL)
    no_halo = jnp.zeros((N_HALO, SUBLANES, LANES), F32)
    qkv_m, _, _, gates_m, tail_m = _inproj(meta_tokens, no_halo, *proj_args, N_META, N_META)
    state0 = _gdn_meta(qkv_m, gates_m)
    qkv, z, ysc, gates, _ = _inproj(x2d, tail_m, *proj_args, seq, min(INPROJ_ROWS, seq))
    o, (w_out16, w_gate16, w_up16, w_down16) = _gdn(
        qkv, gates, state0, (w_out[0], w_gate[0], w_up[0], w_down[0]), batch, seq, min(GDN_ROWS, seq))
    out = _post(x2d, o, z, ysc, row(gdn_norm), w_out16, row(mix_post_norm), row(ffn_pre_norm),
                w_gate16, w_up16, w_down16, row(ffn_post_norm), min(POST_ROWS, seq))
    return out.reshape(batch, seq, D_MODEL)
```

```python
import functools

import jax
import jax.numpy as jnp
from jax import lax
from jax.experimental import pallas as pl
from jax.experimental.pallas import tpu as pltpu

F32 = jnp.float32
BF16 = jnp.bfloat16

D_MODEL = 1024
N_META = 16
HEADS = 4
HEAD_DIM = 128
GDN_WIDTH = HEADS * HEAD_DIM
SC_WIDTH = D_MODEL - GDN_WIDTH
CHUNK = 64
D_FF = 2816
EPS = 1e-6
LANES = 128
SUBLANES = 8
GROUP = 4 * LANES
N_GROUPS = 7
N_HALO = 4 * HEADS
INV_BLK = 16
FF_TILE = 768
INPROJ_ROWS = 1024
GDN_ROWS = 512
POST_ROWS = 512
INPROJ_SPLIT = 8
POST_SPLIT = 2
GDN_STREAMS = 4
FACTOR_STREAMS = 2
VMEM_LIMIT = 56 * 1024 * 1024


def _sigmoid(x):
    return 1.0 / (1.0 + jnp.exp(-x))


def _silu(x):
    return x * _sigmoid(x)


def _softplus(x):
    return jnp.maximum(x, 0.0) + jnp.log1p(jnp.exp(-jnp.abs(x)))


def _rms_norm(x, gain):
    return x * lax.rsqrt(jnp.mean(x * x, axis=-1, keepdims=True) + EPS) * gain


def _l2_normalize(x):
    return x * lax.rsqrt(jnp.sum(x * x, axis=-1, keepdims=True) + EPS)


def _dot(a, b):
    return jnp.dot(a, b, preferred_element_type=F32)


def _dot_nt(a, b):
    return lax.dot_general(a, b, (((1,), (1,)), ((), ())), preferred_element_type=F32)


def _dot_tn(a, b):
    return lax.dot_general(a, b, (((0,), (0,)), ((), ())), preferred_element_type=F32)


def _split_bf16(x):
    hi = x.astype(BF16)
    return hi, (x - hi.astype(F32)).astype(BF16)


def _causal_conv(ext_ref, x, w):
    rows, taps = x.shape[0], w.shape[0]
    ext_ref[pl.ds(SUBLANES, rows), :] = x
    acc = x * w[taps - 1:taps]
    for i in range(taps - 1):
        acc = acc + ext_ref[pl.ds(SUBLANES - (taps - 1) + i, rows), :] * w[i:i + 1]
    ext_ref[pl.ds(0, SUBLANES), :] = x[rows - SUBLANES:rows]
    return acc


def _inproj_kernel(x_ref, g_ref, wa_ref, wsc_ref, cw_ref, csc_ref, alog_ref, dtb_ref, head_ref,
                   qkv_ref, z_ref, ysc_ref, gate_ref, tail_ref, ext_ref, *, tiles_per_seq):
    @pl.when(pl.program_id(0) % tiles_per_seq == 0)
    def _sequence_start():
        ext_ref[:, pl.ds(0, SUBLANES), :] = head_ref[...]

    rows_per = x_ref.shape[0] // min(INPROJ_SPLIT, x_ref.shape[0] // SUBLANES)
    parts = [pl.ds(r, rows_per) for r in range(0, x_ref.shape[0], rows_per)]
    xb = [_rms_norm(x_ref[rows, :], g_ref[...]).astype(BF16) for rows in parts]
    def group(i, j):
        w_ref, j0 = (wa_ref, 0) if j < 4 else (wsc_ref, 4)
        return _dot(xb[i], w_ref[:, GROUP * (j - j0):GROUP * (j - j0 + 1)])

    blocks = lambda r: [r[:, LANES * c:LANES * (c + 1)] for c in range(HEADS)]

    conv_block = lambda slot, x, w: _causal_conv(ext_ref.at[slot], x, w)

    for t in range(3):
        for i, rows in enumerate(parts):
            for h, x in enumerate(blocks(group(i, t))):
                y = _silu(conv_block(HEADS * t + h, x, cw_ref[HEADS * t + h]))
                if t == 0:
                    y = _l2_normalize(y) * (HEAD_DIM ** -0.5)
                elif t == 1:
                    y = _l2_normalize(y)
                qkv_ref[HEADS * t + h, rows, :] = y.astype(BF16)
    for i, rows in enumerate(parts):
        sc_x, sc_c = blocks(group(i, 4)), blocks(group(i, 6))
        conv = [conv_block(3 * HEADS + c, sc_c[c] * sc_x[c], csc_ref[c]) for c in range(HEADS)]
        sc_b = blocks(group(i, 5))
        ysc_ref[rows, :] = jnp.concatenate([sc_b[c] * conv[c] for c in range(HEADS)], axis=1).astype(BF16)
    for i, rows in enumerate(parts):
        logits = _dot(xb[i], wa_ref[:, 4 * GROUP:4 * GROUP + LANES])
        lane = lax.broadcasted_iota(jnp.int32, logits.shape, 1)
        gate_ref[rows, :] = jnp.where(lane < HEADS, _sigmoid(logits),
                                      -jnp.exp(alog_ref[...]) * _softplus(logits + dtb_ref[...]))
    for i, rows in enumerate(parts):
        z_ref[rows, :] = group(i, 3)
    tail_ref[...] = ext_ref[:, pl.ds(0, SUBLANES), :]


def _inproj(x2d, head_halo, gain, w_a, w_sc, conv_w, conv_sc, alog_row, dtb_row, seq, tm):
    rows = x2d.shape[0]
    n_tiles = rows // tm
    once = lambda shape: pl.BlockSpec(shape, lambda i: (0,) * len(shape), pipeline_mode=pl.Buffered(1))
    return pl.pallas_call(
        functools.partial(_inproj_kernel, tiles_per_seq=seq // tm),
        grid=(n_tiles,),
        in_specs=[
            pl.BlockSpec((tm, D_MODEL), lambda i: (i, 0)),
            once((1, D_MODEL)),
            once((D_MODEL, 4 * GROUP + LANES)),
            once((D_MODEL, 3 * GROUP)),
            once((3 * HEADS, 4, LANES)),
            once((HEADS, 3, LANES)),
            once((1, LANES)), once((1, LANES)),
            once((N_HALO, SUBLANES, LANES)),
        ],
        out_specs=[
            pl.BlockSpec((3 * HEADS, tm, LANES), lambda i: (0, i, 0)),
            pl.BlockSpec((tm, GDN_WIDTH), lambda i: (i, 0)),
            pl.BlockSpec((tm, SC_WIDTH), lambda i: (i, 0)),
            pl.BlockSpec((tm, LANES), lambda i: (i, 0)),
            pl.BlockSpec((N_HALO, SUBLANES, LANES), lambda i: (0, 0, 0)),
        ],
        out_shape=[
            jax.ShapeDtypeStruct((3 * HEADS, rows, LANES), BF16),
            jax.ShapeDtypeStruct((rows, GDN_WIDTH), F32),
            jax.ShapeDtypeStruct((rows, SC_WIDTH), BF16),
            jax.ShapeDtypeStruct((rows, LANES), F32),
            jax.ShapeDtypeStruct((N_HALO, SUBLANES, LANES), F32),
        ],
        scratch_shapes=[pltpu.VMEM((N_HALO, SUBLANES + tm // min(INPROJ_SPLIT, tm // SUBLANES), LANES), F32)],
        compiler_params=pltpu.CompilerParams(
            dimension_semantics=("arbitrary",), vmem_limit_bytes=VMEM_LIMIT),
        name="inproj",
    )(x2d, gain, w_a, w_sc, conv_w, conv_sc, alog_row, dtb_row, head_halo)


def _pair_diag(x0, x1):
    z = jnp.zeros_like(x0)
    return jnp.concatenate([jnp.concatenate([x0, z], axis=1), jnp.concatenate([z, x1], axis=1)], axis=0)


def _diag_block_inverse(a2s):
    nblk = CHUNK // INV_BLK
    lane = lax.broadcasted_iota(jnp.int32, (INV_BLK, LANES), 1)
    row = lax.broadcasted_iota(jnp.int32, (INV_BLK, LANES), 0)
    grp = (lane & (CHUNK - 1)) // INV_BLK
    base = (lane - (lane & (INV_BLK - 1)))[:SUBLANES]
    eye = ((lane & (INV_BLK - 1)) == row).astype(F32)
    zs, ds = [], []
    for a2 in a2s:
        z = jnp.zeros((INV_BLK, LANES), F32)
        for m in range(nblk):
            z = jnp.where(grp == m, a2[INV_BLK * m:INV_BLK * (m + 1), :], z)
        zs.append([z[SUBLANES * t:SUBLANES * (t + 1)] for t in range(INV_BLK // SUBLANES)])
        ds.append([eye[SUBLANES * t:SUBLANES * (t + 1)] for t in range(INV_BLK // SUBLANES)])
    for j in range(INV_BLK - 1):
        idx = base + j
        for z, d in zip(zs, ds):
            pivot = d[j // SUBLANES][j % SUBLANES:j % SUBLANES + 1, :]
            for t in range(INV_BLK // SUBLANES):
                if SUBLANES * (t + 1) - 1 > j:
                    d[t] = d[t] - jnp.take_along_axis(z[t], idx, axis=1) * pivot
    out = []
    for d in ds:
        dz = jnp.concatenate(d, axis=0)
        out.append(jnp.concatenate([jnp.where(grp == m, dz, 0.0) for m in range(nblk)], axis=0))
    return out


def _chunk_factors(chunks):
    row = lax.broadcasted_iota(jnp.int32, (CHUNK, LANES), 0)
    lane = lax.broadcasted_iota(jnp.int32, (CHUNK, LANES), 1)
    col = lane & (CHUNK - 1)
    left = lane < CHUNK
    pick = lambda a, b: jnp.where(left, a, b)
    incl2, strict2 = row >= col, row > col
    tri2 = incl2.astype(BF16)
    eye2 = (row == col).astype(BF16)
    eye_stack = jnp.concatenate([eye2, eye2], axis=0)
    off_diag = row // INV_BLK != col // INV_BLK

    cums = [_dot(tri2, jnp.concatenate(_split_bf16(c[3]), axis=0)) for c in chunks]
    ys = [jnp.where(lane >= HEADS, cum, c[3]) for cum, c in zip(cums, chunks)]
    rows_all = [_dot_tn(jnp.concatenate(_split_bf16(y), axis=0), eye_stack) for y in ys]

    outs, items = [], []
    for (q16, k16, v16, _), y, rows in zip(chunks, ys, rows_all):
        beta_c = [jnp.broadcast_to(y[:, h:h + 1], (CHUNK, LANES)) for h in range(HEADS)]
        gc_c = [jnp.broadcast_to(y[:, HEADS + h:HEADS + h + 1], (CHUNK, LANES)) for h in range(HEADS)]
        out = dict(u=[], w=[], eg=[], dk=[], gl=[], qk=[])
        for h in range(HEADS):
            gc_last = gc_c[h][CHUNK - 1:CHUNK, :]
            out["eg"].append(jnp.exp(gc_c[h]))
            out["dk"].append(jnp.exp(gc_last - gc_c[h]))
            out["gl"].append(jnp.exp(gc_last))
        outs.append(out)
        for h0 in range(0, HEADS, 2):
            h1 = h0 + 1
            beta_r = pick(rows[h0:h0 + 1, :], rows[h1:h1 + 1, :])
            gc_r = pick(rows[HEADS + h0:HEADS + h0 + 1, :], rows[HEADS + h1:HEADS + h1 + 1, :])
            decay = jnp.where(incl2, jnp.exp(jnp.where(incl2, pick(gc_c[h0], gc_c[h1]) - gc_r, 0.0)), 0.0)
            items.append(dict(out=out, beta_r=beta_r, w_scale=beta_r * jnp.exp(gc_r), decay=decay,
                              beta_c=pick(beta_c[h0], beta_c[h1]),
                              k_cat=jnp.concatenate([k16[h0], k16[h1]], axis=1),
                              q_cat=jnp.concatenate([q16[h0], q16[h1]], axis=1),
                              k_bd=_pair_diag(k16[h0], k16[h1]), v_bd=_pair_diag(v16[h0], v16[h1])))

    halves_diag = lambda x: jnp.concatenate([jnp.where(left, x, 0.0), jnp.where(left, 0.0, x)], axis=0)

    for it in items:
        kq = _dot_nt(jnp.concatenate([it["k_cat"], it["q_cat"]], axis=0), it["k_bd"])
        it["a2"] = jnp.where(strict2, it["beta_c"] * kq[:CHUNK] * it["decay"], 0.0)
        it["out"]["qk"].append(jnp.where(incl2, kq[CHUNK:] * it["decay"], 0.0).astype(BF16))

    for it, xd in zip(items, _diag_block_inverse([it["a2"] for it in items])):
        it["xd"] = xd
    for it in items:
        off_bd = halves_diag(jnp.where(off_diag, it["a2"], 0.0))
        it["n16"] = _dot(it["xd"].astype(BF16), off_bd.astype(BF16)).astype(BF16)
        it["t"] = it["xd"]
    for _ in range(CHUNK // INV_BLK - 1):
        for it in items:
            it["t"] = it["xd"] - _dot(it["n16"], halves_diag(it["t"]).astype(BF16))
    for it in items:
        u01 = _dot((it["t"] * it["beta_r"]).astype(BF16), it["v_bd"])
        w01 = _dot((it["t"] * it["w_scale"]).astype(BF16), it["k_bd"])
        for i in range(2):
            it["out"]["u"].append(u01[:, LANES * i:LANES * (i + 1)])
            it["out"]["w"].append(w01[:, LANES * i:LANES * (i + 1)].astype(BF16))
    return outs


def _chunk_recurrence(streams, need_out):
    work = []
    for q16, k16, f, states in streams:
        work.append(dict(q=q16, k=k16, f=f, s=states, s16=[x.astype(BF16) for x in states],
                         v_new=[], q_s=[], outs=[]))
    for t in work:
        for h in range(HEADS):
            ws = _dot(jnp.concatenate([t["f"]["w"][h], t["q"][h]], axis=0), t["s16"][h])
            t["v_new"].append(t["f"]["u"][h] - ws[:CHUNK])
            t["q_s"].append(ws[CHUNK:])
    for t in work:
        t["vn16"] = [v.astype(BF16) for v in t["v_new"]]
        t["vd16"] = [(v * d).astype(BF16) for v, d in zip(t["v_new"], t["f"]["dk"])]
    for t in work:
        t["kv"] = [_dot_tn(t["k"][h], t["vd16"][h]) for h in range(HEADS)]
    if need_out:
        for t in work:
            for h0 in range(0, HEADS, 2):
                intra = _dot(t["f"]["qk"][h0 // 2], _pair_diag(t["vn16"][h0], t["vn16"][h0 + 1]))
                for i in range(2):
                    t["outs"].append(t["f"]["eg"][h0 + i] * t["q_s"][h0 + i]
                                     + intra[:, LANES * i:LANES * (i + 1)])
    return [(t["outs"], [t["s"][h] * t["f"]["gl"][h] + t["kv"][h] for h in range(HEADS)]) for t in work]


_FACTOR_KEYS = ("u", "w", "eg", "dk", "gl")


def _gdn_kernel(*refs, n_chunks, n_cast):
    q_ref, k_ref, v_ref, gate_ref, s0_ref = refs[:5]
    cast_in, o_ref, cast_out = refs[5:5 + n_cast], refs[5 + n_cast], refs[6 + n_cast:6 + 2 * n_cast]
    s_ref, u_ref, w_ref, eg_ref, dk_ref, gl_ref, qk_ref = refs[6 + 2 * n_cast:]
    f_refs = dict(u=u_ref, w=w_ref, eg=eg_ref, dk=dk_ref, gl=gl_ref)
    for src, dst in zip(cast_in, cast_out):
        dst[...] = src[...].astype(BF16)

    @pl.when(pl.program_id(1) == 0)
    def _sequence_start():
        for b in range(GDN_STREAMS):
            s_ref[b] = s0_ref[...]

    def rows_of(ref, b, c):
        r0 = pl.multiple_of(c * CHUNK, CHUNK)
        return [ref[h, b, pl.ds(r0, CHUNK), :] for h in range(HEADS)]

    def factors(i, carry):
        bcs = [(i * FACTOR_STREAMS + j, c) for j in range(FACTOR_STREAMS) for c in range(n_chunks)]
        fs = _chunk_factors([(rows_of(q_ref, b, c), rows_of(k_ref, b, c), rows_of(v_ref, b, c),
                              gate_ref[b, pl.ds(c * CHUNK, CHUNK), :]) for b, c in bcs])
        for (b, c), f in zip(bcs, fs):
            slot = b * n_chunks + c
            for h in range(HEADS):
                for key in _FACTOR_KEYS:
                    val = f[key][h]
                    f_refs[key][slot, h] = jnp.broadcast_to(val, f_refs[key].shape[2:]) if key == "gl" else val
            for p in range(HEADS // 2):
                qk_ref[slot, p] = f["qk"][p]
        return carry

    def recurrence(c, carry):
        r0 = pl.multiple_of(c * CHUNK, CHUNK)
        streams = []
        for b in range(GDN_STREAMS):
            slot = b * n_chunks + c
            f = {key: [f_refs[key][slot, h] for h in range(HEADS)] for key in _FACTOR_KEYS}
            f["gl"] = [g[:1] for g in f["gl"]]
            f["qk"] = [qk_ref[slot, p] for p in range(HEADS // 2)]
            streams.append((rows_of(q_ref, b, c), rows_of(k_ref, b, c), f,
                            [s_ref[b, h] for h in range(HEADS)]))
        for b, (outs, st) in enumerate(_chunk_recurrence(streams, True)):
            for h in range(HEADS):
                o_ref[b, pl.ds(r0, CHUNK), LANES * h:LANES * (h + 1)] = outs[h]
                s_ref[b, h] = st[h]
        return carry

    lax.fori_loop(0, GDN_STREAMS // FACTOR_STREAMS, factors, 0)
    lax.fori_loop(0, n_chunks, recurrence, 0)


def _gdn(qkv, gates, state0, cast_weights, batch, seq, ts):
    ns = seq // ts
    nc = ts // CHUNK
    nb = GDN_STREAMS
    n_steps = (batch // nb) * ns
    blk = lambda t: pl.BlockSpec((HEADS, nb, ts, LANES), lambda b, s: (t, b, s, 0))
    per_head = lambda rows, dtype: pltpu.VMEM((nb * nc, HEADS, rows, LANES), dtype)
    slab_specs = []
    for w in cast_weights:
        slab = w.shape[0] // n_steps
        assert slab * n_steps == w.shape[0] and slab % (2 * SUBLANES) == 0, (w.shape, n_steps)
        slab_specs.append(pl.BlockSpec((slab, w.shape[1]), lambda b, s: (b * ns + s, 0)))
    o, *cast = pl.pallas_call(
        functools.partial(_gdn_kernel, n_chunks=nc, n_cast=len(cast_weights)),
        grid=(batch // nb, ns),
        in_specs=[
            blk(0), blk(1), blk(2),
            pl.BlockSpec((nb, ts, LANES), lambda b, s: (b, s, 0)),
            pl.BlockSpec((HEADS, HEAD_DIM, HEAD_DIM), lambda b, s: (0, 0, 0)),
        ] + slab_specs,
        out_specs=[pl.BlockSpec((nb, ts, GDN_WIDTH), lambda b, s: (b, s, 0))] + slab_specs,
        out_shape=[jax.ShapeDtypeStruct((batch, seq, GDN_WIDTH), F32)]
        + [jax.ShapeDtypeStruct(w.shape, BF16) for w in cast_weights],
        scratch_shapes=[
            pltpu.VMEM((nb, HEADS, HEAD_DIM, HEAD_DIM), F32),
            per_head(CHUNK, F32), per_head(CHUNK, BF16), per_head(CHUNK, F32), per_head(CHUNK, F32),
            per_head(SUBLANES, F32),
            pltpu.VMEM((nb * nc, HEADS // 2, CHUNK, LANES), BF16),
        ],
        compiler_params=pltpu.CompilerParams(
            dimension_semantics=("parallel", "arbitrary"), vmem_limit_bytes=VMEM_LIMIT),
        name="gdn",
    )(*([qkv.reshape(3 * HEADS, batch, seq, LANES)] * 3), gates.reshape(batch, seq, LANES), state0,
      *cast_weights)
    return o.reshape(batch * seq, GDN_WIDTH), cast


def _gdn_meta_kernel(k_ref, v_ref, gate_ref, s_ref):
    pad = CHUNK - N_META
    front = lambda x: jnp.concatenate([jnp.zeros((pad, LANES), x.dtype), x], axis=0)
    k16 = [front(k_ref[h]) for h in range(HEADS)]
    v16 = [front(v_ref[h]) for h in range(HEADS)]
    f, = _chunk_factors([(k16, k16, v16, front(gate_ref[...]))])
    zero = [jnp.zeros((HEAD_DIM, HEAD_DIM), F32)] * HEADS
    (_, st), = _chunk_recurrence([(k16, k16, f, zero)], False)
    for h in range(HEADS):
        s_ref[h] = st[h]


def _gdn_meta(qkv_meta, gates_meta):
    blk = lambda t: pl.BlockSpec((HEADS, N_META, LANES), lambda i: (t, 0, 0))
    return pl.pallas_call(
        _gdn_meta_kernel,
        grid=(1,),
        in_specs=[blk(1), blk(2), pl.BlockSpec((N_META, LANES), lambda i: (0, 0))],
        out_specs=pl.BlockSpec((HEADS, HEAD_DIM, HEAD_DIM), lambda i: (0, 0, 0)),
        out_shape=jax.ShapeDtypeStruct((HEADS, HEAD_DIM, HEAD_DIM), F32),
        name="gdn_meta",
    )(qkv_meta, qkv_meta, gates_meta)


def _post_kernel(x_ref, o_ref, z_ref, ysc_ref, gn_ref, wo_ref, g_post_ref, g_pre_ref, wg_ref, wu_ref,
                 wd_ref, g_fpost_ref, out_ref):
    rows_per = x_ref.shape[0] // POST_SPLIT
    parts = [pl.ds(i * rows_per, rows_per) for i in range(POST_SPLIT)]
    gn = gn_ref[...]
    mixes = []
    for rows in parts:
        heads = []
        for h in range(HEADS):
            sl = slice(HEAD_DIM * h, HEAD_DIM * (h + 1))
            heads.append((_rms_norm(o_ref[rows, sl], gn) * _silu(z_ref[rows, sl])).astype(BF16))
        mixes.append(_dot(jnp.concatenate(heads + [ysc_ref[rows, :]], axis=1), wo_ref[...]))
    h1 = [x_ref[rows, :] + _rms_norm(mix, g_post_ref[...]) for rows, mix in zip(parts, mixes)]
    u = [_rms_norm(h, g_pre_ref[...]).astype(BF16) for h in h1]
    ffn = [jnp.zeros((rows_per, D_MODEL), F32)] * POST_SPLIT
    for lo in range(0, D_FF, FF_TILE):
        sl = slice(lo, min(lo + FF_TILE, D_FF))
        for i in range(POST_SPLIT):
            act = (_silu(_dot(u[i], wg_ref[:, sl])) * _dot(u[i], wu_ref[:, sl])).astype(BF16)
            ffn[i] = ffn[i] + _dot(act, wd_ref[sl, :])
    for i, rows in enumerate(parts):
        out_ref[rows, :] = h1[i] + _rms_norm(ffn[i], g_fpost_ref[...])


def _post(x2d, o, z, ysc, gnorm, w_out, g_post, g_pre, w_gate, w_up, w_down, g_fpost, tm):
    rows = x2d.shape[0]
    row_blk = lambda width: pl.BlockSpec((tm, width), lambda i: (i, 0))
    once = lambda shape: pl.BlockSpec(shape, lambda i: (0, 0), pipeline_mode=pl.Buffered(1))
    return pl.pallas_call(
        _post_kernel,
        grid=(rows // tm,),
        in_specs=[
            row_blk(D_MODEL), row_blk(GDN_WIDTH), row_blk(GDN_WIDTH), row_blk(SC_WIDTH),
            once((1, HEAD_DIM)),
            once((D_MODEL, D_MODEL)), once((1, D_MODEL)), once((1, D_MODEL)),
            once((D_MODEL, D_FF)), once((D_MODEL, D_FF)), once((D_FF, D_MODEL)), once((1, D_MODEL)),
        ],
        out_specs=row_blk(D_MODEL),
        out_shape=jax.ShapeDtypeStruct((rows, D_MODEL), F32),
        compiler_params=pltpu.CompilerParams(
            dimension_semantics=("parallel",), vmem_limit_bytes=VMEM_LIMIT),
        name="post",
    )(x2d, o, z, ysc, gnorm, w_out, g_post, g_pre, w_gate, w_up, w_down, g_fpost)


def kernel(x, meta_tokens, mix_pre_norm, mix_post_norm, ffn_pre_norm, ffn_post_norm, w_in, conv_qkv,
           a_log, dt_bias, gdn_norm, conv_sc, w_out, w_gate, w_up, w_down):
    batch, seq, _ = x.shape
    assert mix_pre_norm.shape[0] == 1, "single-layer block"
    assert batch % GDN_STREAMS == 0
    n_logit = 2 * HEADS
    cut = 4 * GDN_WIDTH
    w_in0 = w_in[0]
    w_a = w_in0[:, :cut + LANES].astype(BF16)
    w_sc = w_in0[:, cut + n_logit:].astype(BF16)
    lane_pad = lambda v: jnp.pad(v.reshape(1, HEADS), ((0, 0), (HEADS, LANES - 2 * HEADS)))
    conv_w = conv_qkv[0].reshape(4, 3 * HEADS, LANES).transpose(1, 0, 2)
    conv_s = conv_sc[0].reshape(3, HEADS, LANES).transpose(1, 0, 2)
    row = lambda v: v[0].reshape(1, -1)
    proj_args = (row(mix_pre_norm), w_a, w_sc, conv_w, conv_s, lane_pad(a_log[0]), lane_pad(dt_bias[0]))

    x2d = x.reshape(batch * seq, D_MODEL)
    no_halo = jnp.zeros((N_HALO, SUBLANES, LANES), F32)
    qkv_m, _, _, gates_m, tail_m = _inproj(meta_tokens, no_halo, *proj_args, N_META, N_META)
    state0 = _gdn_meta(qkv_m, gates_m)
    qkv, z, ysc, gates, _ = _inproj(x2d, tail_m, *proj_args, seq, min(INPROJ_ROWS, seq))
    o, (w_out16, w_gate16, w_up16, w_down16) = _gdn(
        qkv, gates, state0, (w_out[0], w_gate[0], w_up[0], w_down[0]), batch, seq, min(GDN_ROWS, seq))
    out = _post(x2d, o, z, ysc, row(gdn_norm), w_out16, row(mix_post_norm), row(ffn_pre_norm),
                w_gate16, w_up16, w_down16, row(ffn_post_norm), min(POST_ROWS, seq))
    return out.reshape(batch, seq, D_MODEL)
```

```python
import functools

import jax
import jax.numpy as jnp
from jax import lax
from jax.experimental import pallas as pl
from jax.experimental.pallas import tpu as pltpu

F32 = jnp.float32
BF16 = jnp.bfloat16

D_MODEL = 1024
N_META = 16
HEADS = 4
HEAD_DIM = 128
GDN_WIDTH = HEADS * HEAD_DIM
SC_WIDTH = D_MODEL - GDN_WIDTH
CHUNK = 64
D_FF = 2816
EPS = 1e-6
LANES = 128
SUBLANES = 8
GROUP = 4 * LANES
N_GROUPS = 7
N_HALO = 4 * HEADS
INV_BLK = 16
FF_TILE = 768
INPROJ_ROWS = 1024
GDN_ROWS = 512
POST_ROWS = 512
INPROJ_SPLIT = 8
POST_SPLIT = 2
GDN_STREAMS = 4
FACTOR_STREAMS = 2
VMEM_LIMIT = 56 * 1024 * 1024


def _sigmoid(x):
    return 1.0 / (1.0 + jnp.exp(-x))


def _silu(x):
    return x * _sigmoid(x)


def _softplus(x):
    return jnp.maximum(x, 0.0) + jnp.log1p(jnp.exp(-jnp.abs(x)))


def _rms_norm(x, gain):
    return x * lax.rsqrt(jnp.mean(x * x, axis=-1, keepdims=True) + EPS) * gain


def _l2_normalize(x):
    return x * lax.rsqrt(jnp.sum(x * x, axis=-1, keepdims=True) + EPS)


def _dot(a, b):
    return jnp.dot(a, b, preferred_element_type=F32)


def _dot_nt(a, b):
    return lax.dot_general(a, b, (((1,), (1,)), ((), ())), preferred_element_type=F32)


def _dot_tn(a, b):
    return lax.dot_general(a, b, (((0,), (0,)), ((), ())), preferred_element_type=F32)


def _split_bf16(x):
    hi = x.astype(BF16)
    return hi, (x - hi.astype(F32)).astype(BF16)


def _causal_conv(ext_ref, x, w):
    rows, taps = x.shape[0], w.shape[0]
    ext_ref[pl.ds(SUBLANES, rows), :] = x
    acc = x * w[taps - 1:taps]
    for i in range(taps - 1):
        acc = acc + ext_ref[pl.ds(SUBLANES - (taps - 1) + i, rows), :] * w[i:i + 1]
    ext_ref[pl.ds(0, SUBLANES), :] = x[rows - SUBLANES:rows]
    return acc


def _inproj_kernel(x_ref, g_ref, wa_ref, wsc_ref, cw_ref, csc_ref, alog_ref, dtb_ref, head_ref,
                   qkv_ref, z_ref, ysc_ref, gate_ref, tail_ref, ext_ref, *, tiles_per_seq):
    @pl.when(pl.program_id(0) % tiles_per_seq == 0)
    def _sequence_start():
        ext_ref[:, pl.ds(0, SUBLANES), :] = head_ref[...]

    rows_per = x_ref.shape[0] // min(INPROJ_SPLIT, x_ref.shape[0] // SUBLANES)
    parts = [pl.ds(r, rows_per) for r in range(0, x_ref.shape[0], rows_per)]
    xb = [_rms_norm(x_ref[rows, :], g_ref[...]).astype(BF16) for rows in parts]
    def group(i, j):
        w_ref, j0 = (wa_ref, 0) if j < 4 else (wsc_ref, 4)
        return _dot(xb[i], w_ref[:, GROUP * (j - j0):GROUP * (j - j0 + 1)])

    blocks = lambda r: [r[:, LANES * c:LANES * (c + 1)] for c in range(HEADS)]

    conv_block = lambda slot, x, w: _causal_conv(ext_ref.at[slot], x, w)

    for t in range(3):
        for i, rows in enumerate(parts):
            for h, x in enumerate(blocks(group(i, t))):
                y = _silu(conv_block(HEADS * t + h, x, cw_ref[HEADS * t + h]))
                if t == 0:
                    y = _l2_normalize(y) * (HEAD_DIM ** -0.5)
                elif t == 1:
                    y = _l2_normalize(y)
                qkv_ref[HEADS * t + h, rows, :] = y.astype(BF16)
    for i, rows in enumerate(parts):
        sc_x, sc_c = blocks(group(i, 4)), blocks(group(i, 6))
        conv = [conv_block(3 * HEADS + c, sc_c[c] * sc_x[c], csc_ref[c]) for c in range(HEADS)]
        sc_b = blocks(group(i, 5))
        ysc_ref[rows, :] = jnp.concatenate([sc_b[c] * conv[c] for c in range(HEADS)], axis=1).astype(BF16)
    for i, rows in enumerate(parts):
        logits = _dot(xb[i], wa_ref[:, 4 * GROUP:4 * GROUP + LANES])
        lane = lax.broadcasted_iota(jnp.int32, logits.shape, 1)
        gate_ref[rows, :] = jnp.where(lane < HEADS, _sigmoid(logits),
                                      -jnp.exp(alog_ref[...]) * _softplus(logits + dtb_ref[...]))
    for i, rows in enumerate(parts):
        z_ref[rows, :] = group(i, 3)
    tail_ref[...] = ext_ref[:, pl.ds(0, SUBLANES), :]


def _inproj(x2d, head_halo, gain, w_a, w_sc, conv_w, conv_sc, alog_row, dtb_row, seq, tm):
    rows = x2d.shape[0]
    n_tiles = rows // tm
    once = lambda shape: pl.BlockSpec(shape, lambda i: (0,) * len(shape), pipeline_mode=pl.Buffered(1))
    return pl.pallas_call(
        functools.partial(_inproj_kernel, tiles_per_seq=seq // tm),
        grid=(n_tiles,),
        in_specs=[
            pl.BlockSpec((tm, D_MODEL), lambda i: (i, 0)),
            once((1, D_MODEL)),
            once((D_MODEL, 4 * GROUP + LANES)),
            once((D_MODEL, 3 * GROUP)),
            once((3 * HEADS, 4, LANES)),
            once((HEADS, 3, LANES)),
            once((1, LANES)), once((1, LANES)),
            once((N_HALO, SUBLANES, LANES)),
        ],
        out_specs=[
            pl.BlockSpec((3 * HEADS, tm, LANES), lambda i: (0, i, 0)),
            pl.BlockSpec((tm, GDN_WIDTH), lambda i: (i, 0)),
            pl.BlockSpec((tm, SC_WIDTH), lambda i: (i, 0)),
            pl.BlockSpec((tm, LANES), lambda i: (i, 0)),
            pl.BlockSpec((N_HALO, SUBLANES, LANES), lambda i: (0, 0, 0)),
        ],
        out_shape=[
            jax.ShapeDtypeStruct((3 * HEADS, rows, LANES), BF16),
            jax.ShapeDtypeStruct((rows, GDN_WIDTH), F32),
            jax.ShapeDtypeStruct((rows, SC_WIDTH), BF16),
            jax.ShapeDtypeStruct((rows, LANES), F32),
            jax.ShapeDtypeStruct((N_HALO, SUBLANES, LANES), F32),
        ],
        scratch_shapes=[pltpu.VMEM((N_HALO, SUBLANES + tm // min(INPROJ_SPLIT, tm // SUBLANES), LANES), F32)],
        compiler_params=pltpu.CompilerParams(
            dimension_semantics=("arbitrary",), vmem_limit_bytes=VMEM_LIMIT),
        name="inproj",
    )(x2d, gain, w_a, w_sc, conv_w, conv_sc, alog_row, dtb_row, head_halo)


def _pair_diag(x0, x1):
    z = jnp.zeros_like(x0)
    return jnp.concatenate([jnp.concatenate([x0, z], axis=1), jnp.concatenate([z, x1], axis=1)], axis=0)


def _diag_block_inverse(a2s):
    nblk = CHUNK // INV_BLK
    lane = lax.broadcasted_iota(jnp.int32, (INV_BLK, LANES), 1)
    row = lax.broadcasted_iota(jnp.int32, (INV_BLK, LANES), 0)
    grp = (lane & (CHUNK - 1)) // INV_BLK
    base = (lane - (lane & (INV_BLK - 1)))[:SUBLANES]
    eye = ((lane & (INV_BLK - 1)) == row).astype(F32)
    zs, ds = [], []
    for a2 in a2s:
        z = jnp.zeros((INV_BLK, LANES), F32)
        for m in range(nblk):
            z = jnp.where(grp == m, a2[INV_BLK * m:INV_BLK * (m + 1), :], z)
        zs.append([z[SUBLANES * t:SUBLANES * (t + 1)] for t in range(INV_BLK // SUBLANES)])
        ds.append([eye[SUBLANES * t:SUBLANES * (t + 1)] for t in range(INV_BLK // SUBLANES)])
    for j in range(INV_BLK - 1):
        idx = base + j
        for z, d in zip(zs, ds):
            pivot = d[j // SUBLANES][j % SUBLANES:j % SUBLANES + 1, :]
            for t in range(INV_BLK // SUBLANES):
                if SUBLANES * (t + 1) - 1 > j:
                    d[t] = d[t] - jnp.take_along_axis(z[t], idx, axis=1) * pivot
    out = []
    for d in ds:
        dz = jnp.concatenate(d, axis=0)
        out.append(jnp.concatenate([jnp.where(grp == m, dz, 0.0) for m in range(nblk)], axis=0))
    return out


def _chunk_factors(chunks):
    row = lax.broadcasted_iota(jnp.int32, (CHUNK, LANES), 0)
    lane = lax.broadcasted_iota(jnp.int32, (CHUNK, LANES), 1)
    col = lane & (CHUNK - 1)
    left = lane < CHUNK
    pick = lambda a, b: jnp.where(left, a, b)
    incl2, strict2 = row >= col, row > col
    tri2 = incl2.astype(BF16)
    eye2 = (row == col).astype(BF16)
    eye_stack = jnp.concatenate([eye2, eye2], axis=0)
    off_diag = row // INV_BLK != col // INV_BLK

    cums = [_dot(tri2, jnp.concatenate(_split_bf16(c[3]), axis=0)) for c in chunks]
    ys = [jnp.where(lane >= HEADS, cum, c[3]) for cum, c in zip(cums, chunks)]
    rows_all = [_dot_tn(jnp.concatenate(_split_bf16(y), axis=0), eye_stack) for y in ys]

    outs, items = [], []
    for (q16, k16, v16, _), y, rows in zip(chunks, ys, rows_all):
        beta_c = [jnp.broadcast_to(y[:, h:h + 1], (CHUNK, LANES)) for h in range(HEADS)]
        gc_c = [jnp.broadcast_to(y[:, HEADS + h:HEADS + h + 1], (CHUNK, LANES)) for h in range(HEADS)]
        out = dict(u=[], w=[], eg=[], dk=[], gl=[], qk=[])
        for h in range(HEADS):
            gc_last = gc_c[h][CHUNK - 1:CHUNK, :]
            out["eg"].append(jnp.exp(gc_c[h]))
            out["dk"].append(jnp.exp(gc_last - gc_c[h]))
            out["gl"].append(jnp.exp(gc_last))
        outs.append(out)
        for h0 in range(0, HEADS, 2):
            h1 = h0 + 1
            beta_r = pick(rows[h0:h0 + 1, :], rows[h1:h1 + 1, :])
            gc_r = pick(rows[HEADS + h0:HEADS + h0 + 1, :], rows[HEADS + h1:HEADS + h1 + 1, :])
            decay = jnp.where(incl2, jnp.exp(jnp.where(incl2, pick(gc_c[h0], gc_c[h1]) - gc_r, 0.0)), 0.0)
            items.append(dict(out=out, beta_r=beta_r, w_scale=beta_r * jnp.exp(gc_r), decay=decay,
                              beta_c=pick(beta_c[h0], beta_c[h1]),
                              k_cat=jnp.concatenate([k16[h0], k16[h1]], axis=1),
                              q_cat=jnp.concatenate([q16[h0], q16[h1]], axis=1),
                              k_bd=_pair_diag(k16[h0], k16[h1]), v_bd=_pair_diag(v16[h0], v16[h1])))

    halves_diag = lambda x: jnp.concatenate([jnp.where(left, x, 0.0), jnp.where(left, 0.0, x)], axis=0)

    for it in items:
        kq = _dot_nt(jnp.concatenate([it["k_cat"], it["q_cat"]], axis=0), it["k_bd"])
        it["a2"] = jnp.where(strict2, it["beta_c"] * kq[:CHUNK] * it["decay"], 0.0)
        it["out"]["qk"].append(jnp.where(incl2, kq[CHUNK:] * it["decay"], 0.0).astype(BF16))

    for it, xd in zip(items, _diag_block_inverse([it["a2"] for it in items])):
        it["xd"] = xd
    for it in items:
        off_bd = halves_diag(jnp.where(off_diag, it["a2"], 0.0))
        it["n16"] = _dot(it["xd"].astype(BF16), off_bd.astype(BF16)).astype(BF16)
        it["t"] = it["xd"]
    for _ in range(CHUNK // INV_BLK - 1):
        for it in items:
            it["t"] = it["xd"] - _dot(it["n16"], halves_diag(it["t"]).astype(BF16))
    for it in items:
        u01 = _dot((it["t"] * it["beta_r"]).astype(BF16), it["v_bd"])
        w01 = _dot((it["t"] * it["w_scale"]).astype(BF16), it["k_bd"])
        for i in range(2):
            it["out"]["u"].append(u01[:, LANES * i:LANES * (i + 1)])
            it["out"]["w"].append(w01[:, LANES * i:LANES * (i + 1)].astype(BF16))
    return outs


def _chunk_recurrence(streams, need_out):
    work = []
    for q16, k16, f, states in streams:
        work.append(dict(q=q16, k=k16, f=f, s=states, s16=[x.astype(BF16) for x in states],
                         v_new=[], q_s=[], outs=[]))
    for t in work:
        for h in range(HEADS):
            ws = _dot(jnp.concatenate([t["f"]["w"][h], t["q"][h]], axis=0), t["s16"][h])
            t["v_new"].append(t["f"]["u"][h] - ws[:CHUNK])
            t["q_s"].append(ws[CHUNK:])
    for t in work:
        t["vn16"] = [v.astype(BF16) for v in t["v_new"]]
        t["vd16"] = [(v * d).astype(BF16) for v, d in zip(t["v_new"], t["f"]["dk"])]
    for t in work:
        t["kv"] = [_dot_tn(t["k"][h], t["vd16"][h]) for h in range(HEADS)]
    if need_out:
        for t in work:
            for h0 in range(0, HEADS, 2):
                intra = _dot(t["f"]["qk"][h0 // 2], _pair_diag(t["vn16"][h0], t["vn16"][h0 + 1]))
                for i in range(2):
                    t["outs"].append(t["f"]["eg"][h0 + i] * t["q_s"][h0 + i]
                                     + intra[:, LANES * i:LANES * (i + 1)])
    return [(t["outs"], [t["s"][h] * t["f"]["gl"][h] + t["kv"][h] for h in range(HEADS)]) for t in work]


_FACTOR_KEYS = ("u", "w", "eg", "dk", "gl")


def _gdn_kernel(*refs, n_chunks, n_cast):
    q_ref, k_ref, v_ref, gate_ref, s0_ref = refs[:5]
    cast_in, o_ref, cast_out = refs[5:5 + n_cast], refs[5 + n_cast], refs[6 + n_cast:6 + 2 * n_cast]
    s_ref, u_ref, w_ref, eg_ref, dk_ref, gl_ref, qk_ref = refs[6 + 2 * n_cast:]
    f_refs = dict(u=u_ref, w=w_ref, eg=eg_ref, dk=dk_ref, gl=gl_ref)
    for src, dst in zip(cast_in, cast_out):
        dst[...] = src[...].astype(BF16)

    @pl.when(pl.program_id(1) == 0)
    def _sequence_start():
        for b in range(GDN_STREAMS):
            s_ref[b] = s0_ref[...]

    def rows_of(ref, b, c):
        r0 = pl.multiple_of(c * CHUNK, CHUNK)
        return [ref[h, b, pl.ds(r0, CHUNK), :] for h in range(HEADS)]

    def factors(i, carry):
        bcs = [(i * FACTOR_STREAMS + j, c) for j in range(FACTOR_STREAMS) for c in range(n_chunks)]
        fs = _chunk_factors([(rows_of(q_ref, b, c), rows_of(k_ref, b, c), rows_of(v_ref, b, c),
                              gate_ref[b, pl.ds(c * CHUNK, CHUNK), :]) for b, c in bcs])
        for (b, c), f in zip(bcs, fs):
            slot = b * n_chunks + c
            for h in range(HEADS):
                for key in _FACTOR_KEYS:
                    val = f[key][h]
                    f_refs[key][slot, h] = jnp.broadcast_to(val, f_refs[key].shape[2:]) if key == "gl" else val
            for p in range(HEADS // 2):
                qk_ref[slot, p] = f["qk"][p]
        return carry

    def recurrence(c, carry):
        r0 = pl.multiple_of(c * CHUNK, CHUNK)
        streams = []
        for b in range(GDN_STREAMS):
            slot = b * n_chunks + c
            f = {key: [f_refs[key][slot, h] for h in range(HEADS)] for key in _FACTOR_KEYS}
            f["gl"] = [g[:1] for g in f["gl"]]
            f["qk"] = [qk_ref[slot, p] for p in range(HEADS // 2)]
            streams.append((rows_of(q_ref, b, c), rows_of(k_ref, b, c), f,
                            [s_ref[b, h] for h in range(HEADS)]))
        for b, (outs, st) in enumerate(_chunk_recurrence(streams, True)):
            for h in range(HEADS):
                o_ref[b, pl.ds(r0, CHUNK), LANES * h:LANES * (h + 1)] = outs[h]
                s_ref[b, h] = st[h]
        return carry

    lax.fori_loop(0, GDN_STREAMS // FACTOR_STREAMS, factors, 0)
    lax.fori_loop(0, n_chunks, recurrence, 0)


def _gdn(qkv, gates, state0, cast_weights, batch, seq, ts):
    ns = seq // ts
    nc = ts // CHUNK
    nb = GDN_STREAMS
    n_steps = (batch // nb) * ns
    blk = lambda t: pl.BlockSpec((HEADS, nb, ts, LANES), lambda b, s: (t, b, s, 0))
    per_head = lambda rows, dtype: pltpu.VMEM((nb * nc, HEADS, rows, LANES), dtype)
    slab_specs = []
    for w in cast_weights:
        slab = w.shape[0] // n_steps
        assert slab * n_steps == w.shape[0] and slab % (2 * SUBLANES) == 0, (w.shape, n_steps)
        slab_specs.append(pl.BlockSpec((slab, w.shape[1]), lambda b, s: (b * ns + s, 0)))
    o, *cast = pl.pallas_call(
        functools.partial(_gdn_kernel, n_chunks=nc, n_cast=len(cast_weights)),
        grid=(batch // nb, ns),
        in_specs=[
            blk(0), blk(1), blk(2),
            pl.BlockSpec((nb, ts, LANES), lambda b, s: (b, s, 0)),
            pl.BlockSpec((HEADS, HEAD_DIM, HEAD_DIM), lambda b, s: (0, 0, 0)),
        ] + slab_specs,
        out_specs=[pl.BlockSpec((nb, ts, GDN_WIDTH), lambda b, s: (b, s, 0))] + slab_specs,
        out_shape=[jax.ShapeDtypeStruct((batch, seq, GDN_WIDTH), F32)]
        + [jax.ShapeDtypeStruct(w.shape, BF16) for w in cast_weights],
        scratch_shapes=[
            pltpu.VMEM((nb, HEADS, HEAD_DIM, HEAD_DIM), F32),
            per_head(CHUNK, F32), per_head(CHUNK, BF16), per_head(CHUNK, F32), per_head(CHUNK, F32),
            per_head(SUBLANES, F32),
            pltpu.VMEM((nb * nc, HEADS // 2, CHUNK, LANES), BF16),
        ],
        compiler_params=pltpu.CompilerParams(
            dimension_semantics=("parallel", "arbitrary"), vmem_limit_bytes=VMEM_LIMIT),
        name="gdn",
    )(*([qkv.reshape(3 * HEADS, batch, seq, LANES)] * 3), gates.reshape(batch, seq, LANES), state0,
      *cast_weights)
    return o.reshape(batch * seq, GDN_WIDTH), cast


def _gdn_meta_kernel(k_ref, v_ref, gate_ref, s_ref):
    pad = CHUNK - N_META
    front = lambda x: jnp.concatenate([jnp.zeros((pad, LANES), x.dtype), x], axis=0)
    k16 = [front(k_ref[h]) for h in range(HEADS)]
    v16 = [front(v_ref[h]) for h in range(HEADS)]
    f, = _chunk_factors([(k16, k16, v16, front(gate_ref[...]))])
    zero = [jnp.zeros((HEAD_DIM, HEAD_DIM), F32)] * HEADS
    (_, st), = _chunk_recurrence([(k16, k16, f, zero)], False)
    for h in range(HEADS):
        s_ref[h] = st[h]


def _gdn_meta(qkv_meta, gates_meta):
    blk = lambda t: pl.BlockSpec((HEADS, N_META, LANES), lambda i: (t, 0, 0))
    return pl.pallas_call(
        _gdn_meta_kernel,
        grid=(1,),
        in_specs=[blk(1), blk(2), pl.BlockSpec((N_META, LANES), lambda i: (0, 0))],
        out_specs=pl.BlockSpec((HEADS, HEAD_DIM, HEAD_DIM), lambda i: (0, 0, 0)),
        out_shape=jax.ShapeDtypeStruct((HEADS, HEAD_DIM, HEAD_DIM), F32),
        name="gdn_meta",
    )(qkv_meta, qkv_meta, gates_meta)


def _post_kernel(x_ref, o_ref, z_ref, ysc_ref, gn_ref, wo_ref, g_post_ref, g_pre_ref, wg_ref, wu_ref,
                 wd_ref, g_fpost_ref, out_ref):
    rows_per = x_ref.shape[0] // POST_SPLIT
    parts = [pl.ds(i * rows_per, rows_per) for i in range(POST_SPLIT)]
    gn = gn_ref[...]
    mixes = []
    for rows in parts:
        heads = []
        for h in range(HEADS):
            sl = slice(HEAD_DIM * h, HEAD_DIM * (h + 1))
            heads.append((_rms_norm(o_ref[rows, sl], gn) * _silu(z_ref[rows, sl])).astype(BF16))
        mixes.append(_dot(jnp.concatenate(heads + [ysc_ref[rows, :]], axis=1), wo_ref[...]))
    h1 = [x_ref[rows, :] + _rms_norm(mix, g_post_ref[...]) for rows, mix in zip(parts, mixes)]
    u = [_rms_norm(h, g_pre_ref[...]).astype(BF16) for h in h1]
    ffn = [jnp.zeros((rows_per, D_MODEL), F32)] * POST_SPLIT
    for lo in range(0, D_FF, FF_TILE):
        sl = slice(lo, min(lo + FF_TILE, D_FF))
        for i in range(POST_SPLIT):
            act = (_silu(_dot(u[i], wg_ref[:, sl])) * _dot(u[i], wu_ref[:, sl])).astype(BF16)
            ffn[i] = ffn[i] + _dot(act, wd_ref[sl, :])
    for i, rows in enumerate(parts):
        out_ref[rows, :] = h1[i] + _rms_norm(ffn[i], g_fpost_ref[...])


def _post(x2d, o, z, ysc, gnorm, w_out, g_post, g_pre, w_gate, w_up, w_down, g_fpost, tm):
    rows = x2d.shape[0]
    row_blk = lambda width: pl.BlockSpec((tm, width), lambda i: (i, 0))
    once = lambda shape: pl.BlockSpec(shape, lambda i: (0, 0), pipeline_mode=pl.Buffered(1))
    return pl.pallas_call(
        _post_kernel,
        grid=(rows // tm,),
        in_specs=[
            row_blk(D_MODEL), row_blk(GDN_WIDTH), row_blk(GDN_WIDTH), row_blk(SC_WIDTH),
            once((1, HEAD_DIM)),
            once((D_MODEL, D_MODEL)), once((1, D_MODEL)), once((1, D_MODEL)),
            once((D_MODEL, D_FF)), once((D_MODEL, D_FF)), once((D_FF, D_MODEL)), once((1, D_MODEL)),
        ],
        out_specs=row_blk(D_MODEL),
        out_shape=jax.ShapeDtypeStruct((rows, D_MODEL), F32),
        compiler_params=pltpu.CompilerParams(
            dimension_semantics=("parallel",), vmem_limit_bytes=VMEM_LIMIT),
        name="post",
    )(x2d, o, z, ysc, gnorm, w_out, g_post, g_pre, w_gate, w_up, w_down, g_fpost)


def kernel(x, meta_tokens, mix_pre_norm, mix_post_norm, ffn_pre_norm, ffn_post_norm, w_in, conv_qkv,
           a_log, dt_bias, gdn_norm, conv_sc, w_out, w_gate, w_up, w_down):
    batch, seq, _ = x.shape
    assert mix_pre_norm.shape[0] == 1, "single-layer block"
    assert batch % GDN_STREAMS == 0
    n_logit = 2 * HEADS
    cut = 4 * GDN_WIDTH
    w_in0 = w_in[0]
    w_a = w_in0.astype(BF16)
    w_sc = w_a[:, cut + n_logit:]
    lane_pad = lambda v: jnp.pad(v.reshape(1, HEADS), ((0, 0), (HEADS, LANES - 2 * HEADS)))
    conv_w = conv_qkv[0].reshape(4, 3 * HEADS, LANES).transpose(1, 0, 2)
    conv_s = conv_sc[0].reshape(3, HEADS, LANES).transpose(1, 0, 2)
    row = lambda v: v[0].reshape(1, -1)
    proj_args = (row(mix_pre_norm), w_a, w_sc, conv_w, conv_s, lane_pad(a_log[0]), lane_pad(dt_bias[0]))

    x2d = x.reshape(batch * seq, D_MODEL)
    no_halo = jnp.zeros((N_HALO, SUBLANES, LANES), F32)
    qkv_m, _, _, gates_m, tail_m = _inproj(meta_tokens, no_halo, *proj_args, N_META, N_META)
    state0 = _gdn_meta(qkv_m, gates_m)
    qkv, z, ysc, gates, _ = _inproj(x2d, tail_m, *proj_args, seq, min(INPROJ_ROWS, seq))
    o, (w_out16, w_gate16, w_up16, w_down16) = _gdn(
        qkv, gates, state0, (w_out[0], w_gate[0], w_up[0], w_down[0]), batch, seq, min(GDN_ROWS, seq))
    out = _post(x2d, o, z, ysc, row(gdn_norm), w_out16, row(mix_post_norm), row(ffn_pre_norm),
                w_gate16, w_up16, w_down16, row(ffn_post_norm), min(POST_ROWS, seq))
    return out.reshape(batch, seq, D_MODEL)
```

```python
import functools

import jax
import jax.numpy as jnp
from jax import lax
from jax.experimental import pallas as pl
from jax.experimental.pallas import tpu as pltpu

F32 = jnp.float32
BF16 = jnp.bfloat16

D_MODEL = 1024
N_META = 16
HEADS = 4
HEAD_DIM = 128
GDN_WIDTH = HEADS * HEAD_DIM
SC_WIDTH = D_MODEL - GDN_WIDTH
CHUNK = 64
D_FF = 2816
EPS = 1e-6
LANES = 128
SUBLANES = 8
GROUP = 4 * LANES
N_GROUPS = 7
N_HALO = 4 * HEADS
INV_BLK = 16
FF_TILE = 768
INPROJ_ROWS = 1024
GDN_ROWS = 512
POST_ROWS = 512
INPROJ_SPLIT = 8
POST_SPLIT = 2
GDN_STREAMS = 4
RECURRENCE_UNROLL = 8
FACTOR_STREAMS = 2
VMEM_LIMIT = 56 * 1024 * 1024


def _sigmoid(x):
    return 1.0 / (1.0 + jnp.exp(-x))


def _silu(x):
    return x * _sigmoid(x)


def _softplus(x):
    return jnp.maximum(x, 0.0) + jnp.log1p(jnp.exp(-jnp.abs(x)))


def _rms_norm(x, gain):
    return x * lax.rsqrt(jnp.mean(x * x, axis=-1, keepdims=True) + EPS) * gain


def _l2_normalize(x):
    return x * lax.rsqrt(jnp.sum(x * x, axis=-1, keepdims=True) + EPS)


def _dot(a, b):
    return jnp.dot(a, b, preferred_element_type=F32)


def _dot_nt(a, b):
    return lax.dot_general(a, b, (((1,), (1,)), ((), ())), preferred_element_type=F32)


def _dot_tn(a, b):
    return lax.dot_general(a, b, (((0,), (0,)), ((), ())), preferred_element_type=F32)


def _split_bf16(x):
    hi = x.astype(BF16)
    return hi, (x - hi.astype(F32)).astype(BF16)


def _causal_conv(ext_ref, x, w):
    rows, taps = x.shape[0], w.shape[0]
    ext_ref[pl.ds(SUBLANES, rows), :] = x
    acc = x * w[taps - 1:taps]
    for i in range(taps - 1):
        acc = acc + ext_ref[pl.ds(SUBLANES - (taps - 1) + i, rows), :] * w[i:i + 1]
    ext_ref[pl.ds(0, SUBLANES), :] = x[rows - SUBLANES:rows]
    return acc


def _inproj_kernel(x_ref, g_ref, wa_ref, wsc_ref, cw_ref, csc_ref, alog_ref, dtb_ref, head_ref,
                   qkv_ref, z_ref, ysc_ref, gate_ref, tail_ref, ext_ref, *, tiles_per_seq):
    @pl.when(pl.program_id(0) % tiles_per_seq == 0)
    def _sequence_start():
        ext_ref[:, pl.ds(0, SUBLANES), :] = head_ref[...]

    rows_per = x_ref.shape[0] // min(INPROJ_SPLIT, x_ref.shape[0] // SUBLANES)
    parts = [pl.ds(r, rows_per) for r in range(0, x_ref.shape[0], rows_per)]
    xb = [_rms_norm(x_ref[rows, :], g_ref[...]).astype(BF16) for rows in parts]
    def group(i, j):
        w_ref, j0 = (wa_ref, 0) if j < 4 else (wsc_ref, 4)
        return _dot(xb[i], w_ref[:, GROUP * (j - j0):GROUP * (j - j0 + 1)])

    blocks = lambda r: [r[:, LANES * c:LANES * (c + 1)] for c in range(HEADS)]

    conv_block = lambda slot, x, w: _causal_conv(ext_ref.at[slot], x, w)

    for t in range(3):
        for i, rows in enumerate(parts):
            for h, x in enumerate(blocks(group(i, t))):
                y = _silu(conv_block(HEADS * t + h, x, cw_ref[HEADS * t + h]))
                if t == 0:
                    y = _l2_normalize(y) * (HEAD_DIM ** -0.5)
                elif t == 1:
                    y = _l2_normalize(y)
                qkv_ref[HEADS * t + h, rows, :] = y.astype(BF16)
    for i, rows in enumerate(parts):
        sc_x, sc_c = blocks(group(i, 4)), blocks(group(i, 6))
        conv = [conv_block(3 * HEADS + c, sc_c[c] * sc_x[c], csc_ref[c]) for c in range(HEADS)]
        sc_b = blocks(group(i, 5))
        ysc_ref[rows, :] = jnp.concatenate([sc_b[c] * conv[c] for c in range(HEADS)], axis=1).astype(BF16)
    for i, rows in enumerate(parts):
        logits = _dot(xb[i], wa_ref[:, 4 * GROUP:4 * GROUP + LANES])
        lane = lax.broadcasted_iota(jnp.int32, logits.shape, 1)
        gate_ref[rows, :] = jnp.where(lane < HEADS, _sigmoid(logits),
                                      -jnp.exp(alog_ref[...]) * _softplus(logits + dtb_ref[...]))
    for i, rows in enumerate(parts):
        z_ref[rows, :] = group(i, 3)
    tail_ref[...] = ext_ref[:, pl.ds(0, SUBLANES), :]


def _inproj(x2d, head_halo, gain, w_a, w_sc, conv_w, conv_sc, alog_row, dtb_row, seq, tm):
    rows = x2d.shape[0]
    n_tiles = rows // tm
    once = lambda shape: pl.BlockSpec(shape, lambda i: (0,) * len(shape), pipeline_mode=pl.Buffered(1))
    return pl.pallas_call(
        functools.partial(_inproj_kernel, tiles_per_seq=seq // tm),
        grid=(n_tiles,),
        in_specs=[
            pl.BlockSpec((tm, D_MODEL), lambda i: (i, 0)),
            once((1, D_MODEL)),
            once((D_MODEL, 4 * GROUP + LANES)),
            once((D_MODEL, 3 * GROUP)),
            once((3 * HEADS, 4, LANES)),
            once((HEADS, 3, LANES)),
            once((1, LANES)), once((1, LANES)),
            once((N_HALO, SUBLANES, LANES)),
        ],
        out_specs=[
            pl.BlockSpec((3 * HEADS, tm, LANES), lambda i: (0, i, 0)),
            pl.BlockSpec((tm, GDN_WIDTH), lambda i: (i, 0)),
            pl.BlockSpec((tm, SC_WIDTH), lambda i: (i, 0)),
            pl.BlockSpec((tm, LANES), lambda i: (i, 0)),
            pl.BlockSpec((N_HALO, SUBLANES, LANES), lambda i: (0, 0, 0)),
        ],
        out_shape=[
            jax.ShapeDtypeStruct((3 * HEADS, rows, LANES), BF16),
            jax.ShapeDtypeStruct((rows, GDN_WIDTH), F32),
            jax.ShapeDtypeStruct((rows, SC_WIDTH), BF16),
            jax.ShapeDtypeStruct((rows, LANES), F32),
            jax.ShapeDtypeStruct((N_HALO, SUBLANES, LANES), F32),
        ],
        scratch_shapes=[pltpu.VMEM((N_HALO, SUBLANES + tm // min(INPROJ_SPLIT, tm // SUBLANES), LANES), F32)],
        compiler_params=pltpu.CompilerParams(
            dimension_semantics=("arbitrary",), vmem_limit_bytes=VMEM_LIMIT),
        name="inproj",
    )(x2d, gain, w_a, w_sc, conv_w, conv_sc, alog_row, dtb_row, head_halo)


def _pair_diag(x0, x1):
    z = jnp.zeros_like(x0)
    return jnp.concatenate([jnp.concatenate([x0, z], axis=1), jnp.concatenate([z, x1], axis=1)], axis=0)


def _diag_block_inverse(a2s):
    nblk = CHUNK // INV_BLK
    lane = lax.broadcasted_iota(jnp.int32, (INV_BLK, LANES), 1)
    row = lax.broadcasted_iota(jnp.int32, (INV_BLK, LANES), 0)
    grp = (lane & (CHUNK - 1)) // INV_BLK
    base = (lane - (lane & (INV_BLK - 1)))[:SUBLANES]
    eye = ((lane & (INV_BLK - 1)) == row).astype(F32)
    zs, ds = [], []
    for a2 in a2s:
        z = jnp.zeros((INV_BLK, LANES), F32)
        for m in range(nblk):
            z = jnp.where(grp == m, a2[INV_BLK * m:INV_BLK * (m + 1), :], z)
        zs.append([z[SUBLANES * t:SUBLANES * (t + 1)] for t in range(INV_BLK // SUBLANES)])
        ds.append([eye[SUBLANES * t:SUBLANES * (t + 1)] for t in range(INV_BLK // SUBLANES)])
    for j in range(INV_BLK - 1):
        idx = base + j
        for z, d in zip(zs, ds):
            pivot = d[j // SUBLANES][j % SUBLANES:j % SUBLANES + 1, :]
            for t in range(INV_BLK // SUBLANES):
                if SUBLANES * (t + 1) - 1 > j:
                    d[t] = d[t] - jnp.take_along_axis(z[t], idx, axis=1) * pivot
    out = []
    for d in ds:
        dz = jnp.concatenate(d, axis=0)
        out.append(jnp.concatenate([jnp.where(grp == m, dz, 0.0) for m in range(nblk)], axis=0))
    return out


def _chunk_factors(chunks):
    row = lax.broadcasted_iota(jnp.int32, (CHUNK, LANES), 0)
    lane = lax.broadcasted_iota(jnp.int32, (CHUNK, LANES), 1)
    col = lane & (CHUNK - 1)
    left = lane < CHUNK
    pick = lambda a, b: jnp.where(left, a, b)
    incl2, strict2 = row >= col, row > col
    tri2 = incl2.astype(BF16)
    eye2 = (row == col).astype(BF16)
    eye_stack = jnp.concatenate([eye2, eye2], axis=0)
    off_diag = row // INV_BLK != col // INV_BLK

    cums = [_dot(tri2, jnp.concatenate(_split_bf16(c[3]), axis=0)) for c in chunks]
    ys = [jnp.where(lane >= HEADS, cum, c[3]) for cum, c in zip(cums, chunks)]
    rows_all = [_dot_tn(jnp.concatenate(_split_bf16(y), axis=0), eye_stack) for y in ys]

    outs, items = [], []
    for (q16, k16, v16, _), y, rows in zip(chunks, ys, rows_all):
        beta_c = [jnp.broadcast_to(y[:, h:h + 1], (CHUNK, LANES)) for h in range(HEADS)]
        gc_c = [jnp.broadcast_to(y[:, HEADS + h:HEADS + h + 1], (CHUNK, LANES)) for h in range(HEADS)]
        out = dict(u=[], w=[], eg=[], dk=[], gl=[], qk=[])
        for h in range(HEADS):
            gc_last = gc_c[h][CHUNK - 1:CHUNK, :]
            out["eg"].append(jnp.exp(gc_c[h]))
            out["dk"].append(jnp.exp(gc_last - gc_c[h]))
            out["gl"].append(jnp.exp(gc_last))
        outs.append(out)
        for h0 in range(0, HEADS, 2):
            h1 = h0 + 1
            beta_r = pick(rows[h0:h0 + 1, :], rows[h1:h1 + 1, :])
            gc_r = pick(rows[HEADS + h0:HEADS + h0 + 1, :], rows[HEADS + h1:HEADS + h1 + 1, :])
            decay = jnp.where(incl2, jnp.exp(jnp.where(incl2, pick(gc_c[h0], gc_c[h1]) - gc_r, 0.0)), 0.0)
            items.append(dict(out=out, beta_r=beta_r, w_scale=beta_r * jnp.exp(gc_r), decay=decay,
                              beta_c=pick(beta_c[h0], beta_c[h1]),
                              k_cat=jnp.concatenate([k16[h0], k16[h1]], axis=1),
                              q_cat=jnp.concatenate([q16[h0], q16[h1]], axis=1),
                              k_bd=_pair_diag(k16[h0], k16[h1]), v_bd=_pair_diag(v16[h0], v16[h1])))

    halves_diag = lambda x: jnp.concatenate([jnp.where(left, x, 0.0), jnp.where(left, 0.0, x)], axis=0)

    for it in items:
        kq = _dot_nt(jnp.concatenate([it["k_cat"], it["q_cat"]], axis=0), it["k_bd"])
        it["a2"] = jnp.where(strict2, it["beta_c"] * kq[:CHUNK] * it["decay"], 0.0)
        it["out"]["qk"].append(jnp.where(incl2, kq[CHUNK:] * it["decay"], 0.0).astype(BF16))

    for it, xd in zip(items, _diag_block_inverse([it["a2"] for it in items])):
        it["xd"] = xd
    for it in items:
        off_bd = halves_diag(jnp.where(off_diag, it["a2"], 0.0))
        it["n16"] = _dot(it["xd"].astype(BF16), off_bd.astype(BF16)).astype(BF16)
        it["t"] = it["xd"]
    for _ in range(CHUNK // INV_BLK - 1):
        for it in items:
            it["t"] = it["xd"] - _dot(it["n16"], halves_diag(it["t"]).astype(BF16))
    for it in items:
        u01 = _dot((it["t"] * it["beta_r"]).astype(BF16), it["v_bd"])
        w01 = _dot((it["t"] * it["w_scale"]).astype(BF16), it["k_bd"])
        for i in range(2):
            it["out"]["u"].append(u01[:, LANES * i:LANES * (i + 1)])
            it["out"]["w"].append(w01[:, LANES * i:LANES * (i + 1)].astype(BF16))
    return outs


def _chunk_recurrence(streams, need_out):
    work = []
    for q16, k16, f, states in streams:
        work.append(dict(q=q16, k=k16, f=f, s=states, s16=[x.astype(BF16) for x in states],
                         v_new=[], q_s=[], outs=[]))
    for t in work:
        for h in range(HEADS):
            ws = _dot(jnp.concatenate([t["f"]["w"][h], t["q"][h]], axis=0), t["s16"][h])
            t["v_new"].append(t["f"]["u"][h] - ws[:CHUNK])
            t["q_s"].append(ws[CHUNK:])
    for t in work:
        t["vn16"] = [v.astype(BF16) for v in t["v_new"]]
        t["vd16"] = [(v * d).astype(BF16) for v, d in zip(t["v_new"], t["f"]["dk"])]
    for t in work:
        t["kv"] = [_dot_tn(t["k"][h], t["vd16"][h]) for h in range(HEADS)]
    if need_out:
        for t in work:
            for h0 in range(0, HEADS, 2):
                intra = _dot(t["f"]["qk"][h0 // 2], _pair_diag(t["vn16"][h0], t["vn16"][h0 + 1]))
                for i in range(2):
                    t["outs"].append(t["f"]["eg"][h0 + i] * t["q_s"][h0 + i]
                                     + intra[:, LANES * i:LANES * (i + 1)])
    return [(t["outs"], [t["s"][h] * t["f"]["gl"][h] + t["kv"][h] for h in range(HEADS)]) for t in work]


_FACTOR_KEYS = ("u", "w", "eg", "dk", "gl")


def _gdn_kernel(*refs, n_chunks, n_cast):
    q_ref, k_ref, v_ref, gate_ref, s0_ref = refs[:5]
    cast_in, o_ref, cast_out = refs[5:5 + n_cast], refs[5 + n_cast], refs[6 + n_cast:6 + 2 * n_cast]
    s_ref, u_ref, w_ref, eg_ref, dk_ref, gl_ref, qk_ref = refs[6 + 2 * n_cast:]
    f_refs = dict(u=u_ref, w=w_ref, eg=eg_ref, dk=dk_ref, gl=gl_ref)
    for src, dst in zip(cast_in, cast_out):
        dst[...] = src[...].astype(BF16)

    @pl.when(pl.program_id(1) == 0)
    def _sequence_start():
        for b in range(GDN_STREAMS):
            s_ref[b] = s0_ref[...]

    def rows_of(ref, b, c):
        r0 = pl.multiple_of(c * CHUNK, CHUNK)
        return [ref[h, b, pl.ds(r0, CHUNK), :] for h in range(HEADS)]

    def factors(i, carry):
        bcs = [(i * FACTOR_STREAMS + j, c) for j in range(FACTOR_STREAMS) for c in range(n_chunks)]
        fs = _chunk_factors([(rows_of(q_ref, b, c), rows_of(k_ref, b, c), rows_of(v_ref, b, c),
                              gate_ref[b, pl.ds(c * CHUNK, CHUNK), :]) for b, c in bcs])
        for (b, c), f in zip(bcs, fs):
            slot = b * n_chunks + c
            for h in range(HEADS):
                for key in _FACTOR_KEYS:
                    val = f[key][h]
                    f_refs[key][slot, h] = jnp.broadcast_to(val, f_refs[key].shape[2:]) if key == "gl" else val
            for p in range(HEADS // 2):
                qk_ref[slot, p] = f["qk"][p]
        return carry

    def recurrence(i, carry):
        states = [[s_ref[b, h] for h in range(HEADS)] for b in range(GDN_STREAMS)]
        for j in range(RECURRENCE_UNROLL):
            c = i * RECURRENCE_UNROLL + j
            r0 = pl.multiple_of(c * CHUNK, CHUNK)
            streams = []
            for b in range(GDN_STREAMS):
                slot = b * n_chunks + c
                f = {key: [f_refs[key][slot, h] for h in range(HEADS)] for key in _FACTOR_KEYS}
                f["gl"] = [g[:1] for g in f["gl"]]
                f["qk"] = [qk_ref[slot, p] for p in range(HEADS // 2)]
                streams.append((rows_of(q_ref, b, c), rows_of(k_ref, b, c), f, states[b]))
            results = _chunk_recurrence(streams, True)
            states = [st for _, st in results]
            for b, (outs, _) in enumerate(results):
                for h in range(HEADS):
                    o_ref[b, pl.ds(r0, CHUNK), LANES * h:LANES * (h + 1)] = outs[h]
        for b in range(GDN_STREAMS):
            for h in range(HEADS):
                s_ref[b, h] = states[b][h]
        return carry

    lax.fori_loop(0, GDN_STREAMS // FACTOR_STREAMS, factors, 0)
    lax.fori_loop(0, n_chunks // RECURRENCE_UNROLL, recurrence, 0)


def _gdn(qkv, gates, state0, cast_weights, batch, seq, ts):
    ns = seq // ts
    nc = ts // CHUNK
    nb = GDN_STREAMS
    n_steps = (batch // nb) * ns
    blk = lambda t: pl.BlockSpec((HEADS, nb, ts, LANES), lambda b, s: (t, b, s, 0))
    per_head = lambda rows, dtype: pltpu.VMEM((nb * nc, HEADS, rows, LANES), dtype)
    slab_specs = []
    for w in cast_weights:
        slab = w.shape[0] // n_steps
        assert slab * n_steps == w.shape[0] and slab % (2 * SUBLANES) == 0, (w.shape, n_steps)
        slab_specs.append(pl.BlockSpec((slab, w.shape[1]), lambda b, s: (b * ns + s, 0)))
    o, *cast = pl.pallas_call(
        functools.partial(_gdn_kernel, n_chunks=nc, n_cast=len(cast_weights)),
        grid=(batch // nb, ns),
        in_specs=[
            blk(0), blk(1), blk(2),
            pl.BlockSpec((nb, ts, LANES), lambda b, s: (b, s, 0)),
            pl.BlockSpec((HEADS, HEAD_DIM, HEAD_DIM), lambda b, s: (0, 0, 0)),
        ] + slab_specs,
        out_specs=[pl.BlockSpec((nb, ts, GDN_WIDTH), lambda b, s: (b, s, 0))] + slab_specs,
        out_shape=[jax.ShapeDtypeStruct((batch, seq, GDN_WIDTH), F32)]
        + [jax.ShapeDtypeStruct(w.shape, BF16) for w in cast_weights],
        scratch_shapes=[
            pltpu.VMEM((nb, HEADS, HEAD_DIM, HEAD_DIM), F32),
            per_head(CHUNK, F32), per_head(CHUNK, BF16), per_head(CHUNK, F32), per_head(CHUNK, F32),
            per_head(SUBLANES, F32),
            pltpu.VMEM((nb * nc, HEADS // 2, CHUNK, LANES), BF16),
        ],
        compiler_params=pltpu.CompilerParams(
            dimension_semantics=("parallel", "arbitrary"), vmem_limit_bytes=VMEM_LIMIT),
        name="gdn",
    )(*([qkv.reshape(3 * HEADS, batch, seq, LANES)] * 3), gates.reshape(batch, seq, LANES), state0,
      *cast_weights)
    return o.reshape(batch * seq, GDN_WIDTH), cast


def _gdn_meta_kernel(k_ref, v_ref, gate_ref, s_ref):
    pad = CHUNK - N_META
    front = lambda x: jnp.concatenate([jnp.zeros((pad, LANES), x.dtype), x], axis=0)
    k16 = [front(k_ref[h]) for h in range(HEADS)]
    v16 = [front(v_ref[h]) for h in range(HEADS)]
    f, = _chunk_factors([(k16, k16, v16, front(gate_ref[...]))])
    zero = [jnp.zeros((HEAD_DIM, HEAD_DIM), F32)] * HEADS
    (_, st), = _chunk_recurrence([(k16, k16, f, zero)], False)
    for h in range(HEADS):
        s_ref[h] = st[h]


def _gdn_meta(qkv_meta, gates_meta):
    blk = lambda t: pl.BlockSpec((HEADS, N_META, LANES), lambda i: (t, 0, 0))
    return pl.pallas_call(
        _gdn_meta_kernel,
        grid=(1,),
        in_specs=[blk(1), blk(2), pl.BlockSpec((N_META, LANES), lambda i: (0, 0))],
        out_specs=pl.BlockSpec((HEADS, HEAD_DIM, HEAD_DIM), lambda i: (0, 0, 0)),
        out_shape=jax.ShapeDtypeStruct((HEADS, HEAD_DIM, HEAD_DIM), F32),
        name="gdn_meta",
    )(qkv_meta, qkv_meta, gates_meta)


def _post_kernel(x_ref, o_ref, z_ref, ysc_ref, gn_ref, wo_ref, g_post_ref, g_pre_ref, wg_ref, wu_ref,
                 wd_ref, g_fpost_ref, out_ref):
    rows_per = x_ref.shape[0] // POST_SPLIT
    parts = [pl.ds(i * rows_per, rows_per) for i in range(POST_SPLIT)]
    gn = gn_ref[...]
    mixes = []
    for rows in parts:
        heads = []
        for h in range(HEADS):
            sl = slice(HEAD_DIM * h, HEAD_DIM * (h + 1))
            heads.append((_rms_norm(o_ref[rows, sl], gn) * _silu(z_ref[rows, sl])).astype(BF16))
        mixes.append(_dot(jnp.concatenate(heads + [ysc_ref[rows, :]], axis=1), wo_ref[...]))
    h1 = [x_ref[rows, :] + _rms_norm(mix, g_post_ref[...]) for rows, mix in zip(parts, mixes)]
    u = [_rms_norm(h, g_pre_ref[...]).astype(BF16) for h in h1]
    ffn = [jnp.zeros((rows_per, D_MODEL), F32)] * POST_SPLIT
    for lo in range(0, D_FF, FF_TILE):
        sl = slice(lo, min(lo + FF_TILE, D_FF))
        for i in range(POST_SPLIT):
            act = (_silu(_dot(u[i], wg_ref[:, sl])) * _dot(u[i], wu_ref[:, sl])).astype(BF16)
            ffn[i] = ffn[i] + _dot(act, wd_ref[sl, :])
    for i, rows in enumerate(parts):
        out_ref[rows, :] = h1[i] + _rms_norm(ffn[i], g_fpost_ref[...])


def _post(x2d, o, z, ysc, gnorm, w_out, g_post, g_pre, w_gate, w_up, w_down, g_fpost, tm):
    rows = x2d.shape[0]
    row_blk = lambda width: pl.BlockSpec((tm, width), lambda i: (i, 0))
    once = lambda shape: pl.BlockSpec(shape, lambda i: (0, 0), pipeline_mode=pl.Buffered(1))
    return pl.pallas_call(
        _post_kernel,
        grid=(rows // tm,),
        in_specs=[
            row_blk(D_MODEL), row_blk(GDN_WIDTH), row_blk(GDN_WIDTH), row_blk(SC_WIDTH),
            once((1, HEAD_DIM)),
            once((D_MODEL, D_MODEL)), once((1, D_MODEL)), once((1, D_MODEL)),
            once((D_MODEL, D_FF)), once((D_MODEL, D_FF)), once((D_FF, D_MODEL)), once((1, D_MODEL)),
        ],
        out_specs=row_blk(D_MODEL),
        out_shape=jax.ShapeDtypeStruct((rows, D_MODEL), F32),
        compiler_params=pltpu.CompilerParams(
            dimension_semantics=("parallel",), vmem_limit_bytes=VMEM_LIMIT),
        name="post",
    )(x2d, o, z, ysc, gnorm, w_out, g_post, g_pre, w_gate, w_up, w_down, g_fpost)


def kernel(x, meta_tokens, mix_pre_norm, mix_post_norm, ffn_pre_norm, ffn_post_norm, w_in, conv_qkv,
           a_log, dt_bias, gdn_norm, conv_sc, w_out, w_gate, w_up, w_down):
    batch, seq, _ = x.shape
    assert mix_pre_norm.shape[0] == 1, "single-layer block"
    assert batch % GDN_STREAMS == 0
    n_logit = 2 * HEADS
    cut = 4 * GDN_WIDTH
    w_in0 = w_in[0]
    w_a = w_in0.astype(BF16)
    w_sc = w_a[:, cut + n_logit:]
    lane_pad = lambda v: jnp.pad(v.reshape(1, HEADS), ((0, 0), (HEADS, LANES - 2 * HEADS)))
    conv_w = conv_qkv[0].reshape(4, 3 * HEADS, LANES).transpose(1, 0, 2)
    conv_s = conv_sc[0].reshape(3, HEADS, LANES).transpose(1, 0, 2)
    row = lambda v: v[0].reshape(1, -1)
    proj_args = (row(mix_pre_norm), w_a, w_sc, conv_w, conv_s, lane_pad(a_log[0]), lane_pad(dt_bias[0]))

    x2d = x.reshape(batch * seq, D_MODEL)
    no_halo = jnp.zeros((N_HALO, SUBLANES, LANES), F32)
    qkv_m, _, _, gates_m, tail_m = _inproj(meta_tokens, no_halo, *proj_args, N_META, N_META)
    state0 = _gdn_meta(qkv_m, gates_m)
    qkv, z, ysc, gates, _ = _inproj(x2d, tail_m, *proj_args, seq, min(INPROJ_ROWS, seq))
    o, (w_out16, w_gate16, w_up16, w_down16) = _gdn(
        qkv, gates, state0, (w_out[0], w_gate[0], w_up[0], w_down[0]), batch, seq, min(GDN_ROWS, seq))
    out = _post(x2d, o, z, ysc, row(gdn_norm), w_out16, row(mix_post_norm), row(ffn_pre_norm),
                w_gate16, w_up16, w_down16, row(ffn_post_norm), min(POST_ROWS, seq))
    return out.reshape(batch, seq, D_MODEL)
```

```python
import functools

import jax
import jax.numpy as jnp
from jax import lax
from jax.experimental import pallas as pl
from jax.experimental.pallas import tpu as pltpu

F32 = jnp.float32
BF16 = jnp.bfloat16

D_MODEL = 1024
N_META = 16
HEADS = 4
HEAD_DIM = 128
GDN_WIDTH = HEADS * HEAD_DIM
SC_WIDTH = D_MODEL - GDN_WIDTH
CHUNK = 64
D_FF = 2816
EPS = 1e-6
LANES = 128
SUBLANES = 8
GROUP = 4 * LANES
N_HALO = 4 * HEADS
INV_BLK = 16
FF_TILE = 768
INPROJ_ROWS = 1024
GDN_ROWS = 512
POST_ROWS = 512
INPROJ_SPLIT = 8
POST_SPLIT = 2
GDN_STREAMS = 4
RECURRENCE_UNROLL = 8
FACTOR_STREAMS = 2
VMEM_LIMIT = 56 * 1024 * 1024


def _sigmoid(x):
    return 1.0 / (1.0 + jnp.exp(-x))


def _silu(x):
    return x * _sigmoid(x)


def _softplus(x):
    return jnp.maximum(x, 0.0) + jnp.log1p(jnp.exp(-jnp.abs(x)))


def _rms_norm(x, gain):
    return x * lax.rsqrt(jnp.mean(x * x, axis=-1, keepdims=True) + EPS) * gain


def _l2_normalize(x):
    return x * lax.rsqrt(jnp.sum(x * x, axis=-1, keepdims=True) + EPS)


def _dot(a, b):
    return jnp.dot(a, b, preferred_element_type=F32)


def _dot_nt(a, b):
    return lax.dot_general(a, b, (((1,), (1,)), ((), ())), preferred_element_type=F32)


def _dot_tn(a, b):
    return lax.dot_general(a, b, (((0,), (0,)), ((), ())), preferred_element_type=F32)


def _split_bf16(x):
    hi = x.astype(BF16)
    return hi, (x - hi.astype(F32)).astype(BF16)


def _causal_conv(ext_ref, x, w):
    rows, taps = x.shape[0], w.shape[0]
    ext_ref[pl.ds(SUBLANES, rows), :] = x
    acc = x * w[taps - 1:taps]
    for i in range(taps - 1):
        acc = acc + ext_ref[pl.ds(SUBLANES - (taps - 1) + i, rows), :] * w[i:i + 1]
    ext_ref[pl.ds(0, SUBLANES), :] = x[rows - SUBLANES:rows]
    return acc


def _inproj_kernel(x_ref, g_ref, wa_ref, wsc_ref, cw_ref, csc_ref, alog_ref, dtb_ref, head_ref,
                   qkv_ref, z_ref, ysc_ref, gate_ref, tail_ref, ext_ref, *, tiles_per_seq):
    @pl.when(pl.program_id(0) % tiles_per_seq == 0)
    def _sequence_start():
        ext_ref[:, pl.ds(0, SUBLANES), :] = head_ref[...]

    rows_per = x_ref.shape[0] // min(INPROJ_SPLIT, x_ref.shape[0] // SUBLANES)
    parts = [pl.ds(r, rows_per) for r in range(0, x_ref.shape[0], rows_per)]
    xb = [_rms_norm(x_ref[rows, :], g_ref[...]).astype(BF16) for rows in parts]
    def group(i, j):
        w_ref, j0 = (wa_ref, 0) if j < 4 else (wsc_ref, 4)
        return _dot(xb[i], w_ref[:, GROUP * (j - j0):GROUP * (j - j0 + 1)])

    blocks = lambda r: [r[:, LANES * c:LANES * (c + 1)] for c in range(HEADS)]

    conv_block = lambda slot, x, w: _causal_conv(ext_ref.at[slot], x, w)

    for t in range(3):
        for i, rows in enumerate(parts):
            for h, x in enumerate(blocks(group(i, t))):
                y = _silu(conv_block(HEADS * t + h, x, cw_ref[HEADS * t + h]))
                if t == 0:
                    y = _l2_normalize(y) * (HEAD_DIM ** -0.5)
                elif t == 1:
                    y = _l2_normalize(y)
                qkv_ref[HEADS * t + h, rows, :] = y.astype(BF16)
    for i, rows in enumerate(parts):
        sc_x, sc_c = blocks(group(i, 4)), blocks(group(i, 6))
        conv = [conv_block(3 * HEADS + c, sc_c[c] * sc_x[c], csc_ref[c]) for c in range(HEADS)]
        sc_b = blocks(group(i, 5))
        ysc_ref[rows, :] = jnp.concatenate([sc_b[c] * conv[c] for c in range(HEADS)], axis=1).astype(BF16)
    for i, rows in enumerate(parts):
        logits = _dot(xb[i], wa_ref[:, 4 * GROUP:4 * GROUP + LANES])
        lane = lax.broadcasted_iota(jnp.int32, logits.shape, 1)
        gate_ref[rows, :] = jnp.where(lane < HEADS, _sigmoid(logits),
                                      -jnp.exp(alog_ref[...]) * _softplus(logits + dtb_ref[...]))
    for i, rows in enumerate(parts):
        z_ref[rows, :] = group(i, 3)
    tail_ref[...] = ext_ref[:, pl.ds(0, SUBLANES), :]


def _inproj(x2d, head_halo, gain, w_a, w_sc, conv_w, conv_sc, alog_row, dtb_row, seq, tm):
    rows = x2d.shape[0]
    n_tiles = rows // tm
    once = lambda shape: pl.BlockSpec(shape, lambda i: (0,) * len(shape), pipeline_mode=pl.Buffered(1))
    return pl.pallas_call(
        functools.partial(_inproj_kernel, tiles_per_seq=seq // tm),
        grid=(n_tiles,),
        in_specs=[
            pl.BlockSpec((tm, D_MODEL), lambda i: (i, 0)),
            once((1, D_MODEL)),
            once((D_MODEL, 4 * GROUP + LANES)),
            once((D_MODEL, 3 * GROUP)),
            once((3 * HEADS, 4, LANES)),
            once((HEADS, 3, LANES)),
            once((1, LANES)), once((1, LANES)),
            once((N_HALO, SUBLANES, LANES)),
        ],
        out_specs=[
            pl.BlockSpec((3 * HEADS, tm, LANES), lambda i: (0, i, 0)),
            pl.BlockSpec((tm, GDN_WIDTH), lambda i: (i, 0)),
            pl.BlockSpec((tm, SC_WIDTH), lambda i: (i, 0)),
            pl.BlockSpec((tm, LANES), lambda i: (i, 0)),
            pl.BlockSpec((N_HALO, SUBLANES, LANES), lambda i: (0, 0, 0)),
        ],
        out_shape=[
            jax.ShapeDtypeStruct((3 * HEADS, rows, LANES), BF16),
            jax.ShapeDtypeStruct((rows, GDN_WIDTH), F32),
            jax.ShapeDtypeStruct((rows, SC_WIDTH), BF16),
            jax.ShapeDtypeStruct((rows, LANES), F32),
            jax.ShapeDtypeStruct((N_HALO, SUBLANES, LANES), F32),
        ],
        scratch_shapes=[pltpu.VMEM((N_HALO, SUBLANES + tm // min(INPROJ_SPLIT, tm // SUBLANES), LANES), F32)],
        compiler_params=pltpu.CompilerParams(
            dimension_semantics=("arbitrary",), vmem_limit_bytes=VMEM_LIMIT),
        name="inproj",
    )(x2d, gain, w_a, w_sc, conv_w, conv_sc, alog_row, dtb_row, head_halo)


def _pair_diag(x0, x1):
    z = jnp.zeros_like(x0)
    return jnp.concatenate([jnp.concatenate([x0, z], axis=1), jnp.concatenate([z, x1], axis=1)], axis=0)


def _diag_block_inverse(a2s):
    nblk = CHUNK // INV_BLK
    lane = lax.broadcasted_iota(jnp.int32, (INV_BLK, LANES), 1)
    row = lax.broadcasted_iota(jnp.int32, (INV_BLK, LANES), 0)
    grp = (lane & (CHUNK - 1)) // INV_BLK
    base = (lane - (lane & (INV_BLK - 1)))[:SUBLANES]
    eye = ((lane & (INV_BLK - 1)) == row).astype(F32)
    zs, ds = [], []
    for a2 in a2s:
        z = jnp.zeros((INV_BLK, LANES), F32)
        for m in range(nblk):
            z = jnp.where(grp == m, a2[INV_BLK * m:INV_BLK * (m + 1), :], z)
        zs.append([z[SUBLANES * t:SUBLANES * (t + 1)] for t in range(INV_BLK // SUBLANES)])
        ds.append([eye[SUBLANES * t:SUBLANES * (t + 1)] for t in range(INV_BLK // SUBLANES)])
    for j in range(INV_BLK - 1):
        idx = base + j
        for z, d in zip(zs, ds):
            pivot = d[j // SUBLANES][j % SUBLANES:j % SUBLANES + 1, :]
            for t in range(INV_BLK // SUBLANES):
                if SUBLANES * (t + 1) - 1 > j:
                    d[t] = d[t] - jnp.take_along_axis(z[t], idx, axis=1) * pivot
    out = []
    for d in ds:
        dz = jnp.concatenate(d, axis=0)
        out.append(jnp.concatenate([jnp.where(grp == m, dz, 0.0) for m in range(nblk)], axis=0))
    return out


def _chunk_factors(chunks):
    row = lax.broadcasted_iota(jnp.int32, (CHUNK, LANES), 0)
    lane = lax.broadcasted_iota(jnp.int32, (CHUNK, LANES), 1)
    col = lane & (CHUNK - 1)
    left = lane < CHUNK
    pick = lambda a, b: jnp.where(left, a, b)
    incl2, strict2 = row >= col, row > col
    tri2 = incl2.astype(BF16)
    eye2 = (row == col).astype(BF16)
    eye_stack = jnp.concatenate([eye2, eye2], axis=0)
    off_diag = row // INV_BLK != col // INV_BLK

    cums = [_dot(tri2, jnp.concatenate(_split_bf16(c[3]), axis=0)) for c in chunks]
    ys = [jnp.where(lane >= HEADS, cum, c[3]) for cum, c in zip(cums, chunks)]
    rows_all = [_dot_tn(jnp.concatenate(_split_bf16(y), axis=0), eye_stack) for y in ys]

    outs, items = [], []
    for (q16, k16, v16, _), y, rows in zip(chunks, ys, rows_all):
        beta_c = [jnp.broadcast_to(y[:, h:h + 1], (CHUNK, LANES)) for h in range(HEADS)]
        gc_c = [jnp.broadcast_to(y[:, HEADS + h:HEADS + h + 1], (CHUNK, LANES)) for h in range(HEADS)]
        out = dict(u=[], w=[], eg=[], dk=[], gl=[], qk=[])
        for h in range(HEADS):
            gc_last = gc_c[h][CHUNK - 1:CHUNK, :]
            out["eg"].append(jnp.exp(gc_c[h]))
            out["dk"].append(jnp.exp(gc_last - gc_c[h]))
            out["gl"].append(jnp.exp(gc_last))
        outs.append(out)
        for h0 in range(0, HEADS, 2):
            h1 = h0 + 1
            beta_r = pick(rows[h0:h0 + 1, :], rows[h1:h1 + 1, :])
            gc_r = pick(rows[HEADS + h0:HEADS + h0 + 1, :], rows[HEADS + h1:HEADS + h1 + 1, :])
            decay = jnp.where(incl2, jnp.exp(jnp.where(incl2, pick(gc_c[h0], gc_c[h1]) - gc_r, 0.0)), 0.0)
            items.append(dict(out=out, beta_r=beta_r, w_scale=beta_r * jnp.exp(gc_r), decay=decay,
                              beta_c=pick(beta_c[h0], beta_c[h1]),
                              k_cat=jnp.concatenate([k16[h0], k16[h1]], axis=1),
                              q_cat=jnp.concatenate([q16[h0], q16[h1]], axis=1),
                              k_bd=_pair_diag(k16[h0], k16[h1]), v_bd=_pair_diag(v16[h0], v16[h1])))

    halves_diag = lambda x: jnp.concatenate([jnp.where(left, x, 0.0), jnp.where(left, 0.0, x)], axis=0)

    for it in items:
        kq = _dot_nt(jnp.concatenate([it["k_cat"], it["q_cat"]], axis=0), it["k_bd"])
        it["a2"] = jnp.where(strict2, it["beta_c"] * kq[:CHUNK] * it["decay"], 0.0)
        it["out"]["qk"].append(jnp.where(incl2, kq[CHUNK:] * it["decay"], 0.0).astype(BF16))

    for it, xd in zip(items, _diag_block_inverse([it["a2"] for it in items])):
        it["xd"] = xd
    for it in items:
        off_bd = halves_diag(jnp.where(off_diag, it["a2"], 0.0))
        it["n16"] = _dot(it["xd"].astype(BF16), off_bd.astype(BF16)).astype(BF16)
        it["t"] = it["xd"]
    for _ in range(CHUNK // INV_BLK - 1):
        for it in items:
            it["t"] = it["xd"] - _dot(it["n16"], halves_diag(it["t"]).astype(BF16))
    for it in items:
        u01 = _dot((it["t"] * it["beta_r"]).astype(BF16), it["v_bd"])
        w01 = _dot((it["t"] * it["w_scale"]).astype(BF16), it["k_bd"])
        for i in range(2):
            it["out"]["u"].append(u01[:, LANES * i:LANES * (i + 1)])
            it["out"]["w"].append(w01[:, LANES * i:LANES * (i + 1)].astype(BF16))
    return outs


def _chunk_recurrence(streams, need_out):
    work = []
    for q16, k16, f, states in streams:
        work.append(dict(q=q16, k=k16, f=f, s=states, s16=[x.astype(BF16) for x in states],
                         v_new=[], q_s=[], outs=[]))
    for t in work:
        for h in range(HEADS):
            ws = _dot(jnp.concatenate([t["f"]["w"][h], t["q"][h]], axis=0), t["s16"][h])
            t["v_new"].append(t["f"]["u"][h] - ws[:CHUNK])
            t["q_s"].append(ws[CHUNK:])
    for t in work:
        t["vn16"] = [v.astype(BF16) for v in t["v_new"]]
        t["vd16"] = [(v * d).astype(BF16) for v, d in zip(t["v_new"], t["f"]["dk"])]
    for t in work:
        t["kv"] = [_dot_tn(t["k"][h], t["vd16"][h]) for h in range(HEADS)]
    if need_out:
        for t in work:
            for h0 in range(0, HEADS, 2):
                intra = _dot(t["f"]["qk"][h0 // 2], _pair_diag(t["vn16"][h0], t["vn16"][h0 + 1]))
                for i in range(2):
                    t["outs"].append(t["f"]["eg"][h0 + i] * t["q_s"][h0 + i]
                                     + intra[:, LANES * i:LANES * (i + 1)])
    return [(t["outs"], [t["s"][h] * t["f"]["gl"][h] + t["kv"][h] for h in range(HEADS)]) for t in work]


_FACTOR_KEYS = ("u", "w", "eg", "dk", "gl")


def _gdn_kernel(*refs, n_chunks, n_cast):
    q_ref, k_ref, v_ref, gate_ref, s0_ref = refs[:5]
    cast_in, o_ref, cast_out = refs[5:5 + n_cast], refs[5 + n_cast], refs[6 + n_cast:6 + 2 * n_cast]
    s_ref, u_ref, w_ref, eg_ref, dk_ref, gl_ref, qk_ref = refs[6 + 2 * n_cast:]
    f_refs = dict(u=u_ref, w=w_ref, eg=eg_ref, dk=dk_ref, gl=gl_ref)
    for src, dst in zip(cast_in, cast_out):
        dst[...] = src[...].astype(BF16)

    @pl.when(pl.program_id(1) == 0)
    def _sequence_start():
        for b in range(GDN_STREAMS):
            s_ref[b] = s0_ref[...]

    def rows_of(ref, b, c):
        r0 = pl.multiple_of(c * CHUNK, CHUNK)
        return [ref[h, b, pl.ds(r0, CHUNK), :] for h in range(HEADS)]

    def factors(i, carry):
        bcs = [(i * FACTOR_STREAMS + j, c) for j in range(FACTOR_STREAMS) for c in range(n_chunks)]
        fs = _chunk_factors([(rows_of(q_ref, b, c), rows_of(k_ref, b, c), rows_of(v_ref, b, c),
                              gate_ref[b, pl.ds(c * CHUNK, CHUNK), :]) for b, c in bcs])
        for (b, c), f in zip(bcs, fs):
            slot = b * n_chunks + c
            for h in range(HEADS):
                for key in _FACTOR_KEYS:
                    val = f[key][h]
                    f_refs[key][slot, h] = jnp.broadcast_to(val, f_refs[key].shape[2:]) if key == "gl" else val
            for p in range(HEADS // 2):
                qk_ref[slot, p] = f["qk"][p]
        return carry

    def recurrence(i, carry):
        states = [[s_ref[b, h] for h in range(HEADS)] for b in range(GDN_STREAMS)]
        for j in range(RECURRENCE_UNROLL):
            c = i * RECURRENCE_UNROLL + j
            r0 = pl.multiple_of(c * CHUNK, CHUNK)
            streams = []
            for b in range(GDN_STREAMS):
                slot = b * n_chunks + c
                f = {key: [f_refs[key][slot, h] for h in range(HEADS)] for key in _FACTOR_KEYS}
                f["gl"] = [g[:1] for g in f["gl"]]
                f["qk"] = [qk_ref[slot, p] for p in range(HEADS // 2)]
                streams.append((rows_of(q_ref, b, c), rows_of(k_ref, b, c), f, states[b]))
            results = _chunk_recurrence(streams, True)
            states = [st for _, st in results]
            for b, (outs, _) in enumerate(results):
                for h in range(HEADS):
                    o_ref[b, pl.ds(r0, CHUNK), LANES * h:LANES * (h + 1)] = outs[h]
        for b in range(GDN_STREAMS):
            for h in range(HEADS):
                s_ref[b, h] = states[b][h]
        return carry

    lax.fori_loop(0, GDN_STREAMS // FACTOR_STREAMS, factors, 0)
    lax.fori_loop(0, n_chunks // RECURRENCE_UNROLL, recurrence, 0)


def _gdn(qkv, gates, state0, cast_weights, batch, seq, ts):
    ns = seq // ts
    nc = ts // CHUNK
    nb = GDN_STREAMS
    n_steps = (batch // nb) * ns
    blk = lambda t: pl.BlockSpec((HEADS, nb, ts, LANES), lambda b, s: (t, b, s, 0))
    per_head = lambda rows, dtype: pltpu.VMEM((nb * nc, HEADS, rows, LANES), dtype)
    slab_specs = []
    for w in cast_weights:
        slab = w.shape[0] // n_steps
        assert slab * n_steps == w.shape[0] and slab % (2 * SUBLANES) == 0, (w.shape, n_steps)
        slab_specs.append(pl.BlockSpec((slab, w.shape[1]), lambda b, s: (b * ns + s, 0)))
    o, *cast = pl.pallas_call(
        functools.partial(_gdn_kernel, n_chunks=nc, n_cast=len(cast_weights)),
        grid=(batch // nb, ns),
        in_specs=[
            blk(0), blk(1), blk(2),
            pl.BlockSpec((nb, ts, LANES), lambda b, s: (b, s, 0)),
            pl.BlockSpec((HEADS, HEAD_DIM, HEAD_DIM), lambda b, s: (0, 0, 0)),
        ] + slab_specs,
        out_specs=[pl.BlockSpec((nb, ts, GDN_WIDTH), lambda b, s: (b, s, 0))] + slab_specs,
        out_shape=[jax.ShapeDtypeStruct((batch, seq, GDN_WIDTH), F32)]
        + [jax.ShapeDtypeStruct(w.shape, BF16) for w in cast_weights],
        scratch_shapes=[
            pltpu.VMEM((nb, HEADS, HEAD_DIM, HEAD_DIM), F32),
            per_head(CHUNK, F32), per_head(CHUNK, BF16), per_head(CHUNK, F32), per_head(CHUNK, F32),
            per_head(SUBLANES, F32),
            pltpu.VMEM((nb * nc, HEADS // 2, CHUNK, LANES), BF16),
        ],
        compiler_params=pltpu.CompilerParams(
            dimension_semantics=("parallel", "arbitrary"), vmem_limit_bytes=VMEM_LIMIT),
        name="gdn",
    )(*([qkv.reshape(3 * HEADS, batch, seq, LANES)] * 3), gates.reshape(batch, seq, LANES), state0,
      *cast_weights)
    return o.reshape(batch * seq, GDN_WIDTH), cast


def _gdn_meta_kernel(k_ref, v_ref, gate_ref, s_ref):
    pad = CHUNK - N_META
    front = lambda x: jnp.concatenate([jnp.zeros((pad, LANES), x.dtype), x], axis=0)
    k16 = [front(k_ref[h]) for h in range(HEADS)]
    v16 = [front(v_ref[h]) for h in range(HEADS)]
    f, = _chunk_factors([(k16, k16, v16, front(gate_ref[...]))])
    zero = [jnp.zeros((HEAD_DIM, HEAD_DIM), F32)] * HEADS
    (_, st), = _chunk_recurrence([(k16, k16, f, zero)], False)
    for h in range(HEADS):
        s_ref[h] = st[h]


def _gdn_meta(qkv_meta, gates_meta):
    blk = lambda t: pl.BlockSpec((HEADS, N_META, LANES), lambda i: (t, 0, 0))
    return pl.pallas_call(
        _gdn_meta_kernel,
        grid=(1,),
        in_specs=[blk(1), blk(2), pl.BlockSpec((N_META, LANES), lambda i: (0, 0))],
        out_specs=pl.BlockSpec((HEADS, HEAD_DIM, HEAD_DIM), lambda i: (0, 0, 0)),
        out_shape=jax.ShapeDtypeStruct((HEADS, HEAD_DIM, HEAD_DIM), F32),
        name="gdn_meta",
    )(qkv_meta, qkv_meta, gates_meta)


def _post_kernel(x_ref, o_ref, z_ref, ysc_ref, gn_ref, wo_ref, g_post_ref, g_pre_ref, wg_ref, wu_ref,
                 wd_ref, g_fpost_ref, out_ref):
    rows_per = x_ref.shape[0] // POST_SPLIT
    parts = [pl.ds(i * rows_per, rows_per) for i in range(POST_SPLIT)]
    gn = gn_ref[...]
    mixes = []
    for rows in parts:
        heads = []
        for h in range(HEADS):
            sl = slice(HEAD_DIM * h, HEAD_DIM * (h + 1))
            heads.append((_rms_norm(o_ref[rows, sl], gn) * _silu(z_ref[rows, sl])).astype(BF16))
        mixes.append(_dot(jnp.concatenate(heads + [ysc_ref[rows, :]], axis=1), wo_ref[...]))
    h1 = [x_ref[rows, :] + _rms_norm(mix, g_post_ref[...]) for rows, mix in zip(parts, mixes)]
    u = [_rms_norm(h, g_pre_ref[...]).astype(BF16) for h in h1]
    acts = [[] for _ in range(POST_SPLIT)]
    for lo in range(0, D_FF, FF_TILE):
        sl = slice(lo, min(lo + FF_TILE, D_FF))
        for i in range(POST_SPLIT):
            acts[i].append((_silu(_dot(u[i], wg_ref[:, sl])) * _dot(u[i], wu_ref[:, sl])).astype(BF16))
    ffn = [_dot(jnp.concatenate(a, axis=1), wd_ref[...]) for a in acts]
    for i, rows in enumerate(parts):
        out_ref[rows, :] = h1[i] + _rms_norm(ffn[i], g_fpost_ref[...])


def _post(x2d, o, z, ysc, gnorm, w_out, g_post, g_pre, w_gate, w_up, w_down, g_fpost, tm):
    rows = x2d.shape[0]
    row_blk = lambda width: pl.BlockSpec((tm, width), lambda i: (i, 0))
    once = lambda shape: pl.BlockSpec(shape, lambda i: (0, 0), pipeline_mode=pl.Buffered(1))
    return pl.pallas_call(
        _post_kernel,
        grid=(rows // tm,),
        in_specs=[
            row_blk(D_MODEL), row_blk(GDN_WIDTH), row_blk(GDN_WIDTH), row_blk(SC_WIDTH),
            once((1, HEAD_DIM)),
            once((D_MODEL, D_MODEL)), once((1, D_MODEL)), once((1, D_MODEL)),
            once((D_MODEL, D_FF)), once((D_MODEL, D_FF)), once((D_FF, D_MODEL)), once((1, D_MODEL)),
        ],
        out_specs=row_blk(D_MODEL),
        out_shape=jax.ShapeDtypeStruct((rows, D_MODEL), F32),
        compiler_params=pltpu.CompilerParams(
            dimension_semantics=("parallel",), vmem_limit_bytes=VMEM_LIMIT),
        name="post",
    )(x2d, o, z, ysc, gnorm, w_out, g_post, g_pre, w_gate, w_up, w_down, g_fpost)


def kernel(x, meta_tokens, mix_pre_norm, mix_post_norm, ffn_pre_norm, ffn_post_norm, w_in, conv_qkv,
           a_log, dt_bias, gdn_norm, conv_sc, w_out, w_gate, w_up, w_down):
    batch, seq, _ = x.shape
    assert mix_pre_norm.shape[0] == 1, "single-layer block"
    assert batch % GDN_STREAMS == 0
    n_logit = 2 * HEADS
    cut = 4 * GDN_WIDTH
    w_in0 = w_in[0]
    w_a = w_in0.astype(BF16)
    w_sc = w_a[:, cut + n_logit:]
    lane_pad = lambda v: jnp.pad(v.reshape(1, HEADS), ((0, 0), (HEADS, LANES - 2 * HEADS)))
    conv_w = conv_qkv[0].reshape(4, 3 * HEADS, LANES).transpose(1, 0, 2)
    conv_s = conv_sc[0].reshape(3, HEADS, LANES).transpose(1, 0, 2)
    row = lambda v: v[0].reshape(1, -1)
    proj_args = (row(mix_pre_norm), w_a, w_sc, conv_w, conv_s, lane_pad(a_log[0]), lane_pad(dt_bias[0]))

    x2d = x.reshape(batch * seq, D_MODEL)
    no_halo = jnp.zeros((N_HALO, SUBLANES, LANES), F32)
    qkv_m, _, _, gates_m, tail_m = _inproj(meta_tokens, no_halo, *proj_args, N_META, N_META)
    state0 = _gdn_meta(qkv_m, gates_m)
    qkv, z, ysc, gates, _ = _inproj(x2d, tail_m, *proj_args, seq, min(INPROJ_ROWS, seq))
    o, (w_out16, w_gate16, w_up16, w_down16) = _gdn(
        qkv, gates, state0, (w_out[0], w_gate[0], w_up[0], w_down[0]), batch, seq, min(GDN_ROWS, seq))
    out = _post(x2d, o, z, ysc, row(gdn_norm), w_out16, row(mix_post_norm), row(ffn_pre_norm),
                w_gate16, w_up16, w_down16, row(ffn_post_norm), min(POST_ROWS, seq))
    return out.reshape(batch, seq, D_MODEL)
```

```python
import functools

import jax
import jax.numpy as jnp
from jax import lax
from jax.experimental import pallas as pl
from jax.experimental.pallas import tpu as pltpu

F32 = jnp.float32
BF16 = jnp.bfloat16

D_MODEL = 1024
N_META = 16
HEADS = 4
HEAD_DIM = 128
GDN_WIDTH = HEADS * HEAD_DIM
SC_WIDTH = D_MODEL - GDN_WIDTH
CHUNK = 64
D_FF = 2816
EPS = 1e-6
LANES = 128
SUBLANES = 8
GROUP = 4 * LANES
N_HALO = 4 * HEADS
INV_BLK = 16
FF_TILE = 768
INPROJ_ROWS = 1024
GDN_ROWS = 512
POST_ROWS = 512
INPROJ_SPLIT = 8
POST_SPLIT = 2
GDN_STREAMS = 4
RECURRENCE_UNROLL = 4
FACTOR_STREAMS = 4
VMEM_LIMIT = 56 * 1024 * 1024


def _sigmoid(x):
    return 1.0 / (1.0 + jnp.exp(-x))


def _silu(x):
    return x * _sigmoid(x)


def _softplus(x):
    return jnp.maximum(x, 0.0) + jnp.log1p(jnp.exp(-jnp.abs(x)))


def _rms_norm(x, gain):
    return x * lax.rsqrt(jnp.mean(x * x, axis=-1, keepdims=True) + EPS) * gain


def _l2_normalize(x):
    return x * lax.rsqrt(jnp.sum(x * x, axis=-1, keepdims=True) + EPS)


def _dot(a, b):
    return jnp.dot(a, b, preferred_element_type=F32)


def _dot_nt(a, b):
    return lax.dot_general(a, b, (((1,), (1,)), ((), ())), preferred_element_type=F32)


def _dot_tn(a, b):
    return lax.dot_general(a, b, (((0,), (0,)), ((), ())), preferred_element_type=F32)


def _split_bf16(x):
    hi = x.astype(BF16)
    return hi, (x - hi.astype(F32)).astype(BF16)


def _causal_conv(ext_ref, x, w):
    rows, taps = x.shape[0], w.shape[0]
    ext_ref[pl.ds(SUBLANES, rows), :] = x
    acc = x * w[taps - 1:taps]
    for i in range(taps - 1):
        acc = acc + ext_ref[pl.ds(SUBLANES - (taps - 1) + i, rows), :] * w[i:i + 1]
    ext_ref[pl.ds(0, SUBLANES), :] = x[rows - SUBLANES:rows]
    return acc


def _inproj_kernel(x_ref, g_ref, wa_ref, wsc_ref, cw_ref, csc_ref, alog_ref, dtb_ref, head_ref,
                   qkv_ref, z_ref, ysc_ref, gate_ref, tail_ref, ext_ref, *, tiles_per_seq):
    @pl.when(pl.program_id(0) % tiles_per_seq == 0)
    def _sequence_start():
        ext_ref[:, pl.ds(0, SUBLANES), :] = head_ref[...]

    rows_per = x_ref.shape[0] // min(INPROJ_SPLIT, x_ref.shape[0] // SUBLANES)
    parts = [pl.ds(r, rows_per) for r in range(0, x_ref.shape[0], rows_per)]
    xb = [_rms_norm(x_ref[rows, :], g_ref[...]).astype(BF16) for rows in parts]
    def group(i, j):
        w_ref, j0 = (wa_ref, 0) if j < 4 else (wsc_ref, 4)
        return _dot(xb[i], w_ref[:, GROUP * (j - j0):GROUP * (j - j0 + 1)])

    blocks = lambda r: [r[:, LANES * c:LANES * (c + 1)] for c in range(HEADS)]

    conv_block = lambda slot, x, w: _causal_conv(ext_ref.at[slot], x, w)

    for t in range(3):
        for i, rows in enumerate(parts):
            for h, x in enumerate(blocks(group(i, t))):
                y = _silu(conv_block(HEADS * t + h, x, cw_ref[HEADS * t + h]))
                if t == 0:
                    y = _l2_normalize(y) * (HEAD_DIM ** -0.5)
                elif t == 1:
                    y = _l2_normalize(y)
                qkv_ref[HEADS * t + h, rows, :] = y.astype(BF16)
    for i, rows in enumerate(parts):
        sc_x, sc_c = blocks(group(i, 4)), blocks(group(i, 6))
        conv = [conv_block(3 * HEADS + c, sc_c[c] * sc_x[c], csc_ref[c]) for c in range(HEADS)]
        sc_b = blocks(group(i, 5))
        ysc_ref[rows, :] = jnp.concatenate([sc_b[c] * conv[c] for c in range(HEADS)], axis=1).astype(BF16)
    for i, rows in enumerate(parts):
        logits = _dot(xb[i], wa_ref[:, 4 * GROUP:4 * GROUP + LANES])
        lane = lax.broadcasted_iota(jnp.int32, logits.shape, 1)
        gate_ref[rows, :] = jnp.where(lane < HEADS, _sigmoid(logits),
                                      -jnp.exp(alog_ref[...]) * _softplus(logits + dtb_ref[...]))
    for i, rows in enumerate(parts):
        z_ref[rows, :] = group(i, 3)
    tail_ref[...] = ext_ref[:, pl.ds(0, SUBLANES), :]


def _inproj(x2d, head_halo, gain, w_a, w_sc, conv_w, conv_sc, alog_row, dtb_row, seq, tm):
    rows = x2d.shape[0]
    n_tiles = rows // tm
    once = lambda shape: pl.BlockSpec(shape, lambda i: (0,) * len(shape), pipeline_mode=pl.Buffered(1))
    return pl.pallas_call(
        functools.partial(_inproj_kernel, tiles_per_seq=seq // tm),
        grid=(n_tiles,),
        in_specs=[
            pl.BlockSpec((tm, D_MODEL), lambda i: (i, 0)),
            once((1, D_MODEL)),
            once((D_MODEL, 4 * GROUP + LANES)),
            once((D_MODEL, 3 * GROUP)),
            once((3 * HEADS, 4, LANES)),
            once((HEADS, 3, LANES)),
            once((1, LANES)), once((1, LANES)),
            once((N_HALO, SUBLANES, LANES)),
        ],
        out_specs=[
            pl.BlockSpec((3 * HEADS, tm, LANES), lambda i: (0, i, 0)),
            pl.BlockSpec((tm, GDN_WIDTH), lambda i: (i, 0)),
            pl.BlockSpec((tm, SC_WIDTH), lambda i: (i, 0)),
            pl.BlockSpec((tm, LANES), lambda i: (i, 0)),
            pl.BlockSpec((N_HALO, SUBLANES, LANES), lambda i: (0, 0, 0)),
        ],
        out_shape=[
            jax.ShapeDtypeStruct((3 * HEADS, rows, LANES), BF16),
            jax.ShapeDtypeStruct((rows, GDN_WIDTH), F32),
            jax.ShapeDtypeStruct((rows, SC_WIDTH), BF16),
            jax.ShapeDtypeStruct((rows, LANES), F32),
            jax.ShapeDtypeStruct((N_HALO, SUBLANES, LANES), F32),
        ],
        scratch_shapes=[pltpu.VMEM((N_HALO, SUBLANES + tm // min(INPROJ_SPLIT, tm // SUBLANES), LANES), F32)],
        compiler_params=pltpu.CompilerParams(
            dimension_semantics=("arbitrary",), vmem_limit_bytes=VMEM_LIMIT),
        name="inproj",
    )(x2d, gain, w_a, w_sc, conv_w, conv_sc, alog_row, dtb_row, head_halo)


def _pair_diag(x0, x1):
    z = jnp.zeros_like(x0)
    return jnp.concatenate([jnp.concatenate([x0, z], axis=1), jnp.concatenate([z, x1], axis=1)], axis=0)


def _diag_block_inverse(a2s):
    nblk = CHUNK // INV_BLK
    lane = lax.broadcasted_iota(jnp.int32, (INV_BLK, LANES), 1)
    row = lax.broadcasted_iota(jnp.int32, (INV_BLK, LANES), 0)
    grp = (lane & (CHUNK - 1)) // INV_BLK
    base = (lane - (lane & (INV_BLK - 1)))[:SUBLANES]
    eye = ((lane & (INV_BLK - 1)) == row).astype(F32)
    zs, ds = [], []
    for a2 in a2s:
        z = jnp.zeros((INV_BLK, LANES), F32)
        for m in range(nblk):
            z = jnp.where(grp == m, a2[INV_BLK * m:INV_BLK * (m + 1), :], z)
        zs.append([z[SUBLANES * t:SUBLANES * (t + 1)] for t in range(INV_BLK // SUBLANES)])
        ds.append([eye[SUBLANES * t:SUBLANES * (t + 1)] for t in range(INV_BLK // SUBLANES)])
    for j in range(INV_BLK - 1):
        idx = base + j
        for z, d in zip(zs, ds):
            pivot = d[j // SUBLANES][j % SUBLANES:j % SUBLANES + 1, :]
            for t in range(INV_BLK // SUBLANES):
                if SUBLANES * (t + 1) - 1 > j:
                    d[t] = d[t] - jnp.take_along_axis(z[t], idx, axis=1) * pivot
    out = []
    for d in ds:
        dz = jnp.concatenate(d, axis=0)
        out.append(jnp.concatenate([jnp.where(grp == m, dz, 0.0) for m in range(nblk)], axis=0))
    return out


def _chunk_factors(chunks):
    row = lax.broadcasted_iota(jnp.int32, (CHUNK, LANES), 0)
    lane = lax.broadcasted_iota(jnp.int32, (CHUNK, LANES), 1)
    col = lane & (CHUNK - 1)
    left = lane < CHUNK
    pick = lambda a, b: jnp.where(left, a, b)
    incl2, strict2 = row >= col, row > col
    tri2 = incl2.astype(BF16)
    eye2 = (row == col).astype(BF16)
    eye_stack = jnp.concatenate([eye2, eye2], axis=0)
    off_diag = row // INV_BLK != col // INV_BLK

    cums = [_dot(tri2, jnp.concatenate(_split_bf16(c[3]), axis=0)) for c in chunks]
    ys = [jnp.where(lane >= HEADS, cum, c[3]) for cum, c in zip(cums, chunks)]
    rows_all = [_dot_tn(jnp.concatenate(_split_bf16(y), axis=0), eye_stack) for y in ys]

    outs, items = [], []
    for (q16, k16, v16, _), y, rows in zip(chunks, ys, rows_all):
        beta_c = [jnp.broadcast_to(y[:, h:h + 1], (CHUNK, LANES)) for h in range(HEADS)]
        gc_c = [jnp.broadcast_to(y[:, HEADS + h:HEADS + h + 1], (CHUNK, LANES)) for h in range(HEADS)]
        out = dict(u=[], w=[], eg=[], dk=[], gl=[], qk=[])
        for h in range(HEADS):
            gc_last = gc_c[h][CHUNK - 1:CHUNK, :]
            out["eg"].append(jnp.exp(gc_c[h]))
            out["dk"].append(jnp.exp(gc_last - gc_c[h]))
            out["gl"].append(jnp.exp(gc_last))
        outs.append(out)
        for h0 in range(0, HEADS, 2):
            h1 = h0 + 1
            beta_r = pick(rows[h0:h0 + 1, :], rows[h1:h1 + 1, :])
            gc_r = pick(rows[HEADS + h0:HEADS + h0 + 1, :], rows[HEADS + h1:HEADS + h1 + 1, :])
            decay = jnp.where(incl2, jnp.exp(jnp.where(incl2, pick(gc_c[h0], gc_c[h1]) - gc_r, 0.0)), 0.0)
            items.append(dict(out=out, beta_r=beta_r, w_scale=beta_r * jnp.exp(gc_r), decay=decay,
                              beta_c=pick(beta_c[h0], beta_c[h1]),
                              k_cat=jnp.concatenate([k16[h0], k16[h1]], axis=1),
                              q_cat=jnp.concatenate([q16[h0], q16[h1]], axis=1),
                              k_bd=_pair_diag(k16[h0], k16[h1]), v_bd=_pair_diag(v16[h0], v16[h1])))

    halves_diag = lambda x: jnp.concatenate([jnp.where(left, x, 0.0), jnp.where(left, 0.0, x)], axis=0)

    for it in items:
        kq = _dot_nt(jnp.concatenate([it["k_cat"], it["q_cat"]], axis=0), it["k_bd"])
        it["a2"] = jnp.where(strict2, it["beta_c"] * kq[:CHUNK] * it["decay"], 0.0)
        it["out"]["qk"].append(jnp.where(incl2, kq[CHUNK:] * it["decay"], 0.0).astype(BF16))

    for it, xd in zip(items, _diag_block_inverse([it["a2"] for it in items])):
        it["xd"] = xd
    for it in items:
        off_bd = halves_diag(jnp.where(off_diag, it["a2"], 0.0))
        it["n16"] = _dot(it["xd"].astype(BF16), off_bd.astype(BF16)).astype(BF16)
        it["t"] = it["xd"]
    for _ in range(CHUNK // INV_BLK - 1):
        for it in items:
            it["t"] = it["xd"] - _dot(it["n16"], halves_diag(it["t"]).astype(BF16))
    for it in items:
        u01 = _dot((it["t"] * it["beta_r"]).astype(BF16), it["v_bd"])
        w01 = _dot((it["t"] * it["w_scale"]).astype(BF16), it["k_bd"])
        for i in range(2):
            it["out"]["u"].append(u01[:, LANES * i:LANES * (i + 1)])
            it["out"]["w"].append(w01[:, LANES * i:LANES * (i + 1)].astype(BF16))
    return outs


def _chunk_recurrence(streams, need_out):
    work = []
    for q16, k16, f, states in streams:
        work.append(dict(q=q16, k=k16, f=f, s=states, s16=[x.astype(BF16) for x in states],
                         v_new=[], q_s=[], outs=[]))
    for t in work:
        for h in range(HEADS):
            ws = _dot(jnp.concatenate([t["f"]["w"][h], t["q"][h]], axis=0), t["s16"][h])
            t["v_new"].append(t["f"]["u"][h] - ws[:CHUNK])
            t["q_s"].append(ws[CHUNK:])
    for t in work:
        t["vn16"] = [v.astype(BF16) for v in t["v_new"]]
        t["vd16"] = [(v * d).astype(BF16) for v, d in zip(t["v_new"], t["f"]["dk"])]
    for t in work:
        t["kv"] = [_dot_tn(t["k"][h], t["vd16"][h]) for h in range(HEADS)]
    if need_out:
        for t in work:
            for h0 in range(0, HEADS, 2):
                intra = _dot(t["f"]["qk"][h0 // 2], _pair_diag(t["vn16"][h0], t["vn16"][h0 + 1]))
                for i in range(2):
                    t["outs"].append(t["f"]["eg"][h0 + i] * t["q_s"][h0 + i]
                                     + intra[:, LANES * i:LANES * (i + 1)])
    return [(t["outs"], [t["s"][h] * t["f"]["gl"][h] + t["kv"][h] for h in range(HEADS)]) for t in work]


_FACTOR_KEYS = ("u", "w", "eg", "dk", "gl")


def _gdn_kernel(*refs, n_chunks, n_cast):
    q_ref, k_ref, v_ref, gate_ref, s0_ref = refs[:5]
    cast_in, o_ref, cast_out = refs[5:5 + n_cast], refs[5 + n_cast], refs[6 + n_cast:6 + 2 * n_cast]
    s_ref, u_ref, w_ref, eg_ref, dk_ref, gl_ref, qk_ref = refs[6 + 2 * n_cast:]
    f_refs = dict(u=u_ref, w=w_ref, eg=eg_ref, dk=dk_ref, gl=gl_ref)
    for src, dst in zip(cast_in, cast_out):
        dst[...] = src[...].astype(BF16)

    @pl.when(pl.program_id(1) == 0)
    def _sequence_start():
        for b in range(GDN_STREAMS):
            s_ref[b] = s0_ref[...]

    def rows_of(ref, b, c):
        r0 = pl.multiple_of(c * CHUNK, CHUNK)
        return [ref[h, b, pl.ds(r0, CHUNK), :] for h in range(HEADS)]

    def factors(i, carry):
        bcs = [(i * FACTOR_STREAMS + j, c) for j in range(FACTOR_STREAMS) for c in range(n_chunks)]
        fs = _chunk_factors([(rows_of(q_ref, b, c), rows_of(k_ref, b, c), rows_of(v_ref, b, c),
                              gate_ref[b, pl.ds(c * CHUNK, CHUNK), :]) for b, c in bcs])
        for (b, c), f in zip(bcs, fs):
            slot = b * n_chunks + c
            for h in range(HEADS):
                for key in _FACTOR_KEYS:
                    val = f[key][h]
                    f_refs[key][slot, h] = jnp.broadcast_to(val, f_refs[key].shape[2:]) if key == "gl" else val
            for p in range(HEADS // 2):
                qk_ref[slot, p] = f["qk"][p]
        return carry

    def recurrence(i, carry):
        states = [[s_ref[b, h] for h in range(HEADS)] for b in range(GDN_STREAMS)]
        for j in range(RECURRENCE_UNROLL):
            c = i * RECURRENCE_UNROLL + j
            r0 = pl.multiple_of(c * CHUNK, CHUNK)
            streams = []
            for b in range(GDN_STREAMS):
                slot = b * n_chunks + c
                f = {key: [f_refs[key][slot, h] for h in range(HEADS)] for key in _FACTOR_KEYS}
                f["gl"] = [g[:1] for g in f["gl"]]
                f["qk"] = [qk_ref[slot, p] for p in range(HEADS // 2)]
                streams.append((rows_of(q_ref, b, c), rows_of(k_ref, b, c), f, states[b]))
            results = _chunk_recurrence(streams, True)
            states = [st for _, st in results]
            for b, (outs, _) in enumerate(results):
                for h in range(HEADS):
                    o_ref[b, pl.ds(r0, CHUNK), LANES * h:LANES * (h + 1)] = outs[h]
        for b in range(GDN_STREAMS):
            for h in range(HEADS):
                s_ref[b, h] = states[b][h]
        return carry

    lax.fori_loop(0, GDN_STREAMS // FACTOR_STREAMS, factors, 0)
    lax.fori_loop(0, n_chunks // RECURRENCE_UNROLL, recurrence, 0)


def _gdn(qkv, gates, state0, cast_weights, batch, seq, ts):
    ns = seq // ts
    nc = ts // CHUNK
    nb = GDN_STREAMS
    n_steps = (batch // nb) * ns
    blk = lambda t: pl.BlockSpec((HEADS, nb, ts, LANES), lambda b, s: (t, b, s, 0))
    per_head = lambda rows, dtype: pltpu.VMEM((nb * nc, HEADS, rows, LANES), dtype)
    slab_specs = []
    for w in cast_weights:
        slab = w.shape[0] // n_steps
        assert slab * n_steps == w.shape[0] and slab % (2 * SUBLANES) == 0, (w.shape, n_steps)
        slab_specs.append(pl.BlockSpec((slab, w.shape[1]), lambda b, s: (b * ns + s, 0)))
    o, *cast = pl.pallas_call(
        functools.partial(_gdn_kernel, n_chunks=nc, n_cast=len(cast_weights)),
        grid=(batch // nb, ns),
        in_specs=[
            blk(0), blk(1), blk(2),
            pl.BlockSpec((nb, ts, LANES), lambda b, s: (b, s, 0)),
            pl.BlockSpec((HEADS, HEAD_DIM, HEAD_DIM), lambda b, s: (0, 0, 0)),
        ] + slab_specs,
        out_specs=[pl.BlockSpec((nb, ts, GDN_WIDTH), lambda b, s: (b, s, 0))] + slab_specs,
        out_shape=[jax.ShapeDtypeStruct((batch, seq, GDN_WIDTH), F32)]
        + [jax.ShapeDtypeStruct(w.shape, BF16) for w in cast_weights],
        scratch_shapes=[
            pltpu.VMEM((nb, HEADS, HEAD_DIM, HEAD_DIM), F32),
            per_head(CHUNK, F32), per_head(CHUNK, BF16), per_head(CHUNK, F32), per_head(CHUNK, F32),
            per_head(SUBLANES, F32),
            pltpu.VMEM((nb * nc, HEADS // 2, CHUNK, LANES), BF16),
        ],
        compiler_params=pltpu.CompilerParams(
            dimension_semantics=("parallel", "arbitrary"), vmem_limit_bytes=VMEM_LIMIT),
        name="gdn",
    )(*([qkv.reshape(3 * HEADS, batch, seq, LANES)] * 3), gates.reshape(batch, seq, LANES), state0,
      *cast_weights)
    return o.reshape(batch * seq, GDN_WIDTH), cast


def _gdn_meta_kernel(k_ref, v_ref, gate_ref, s_ref):
    pad = CHUNK - N_META
    front = lambda x: jnp.concatenate([jnp.zeros((pad, LANES), x.dtype), x], axis=0)
    k16 = [front(k_ref[h]) for h in range(HEADS)]
    v16 = [front(v_ref[h]) for h in range(HEADS)]
    f, = _chunk_factors([(k16, k16, v16, front(gate_ref[...]))])
    zero = [jnp.zeros((HEAD_DIM, HEAD_DIM), F32)] * HEADS
    (_, st), = _chunk_recurrence([(k16, k16, f, zero)], False)
    for h in range(HEADS):
        s_ref[h] = st[h]


def _gdn_meta(qkv_meta, gates_meta):
    blk = lambda t: pl.BlockSpec((HEADS, N_META, LANES), lambda i: (t, 0, 0))
    return pl.pallas_call(
        _gdn_meta_kernel,
        grid=(1,),
        in_specs=[blk(1), blk(2), pl.BlockSpec((N_META, LANES), lambda i: (0, 0))],
        out_specs=pl.BlockSpec((HEADS, HEAD_DIM, HEAD_DIM), lambda i: (0, 0, 0)),
        out_shape=jax.ShapeDtypeStruct((HEADS, HEAD_DIM, HEAD_DIM), F32),
        name="gdn_meta",
    )(qkv_meta, qkv_meta, gates_meta)


def _post_kernel(x_ref, o_ref, z_ref, ysc_ref, gn_ref, wo_ref, g_post_ref, g_pre_ref, wg_ref, wu_ref,
                 wd_ref, g_fpost_ref, out_ref):
    rows_per = x_ref.shape[0] // POST_SPLIT
    parts = [pl.ds(i * rows_per, rows_per) for i in range(POST_SPLIT)]
    gn = gn_ref[...]
    mixes = []
    for rows in parts:
        heads = []
        for h in range(HEADS):
            sl = slice(HEAD_DIM * h, HEAD_DIM * (h + 1))
            heads.append((_rms_norm(o_ref[rows, sl], gn) * _silu(z_ref[rows, sl])).astype(BF16))
        mixes.append(_dot(jnp.concatenate(heads + [ysc_ref[rows, :]], axis=1), wo_ref[...]))
    h1 = [x_ref[rows, :] + _rms_norm(mix, g_post_ref[...]) for rows, mix in zip(parts, mixes)]
    u = [_rms_norm(h, g_pre_ref[...]).astype(BF16) for h in h1]
    acts = [[] for _ in range(POST_SPLIT)]
    for lo in range(0, D_FF, FF_TILE):
        sl = slice(lo, min(lo + FF_TILE, D_FF))
        for i in range(POST_SPLIT):
            acts[i].append((_silu(_dot(u[i], wg_ref[:, sl])) * _dot(u[i], wu_ref[:, sl])).astype(BF16))
    ffn = [_dot(jnp.concatenate(a, axis=1), wd_ref[...]) for a in acts]
    for i, rows in enumerate(parts):
        out_ref[rows, :] = h1[i] + _rms_norm(ffn[i], g_fpost_ref[...])


def _post(x2d, o, z, ysc, gnorm, w_out, g_post, g_pre, w_gate, w_up, w_down, g_fpost, tm):
    rows = x2d.shape[0]
    row_blk = lambda width: pl.BlockSpec((tm, width), lambda i: (i, 0))
    once = lambda shape: pl.BlockSpec(shape, lambda i: (0, 0), pipeline_mode=pl.Buffered(1))
    return pl.pallas_call(
        _post_kernel,
        grid=(rows // tm,),
        in_specs=[
            row_blk(D_MODEL), row_blk(GDN_WIDTH), row_blk(GDN_WIDTH), row_blk(SC_WIDTH),
            once((1, HEAD_DIM)),
            once((D_MODEL, D_MODEL)), once((1, D_MODEL)), once((1, D_MODEL)),
            once((D_MODEL, D_FF)), once((D_MODEL, D_FF)), once((D_FF, D_MODEL)), once((1, D_MODEL)),
        ],
        out_specs=row_blk(D_MODEL),
        out_shape=jax.ShapeDtypeStruct((rows, D_MODEL), F32),
        compiler_params=pltpu.CompilerParams(
            dimension_semantics=("parallel",), vmem_limit_bytes=VMEM_LIMIT),
        name="post",
    )(x2d, o, z, ysc, gnorm, w_out, g_post, g_pre, w_gate, w_up, w_down, g_fpost)


def kernel(x, meta_tokens, mix_pre_norm, mix_post_norm, ffn_pre_norm, ffn_post_norm, w_in, conv_qkv,
           a_log, dt_bias, gdn_norm, conv_sc, w_out, w_gate, w_up, w_down):
    batch, seq, _ = x.shape
    assert mix_pre_norm.shape[0] == 1, "single-layer block"
    assert batch % GDN_STREAMS == 0
    n_logit = 2 * HEADS
    cut = 4 * GDN_WIDTH
    w_in0 = w_in[0]
    w_a = w_in0.astype(BF16)
    w_sc = w_a[:, cut + n_logit:]
    lane_pad = lambda v: jnp.pad(v.reshape(1, HEADS), ((0, 0), (HEADS, LANES - 2 * HEADS)))
    conv_w = conv_qkv[0].reshape(4, 3 * HEADS, LANES).transpose(1, 0, 2)
    conv_s = conv_sc[0].reshape(3, HEADS, LANES).transpose(1, 0, 2)
    row = lambda v: v[0].reshape(1, -1)
    proj_args = (row(mix_pre_norm), w_a, w_sc, conv_w, conv_s, lane_pad(a_log[0]), lane_pad(dt_bias[0]))

    x2d = x.reshape(batch * seq, D_MODEL)
    no_halo = jnp.zeros((N_HALO, SUBLANES, LANES), F32)
    qkv_m, _, _, gates_m, tail_m = _inproj(meta_tokens, no_halo, *proj_args, N_META, N_META)
    state0 = _gdn_meta(qkv_m, gates_m)
    qkv, z, ysc, gates, _ = _inproj(x2d, tail_m, *proj_args, seq, min(INPROJ_ROWS, seq))
    o, (w_out16, w_gate16, w_up16, w_down16) = _gdn(
        qkv, gates, state0, (w_out[0], w_gate[0], w_up[0], w_down[0]), batch, seq, min(GDN_ROWS, seq))
    out = _post(x2d, o, z, ysc, row(gdn_norm), w_out16, row(mix_post_norm), row(ffn_pre_norm),
                w_gate16, w_up16, w_down16, row(ffn_post_norm), min(POST_ROWS, seq))
    return out.reshape(batch, seq, D_MODEL)
```

```python
import functools

import jax
import jax.numpy as jnp
from jax import lax
from jax.experimental import pallas as pl
from jax.experimental.pallas import tpu as pltpu

F32 = jnp.float32
BF16 = jnp.bfloat16

D_MODEL = 1024
N_META = 16
HEADS = 4
HEAD_DIM = 128
GDN_WIDTH = HEADS * HEAD_DIM
SC_WIDTH = D_MODEL - GDN_WIDTH
CHUNK = 64
D_FF = 2816
EPS = 1e-6
LANES = 128
SUBLANES = 8
GROUP = 4 * LANES
N_HALO = 4 * HEADS
INV_BLK = 16
FF_TILE = 768
INPROJ_ROWS = 1024
GDN_ROWS = 512
POST_ROWS = 512
INPROJ_SPLIT = 8
POST_SPLIT = 2
GDN_STREAMS = 4
RECURRENCE_UNROLL = 8
FACTOR_STREAMS = 2
VMEM_LIMIT = 56 * 1024 * 1024


def _sigmoid(x):
    return 1.0 / (1.0 + jnp.exp(-x))


def _silu(x):
    return x * _sigmoid(x)


def _softplus(x):
    return jnp.maximum(x, 0.0) + jnp.log1p(jnp.exp(-jnp.abs(x)))


def _rms_norm(x, gain):
    return x * lax.rsqrt(jnp.mean(x * x, axis=-1, keepdims=True) + EPS) * gain


def _l2_normalize(x):
    return x * lax.rsqrt(jnp.sum(x * x, axis=-1, keepdims=True) + EPS)


def _dot(a, b):
    return jnp.dot(a, b, preferred_element_type=F32)


def _dot_nt(a, b):
    return lax.dot_general(a, b, (((1,), (1,)), ((), ())), preferred_element_type=F32)


def _dot_tn(a, b):
    return lax.dot_general(a, b, (((0,), (0,)), ((), ())), preferred_element_type=F32)


def _split_bf16(x):
    hi = x.astype(BF16)
    return hi, (x - hi.astype(F32)).astype(BF16)


def _causal_conv(ext_ref, x, w):
    rows, taps = x.shape[0], w.shape[0]
    ext_ref[pl.ds(SUBLANES, rows), :] = x
    acc = x * w[taps - 1:taps]
    for i in range(taps - 1):
        acc = acc + ext_ref[pl.ds(SUBLANES - (taps - 1) + i, rows), :] * w[i:i + 1]
    ext_ref[pl.ds(0, SUBLANES), :] = x[rows - SUBLANES:rows]
    return acc


def _inproj_kernel(x_ref, g_ref, wa_ref, wsc_ref, cw_ref, csc_ref, alog_ref, dtb_ref, head_ref,
                   qkv_ref, z_ref, ysc_ref, gate_ref, tail_ref, ext_ref, *, tiles_per_seq):
    @pl.when(pl.program_id(0) % tiles_per_seq == 0)
    def _sequence_start():
        ext_ref[:, pl.ds(0, SUBLANES), :] = head_ref[...]

    rows_per = x_ref.shape[0] // min(INPROJ_SPLIT, x_ref.shape[0] // SUBLANES)
    parts = [pl.ds(r, rows_per) for r in range(0, x_ref.shape[0], rows_per)]
    xb = [_rms_norm(x_ref[rows, :], g_ref[...]).astype(BF16) for rows in parts]
    def group(i, j):
        w_ref, j0 = (wa_ref, 0) if j < 4 else (wsc_ref, 4)
        return _dot(xb[i], w_ref[:, GROUP * (j - j0):GROUP * (j - j0 + 1)])

    blocks = lambda r: [r[:, LANES * c:LANES * (c + 1)] for c in range(HEADS)]

    conv_block = lambda slot, x, w: _causal_conv(ext_ref.at[slot], x, w)

    for t in range(3):
        for i, rows in enumerate(parts):
            for h, x in enumerate(blocks(group(i, t))):
                y = _silu(conv_block(HEADS * t + h, x, cw_ref[HEADS * t + h]))
                if t == 0:
                    y = _l2_normalize(y) * (HEAD_DIM ** -0.5)
                elif t == 1:
                    y = _l2_normalize(y)
                qkv_ref[HEADS * t + h, rows, :] = y.astype(BF16)
    for i, rows in enumerate(parts):
        sc_x, sc_c = blocks(group(i, 4)), blocks(group(i, 6))
        conv = [conv_block(3 * HEADS + c, sc_c[c] * sc_x[c], csc_ref[c]) for c in range(HEADS)]
        sc_b = blocks(group(i, 5))
        ysc_ref[rows, :] = jnp.concatenate([sc_b[c] * conv[c] for c in range(HEADS)], axis=1).astype(BF16)
    for i, rows in enumerate(parts):
        logits = _dot(xb[i], wa_ref[:, 4 * GROUP:4 * GROUP + LANES])
        lane = lax.broadcasted_iota(jnp.int32, logits.shape, 1)
        gate_ref[rows, :] = jnp.where(lane < HEADS, _sigmoid(logits),
                                      -jnp.exp(alog_ref[...]) * _softplus(logits + dtb_ref[...]))
    for i, rows in enumerate(parts):
        z_ref[rows, :] = group(i, 3)
    tail_ref[...] = ext_ref[:, pl.ds(0, SUBLANES), :]


def _inproj(x2d, head_halo, gain, w_a, w_sc, conv_w, conv_sc, alog_row, dtb_row, seq, tm):
    rows = x2d.shape[0]
    n_tiles = rows // tm
    once = lambda shape: pl.BlockSpec(shape, lambda i: (0,) * len(shape), pipeline_mode=pl.Buffered(1))
    return pl.pallas_call(
        functools.partial(_inproj_kernel, tiles_per_seq=seq // tm),
        grid=(n_tiles,),
        in_specs=[
            pl.BlockSpec((tm, D_MODEL), lambda i: (i, 0)),
            once((1, D_MODEL)),
            once((D_MODEL, 4 * GROUP + LANES)),
            once((D_MODEL, 3 * GROUP)),
            once((3 * HEADS, 4, LANES)),
            once((HEADS, 3, LANES)),
            once((1, LANES)), once((1, LANES)),
            once((N_HALO, SUBLANES, LANES)),
        ],
        out_specs=[
            pl.BlockSpec((3 * HEADS, tm, LANES), lambda i: (0, i, 0)),
            pl.BlockSpec((tm, GDN_WIDTH), lambda i: (i, 0)),
            pl.BlockSpec((tm, SC_WIDTH), lambda i: (i, 0)),
            pl.BlockSpec((tm, LANES), lambda i: (i, 0)),
            pl.BlockSpec((N_HALO, SUBLANES, LANES), lambda i: (0, 0, 0)),
        ],
        out_shape=[
            jax.ShapeDtypeStruct((3 * HEADS, rows, LANES), BF16),
            jax.ShapeDtypeStruct((rows, GDN_WIDTH), F32),
            jax.ShapeDtypeStruct((rows, SC_WIDTH), BF16),
            jax.ShapeDtypeStruct((rows, LANES), F32),
            jax.ShapeDtypeStruct((N_HALO, SUBLANES, LANES), F32),
        ],
        scratch_shapes=[pltpu.VMEM((N_HALO, SUBLANES + tm // min(INPROJ_SPLIT, tm // SUBLANES), LANES), F32)],
        compiler_params=pltpu.CompilerParams(
            dimension_semantics=("arbitrary",), vmem_limit_bytes=VMEM_LIMIT,
            allow_input_fusion=[False, False, True, True, False, False, False, False, False]),
        name="inproj",
    )(x2d, gain, w_a, w_sc, conv_w, conv_sc, alog_row, dtb_row, head_halo)


def _pair_diag(x0, x1):
    z = jnp.zeros_like(x0)
    return jnp.concatenate([jnp.concatenate([x0, z], axis=1), jnp.concatenate([z, x1], axis=1)], axis=0)


def _diag_block_inverse(a2s):
    nblk = CHUNK // INV_BLK
    lane = lax.broadcasted_iota(jnp.int32, (INV_BLK, LANES), 1)
    row = lax.broadcasted_iota(jnp.int32, (INV_BLK, LANES), 0)
    grp = (lane & (CHUNK - 1)) // INV_BLK
    base = (lane - (lane & (INV_BLK - 1)))[:SUBLANES]
    eye = ((lane & (INV_BLK - 1)) == row).astype(F32)
    zs, ds = [], []
    for a2 in a2s:
        z = jnp.zeros((INV_BLK, LANES), F32)
        for m in range(nblk):
            z = jnp.where(grp == m, a2[INV_BLK * m:INV_BLK * (m + 1), :], z)
        zs.append([z[SUBLANES * t:SUBLANES * (t + 1)] for t in range(INV_BLK // SUBLANES)])
        ds.append([eye[SUBLANES * t:SUBLANES * (t + 1)] for t in range(INV_BLK // SUBLANES)])
    for j in range(INV_BLK - 1):
        idx = base + j
        for z, d in zip(zs, ds):
            pivot = d[j // SUBLANES][j % SUBLANES:j % SUBLANES + 1, :]
            for t in range(INV_BLK // SUBLANES):
                if SUBLANES * (t + 1) - 1 > j:
                    d[t] = d[t] - jnp.take_along_axis(z[t], idx, axis=1) * pivot
    out = []
    for d in ds:
        dz = jnp.concatenate(d, axis=0)
        out.append(jnp.concatenate([jnp.where(grp == m, dz, 0.0) for m in range(nblk)], axis=0))
    return out


def _chunk_factors(chunks):
    row = lax.broadcasted_iota(jnp.int32, (CHUNK, LANES), 0)
    lane = lax.broadcasted_iota(jnp.int32, (CHUNK, LANES), 1)
    col = lane & (CHUNK - 1)
    left = lane < CHUNK
    pick = lambda a, b: jnp.where(left, a, b)
    incl2, strict2 = row >= col, row > col
    tri2 = incl2.astype(BF16)
    eye2 = (row == col).astype(BF16)
    eye_stack = jnp.concatenate([eye2, eye2], axis=0)
    off_diag = row // INV_BLK != col // INV_BLK

    cums = [_dot(tri2, jnp.concatenate(_split_bf16(c[3]), axis=0)) for c in chunks]
    ys = [jnp.where(lane >= HEADS, cum, c[3]) for cum, c in zip(cums, chunks)]
    rows_all = [_dot_tn(jnp.concatenate(_split_bf16(y), axis=0), eye_stack) for y in ys]

    outs, items = [], []
    for (q16, k16, v16, _), y, rows in zip(chunks, ys, rows_all):
        beta_c = [jnp.broadcast_to(y[:, h:h + 1], (CHUNK, LANES)) for h in range(HEADS)]
        gc_c = [jnp.broadcast_to(y[:, HEADS + h:HEADS + h + 1], (CHUNK, LANES)) for h in range(HEADS)]
        out = dict(u=[], w=[], eg=[], dk=[], gl=[], qk=[])
        for h in range(HEADS):
            gc_last = gc_c[h][CHUNK - 1:CHUNK, :]
            out["eg"].append(jnp.exp(gc_c[h]))
            out["dk"].append(jnp.exp(gc_last - gc_c[h]))
            out["gl"].append(jnp.exp(gc_last))
        outs.append(out)
        for h0 in range(0, HEADS, 2):
            h1 = h0 + 1
            beta_r = pick(rows[h0:h0 + 1, :], rows[h1:h1 + 1, :])
            gc_r = pick(rows[HEADS + h0:HEADS + h0 + 1, :], rows[HEADS + h1:HEADS + h1 + 1, :])
            decay = jnp.where(incl2, jnp.exp(jnp.where(incl2, pick(gc_c[h0], gc_c[h1]) - gc_r, 0.0)), 0.0)
            items.append(dict(out=out, beta_r=beta_r, w_scale=beta_r * jnp.exp(gc_r), decay=decay,
                              beta_c=pick(beta_c[h0], beta_c[h1]),
                              k_cat=jnp.concatenate([k16[h0], k16[h1]], axis=1),
                              q_cat=jnp.concatenate([q16[h0], q16[h1]], axis=1),
                              k_bd=_pair_diag(k16[h0], k16[h1]), v_bd=_pair_diag(v16[h0], v16[h1])))

    halves_diag = lambda x: jnp.concatenate([jnp.where(left, x, 0.0), jnp.where(left, 0.0, x)], axis=0)

    for it in items:
        kq = _dot_nt(jnp.concatenate([it["k_cat"], it["q_cat"]], axis=0), it["k_bd"])
        it["a2"] = jnp.where(strict2, it["beta_c"] * kq[:CHUNK] * it["decay"], 0.0)
        it["out"]["qk"].append(jnp.where(incl2, kq[CHUNK:] * it["decay"], 0.0).astype(BF16))

    for it, xd in zip(items, _diag_block_inverse([it["a2"] for it in items])):
        it["xd"] = xd
    for it in items:
        off_bd = halves_diag(jnp.where(off_diag, it["a2"], 0.0))
        it["n16"] = _dot(it["xd"].astype(BF16), off_bd.astype(BF16)).astype(BF16)
        it["t"] = it["xd"]
    for _ in range(CHUNK // INV_BLK - 1):
        for it in items:
            it["t"] = it["xd"] - _dot(it["n16"], halves_diag(it["t"]).astype(BF16))
    for it in items:
        u01 = _dot((it["t"] * it["beta_r"]).astype(BF16), it["v_bd"])
        w01 = _dot((it["t"] * it["w_scale"]).astype(BF16), it["k_bd"])
        for i in range(2):
            it["out"]["u"].append(u01[:, LANES * i:LANES * (i + 1)])
            it["out"]["w"].append(w01[:, LANES * i:LANES * (i + 1)].astype(BF16))
    return outs


def _chunk_recurrence(streams, need_out):
    work = []
    for q16, k16, f, states in streams:
        work.append(dict(q=q16, k=k16, f=f, s=states, s16=[x.astype(BF16) for x in states],
                         v_new=[], q_s=[], outs=[]))
    for t in work:
        for h in range(HEADS):
            ws = _dot(jnp.concatenate([t["f"]["w"][h], t["q"][h]], axis=0), t["s16"][h])
            t["v_new"].append(t["f"]["u"][h] - ws[:CHUNK])
            t["q_s"].append(ws[CHUNK:])
    for t in work:
        t["vn16"] = [v.astype(BF16) for v in t["v_new"]]
        t["vd16"] = [(v * d).astype(BF16) for v, d in zip(t["v_new"], t["f"]["dk"])]
    for t in work:
        t["kv"] = [_dot_tn(t["k"][h], t["vd16"][h]) for h in range(HEADS)]
    if need_out:
        for t in work:
            for h0 in range(0, HEADS, 2):
                intra = _dot(t["f"]["qk"][h0 // 2], _pair_diag(t["vn16"][h0], t["vn16"][h0 + 1]))
                for i in range(2):
                    t["outs"].append(t["f"]["eg"][h0 + i] * t["q_s"][h0 + i]
                                     + intra[:, LANES * i:LANES * (i + 1)])
    return [(t["outs"], [t["s"][h] * t["f"]["gl"][h] + t["kv"][h] for h in range(HEADS)]) for t in work]


_FACTOR_KEYS = ("u", "w", "eg", "dk", "gl")


def _gdn_kernel(*refs, n_chunks, n_cast):
    q_ref, k_ref, v_ref, gate_ref, s0_ref = refs[:5]
    cast_in, o_ref, cast_out = refs[5:5 + n_cast], refs[5 + n_cast], refs[6 + n_cast:6 + 2 * n_cast]
    s_ref, u_ref, w_ref, eg_ref, dk_ref, gl_ref, qk_ref = refs[6 + 2 * n_cast:]
    f_refs = dict(u=u_ref, w=w_ref, eg=eg_ref, dk=dk_ref, gl=gl_ref)
    for src, dst in zip(cast_in, cast_out):
        dst[...] = src[...].astype(BF16)

    @pl.when(pl.program_id(1) == 0)
    def _sequence_start():
        for b in range(GDN_STREAMS):
            s_ref[b] = s0_ref[...]

    def rows_of(ref, b, c):
        r0 = pl.multiple_of(c * CHUNK, CHUNK)
        return [ref[h, b, pl.ds(r0, CHUNK), :] for h in range(HEADS)]

    def factors(i, carry):
        bcs = [(i * FACTOR_STREAMS + j, c) for j in range(FACTOR_STREAMS) for c in range(n_chunks)]
        fs = _chunk_factors([(rows_of(q_ref, b, c), rows_of(k_ref, b, c), rows_of(v_ref, b, c),
                              gate_ref[b, pl.ds(c * CHUNK, CHUNK), :]) for b, c in bcs])
        for (b, c), f in zip(bcs, fs):
            slot = b * n_chunks + c
            for h in range(HEADS):
                for key in _FACTOR_KEYS:
                    val = f[key][h]
                    f_refs[key][slot, h] = jnp.broadcast_to(val, f_refs[key].shape[2:]) if key == "gl" else val
            for p in range(HEADS // 2):
                qk_ref[slot, p] = f["qk"][p]
        return carry

    def recurrence(i, carry):
        states = [[s_ref[b, h] for h in range(HEADS)] for b in range(GDN_STREAMS)]
        for j in range(RECURRENCE_UNROLL):
            c = i * RECURRENCE_UNROLL + j
            r0 = pl.multiple_of(c * CHUNK, CHUNK)
            streams = []
            for b in range(GDN_STREAMS):
                slot = b * n_chunks + c
                f = {key: [f_refs[key][slot, h] for h in range(HEADS)] for key in _FACTOR_KEYS}
                f["gl"] = [g[:1] for g in f["gl"]]
                f["qk"] = [qk_ref[slot, p] for p in range(HEADS // 2)]
                streams.append((rows_of(q_ref, b, c), rows_of(k_ref, b, c), f, states[b]))
            results = _chunk_recurrence(streams, True)
            states = [st for _, st in results]
            for b, (outs, _) in enumerate(results):
                for h in range(HEADS):
                    o_ref[b, pl.ds(r0, CHUNK), LANES * h:LANES * (h + 1)] = outs[h]
        for b in range(GDN_STREAMS):
            for h in range(HEADS):
                s_ref[b, h] = states[b][h]
        return carry

    lax.fori_loop(0, GDN_STREAMS // FACTOR_STREAMS, factors, 0)
    lax.fori_loop(0, n_chunks // RECURRENCE_UNROLL, recurrence, 0)


def _gdn(qkv, gates, state0, cast_weights, batch, seq, ts):
    ns = seq // ts
    nc = ts // CHUNK
    nb = GDN_STREAMS
    n_steps = (batch // nb) * ns
    blk = lambda t: pl.BlockSpec((HEADS, nb, ts, LANES), lambda b, s: (t, b, s, 0))
    per_head = lambda rows, dtype: pltpu.VMEM((nb * nc, HEADS, rows, LANES), dtype)
    slab_specs = []
    for w in cast_weights:
        slab = w.shape[0] // n_steps
        assert slab * n_steps == w.shape[0] and slab % (2 * SUBLANES) == 0, (w.shape, n_steps)
        slab_specs.append(pl.BlockSpec((slab, w.shape[1]), lambda b, s: (b * ns + s, 0)))
    o, *cast = pl.pallas_call(
        functools.partial(_gdn_kernel, n_chunks=nc, n_cast=len(cast_weights)),
        grid=(batch // nb, ns),
        in_specs=[
            blk(0), blk(1), blk(2),
            pl.BlockSpec((nb, ts, LANES), lambda b, s: (b, s, 0)),
            pl.BlockSpec((HEADS, HEAD_DIM, HEAD_DIM), lambda b, s: (0, 0, 0)),
        ] + slab_specs,
        out_specs=[pl.BlockSpec((nb, ts, GDN_WIDTH), lambda b, s: (b, s, 0))] + slab_specs,
        out_shape=[jax.ShapeDtypeStruct((batch, seq, GDN_WIDTH), F32)]
        + [jax.ShapeDtypeStruct(w.shape, BF16) for w in cast_weights],
        scratch_shapes=[
            pltpu.VMEM((nb, HEADS, HEAD_DIM, HEAD_DIM), F32),
            per_head(CHUNK, F32), per_head(CHUNK, BF16), per_head(CHUNK, F32), per_head(CHUNK, F32),
            per_head(SUBLANES, F32),
            pltpu.VMEM((nb * nc, HEADS // 2, CHUNK, LANES), BF16),
        ],
        compiler_params=pltpu.CompilerParams(
            dimension_semantics=("parallel", "arbitrary"), vmem_limit_bytes=VMEM_LIMIT),
        name="gdn",
    )(*([qkv.reshape(3 * HEADS, batch, seq, LANES)] * 3), gates.reshape(batch, seq, LANES), state0,
      *cast_weights)
    return o.reshape(batch * seq, GDN_WIDTH), cast


def _gdn_meta_kernel(k_ref, v_ref, gate_ref, s_ref):
    pad = CHUNK - N_META
    front = lambda x: jnp.concatenate([jnp.zeros((pad, LANES), x.dtype), x], axis=0)
    k16 = [front(k_ref[h]) for h in range(HEADS)]
    v16 = [front(v_ref[h]) for h in range(HEADS)]
    f, = _chunk_factors([(k16, k16, v16, front(gate_ref[...]))])
    zero = [jnp.zeros((HEAD_DIM, HEAD_DIM), F32)] * HEADS
    (_, st), = _chunk_recurrence([(k16, k16, f, zero)], False)
    for h in range(HEADS):
        s_ref[h] = st[h]


def _gdn_meta(qkv_meta, gates_meta):
    blk = lambda t: pl.BlockSpec((HEADS, N_META, LANES), lambda i: (t, 0, 0))
    return pl.pallas_call(
        _gdn_meta_kernel,
        grid=(1,),
        in_specs=[blk(1), blk(2), pl.BlockSpec((N_META, LANES), lambda i: (0, 0))],
        out_specs=pl.BlockSpec((HEADS, HEAD_DIM, HEAD_DIM), lambda i: (0, 0, 0)),
        out_shape=jax.ShapeDtypeStruct((HEADS, HEAD_DIM, HEAD_DIM), F32),
        name="gdn_meta",
    )(qkv_meta, qkv_meta, gates_meta)


def _post_kernel(x_ref, o_ref, z_ref, ysc_ref, gn_ref, wo_ref, g_post_ref, g_pre_ref, wg_ref, wu_ref,
                 wd_ref, g_fpost_ref, out_ref):
    rows_per = x_ref.shape[0] // POST_SPLIT
    parts = [pl.ds(i * rows_per, rows_per) for i in range(POST_SPLIT)]
    gn = gn_ref[...]
    mixes = []
    for rows in parts:
        heads = []
        for h in range(HEADS):
            sl = slice(HEAD_DIM * h, HEAD_DIM * (h + 1))
            heads.append((_rms_norm(o_ref[rows, sl], gn) * _silu(z_ref[rows, sl])).astype(BF16))
        mixes.append(_dot(jnp.concatenate(heads + [ysc_ref[rows, :]], axis=1), wo_ref[...]))
    h1 = [x_ref[rows, :] + _rms_norm(mix, g_post_ref[...]) for rows, mix in zip(parts, mixes)]
    u = [_rms_norm(h, g_pre_ref[...]).astype(BF16) for h in h1]
    acts = [[] for _ in range(POST_SPLIT)]
    for lo in range(0, D_FF, FF_TILE):
        sl = slice(lo, min(lo + FF_TILE, D_FF))
        for i in range(POST_SPLIT):
            acts[i].append((_silu(_dot(u[i], wg_ref[:, sl])) * _dot(u[i], wu_ref[:, sl])).astype(BF16))
    ffn = [_dot(jnp.concatenate(a, axis=1), wd_ref[...]) for a in acts]
    for i, rows in enumerate(parts):
        out_ref[rows, :] = h1[i] + _rms_norm(ffn[i], g_fpost_ref[...])


def _post(x2d, o, z, ysc, gnorm, w_out, g_post, g_pre, w_gate, w_up, w_down, g_fpost, tm):
    rows = x2d.shape[0]
    row_blk = lambda width: pl.BlockSpec((tm, width), lambda i: (i, 0))
    once = lambda shape: pl.BlockSpec(shape, lambda i: (0, 0), pipeline_mode=pl.Buffered(1))
    return pl.pallas_call(
        _post_kernel,
        grid=(rows // tm,),
        in_specs=[
            row_blk(D_MODEL), row_blk(GDN_WIDTH), row_blk(GDN_WIDTH), row_blk(SC_WIDTH),
            once((1, HEAD_DIM)),
            once((D_MODEL, D_MODEL)), once((1, D_MODEL)), once((1, D_MODEL)),
            once((D_MODEL, D_FF)), once((D_MODEL, D_FF)), once((D_FF, D_MODEL)), once((1, D_MODEL)),
        ],
        out_specs=row_blk(D_MODEL),
        out_shape=jax.ShapeDtypeStruct((rows, D_MODEL), F32),
        compiler_params=pltpu.CompilerParams(
            dimension_semantics=("parallel",), vmem_limit_bytes=VMEM_LIMIT),
        name="post",
    )(x2d, o, z, ysc, gnorm, w_out, g_post, g_pre, w_gate, w_up, w_down, g_fpost)


def kernel(x, meta_tokens, mix_pre_norm, mix_post_norm, ffn_pre_norm, ffn_post_norm, w_in, conv_qkv,
           a_log, dt_bias, gdn_norm, conv_sc, w_out, w_gate, w_up, w_down):
    batch, seq, _ = x.shape
    assert mix_pre_norm.shape[0] == 1, "single-layer block"
    assert batch % GDN_STREAMS == 0
    n_logit = 2 * HEADS
    cut = 4 * GDN_WIDTH
    w_in0 = w_in[0]
    w_a = w_in0.astype(BF16)
    w_sc = w_a[:, cut + n_logit:]
    lane_pad = lambda v: jnp.pad(v.reshape(1, HEADS), ((0, 0), (HEADS, LANES - 2 * HEADS)))
    conv_w = conv_qkv[0].reshape(4, 3 * HEADS, LANES).transpose(1, 0, 2)
    conv_s = conv_sc[0].reshape(3, HEADS, LANES).transpose(1, 0, 2)
    row = lambda v: v[0].reshape(1, -1)
    proj_args = (row(mix_pre_norm), w_a, w_sc, conv_w, conv_s, lane_pad(a_log[0]), lane_pad(dt_bias[0]))

    x2d = x.reshape(batch * seq, D_MODEL)
    no_halo = jnp.zeros((N_HALO, SUBLANES, LANES), F32)
    qkv_m, _, _, gates_m, tail_m = _inproj(meta_tokens, no_halo, *proj_args, N_META, N_META)
    state0 = _gdn_meta(qkv_m, gates_m)
    qkv, z, ysc, gates, _ = _inproj(x2d, tail_m, *proj_args, seq, min(INPROJ_ROWS, seq))
    o, (w_out16, w_gate16, w_up16, w_down16) = _gdn(
        qkv, gates, state0, (w_out[0], w_gate[0], w_up[0], w_down[0]), batch, seq, min(GDN_ROWS, seq))
    out = _post(x2d, o, z, ysc, row(gdn_norm), w_out16, row(mix_post_norm), row(ffn_pre_norm),
                w_gate16, w_up16, w_down16, row(ffn_post_norm), min(POST_ROWS, seq))
    return out.reshape(batch, seq, D_MODEL)
```

```python
import functools

import jax
import jax.numpy as jnp
from jax import lax
from jax.experimental import pallas as pl
from jax.experimental.pallas import tpu as pltpu

F32 = jnp.float32
BF16 = jnp.bfloat16

D_MODEL = 1024
N_META = 16
HEADS = 4
HEAD_DIM = 128
GDN_WIDTH = HEADS * HEAD_DIM
SC_WIDTH = D_MODEL - GDN_WIDTH
CHUNK = 64
D_FF = 2816
EPS = 1e-6
LANES = 128
SUBLANES = 8
GROUP = 4 * LANES
N_HALO = 4 * HEADS
INV_BLK = 16
FF_TILE = 768
INPROJ_ROWS = 1024
GDN_ROWS = 512
POST_ROWS = 512
INPROJ_SPLIT = 8
POST_SPLIT = 2
GDN_STREAMS = 4
RECURRENCE_UNROLL = 8
FACTOR_STREAMS = 2
VMEM_LIMIT = 56 * 1024 * 1024


def _sigmoid(x):
    return 1.0 / (1.0 + jnp.exp(-x))


def _silu(x):
    return x * _sigmoid(x)


def _softplus(x):
    return jnp.maximum(x, 0.0) + jnp.log1p(jnp.exp(-jnp.abs(x)))


def _rms_norm(x, gain):
    return x * lax.rsqrt(jnp.mean(x * x, axis=-1, keepdims=True) + EPS) * gain


def _l2_normalize(x):
    return x * lax.rsqrt(jnp.sum(x * x, axis=-1, keepdims=True) + EPS)


def _dot(a, b):
    return jnp.dot(a, b, preferred_element_type=F32)


def _dot_nt(a, b):
    return lax.dot_general(a, b, (((1,), (1,)), ((), ())), preferred_element_type=F32)


def _dot_tn(a, b):
    return lax.dot_general(a, b, (((0,), (0,)), ((), ())), preferred_element_type=F32)


def _split_bf16(x):
    hi = x.astype(BF16)
    return hi, (x - hi.astype(F32)).astype(BF16)


def _causal_conv(ext_ref, x, w):
    rows, taps = x.shape[0], w.shape[0]
    ext_ref[pl.ds(SUBLANES, rows), :] = x
    acc = x * w[taps - 1:taps]
    for i in range(taps - 1):
        acc = acc + ext_ref[pl.ds(SUBLANES - (taps - 1) + i, rows), :] * w[i:i + 1]
    ext_ref[pl.ds(0, SUBLANES), :] = x[rows - SUBLANES:rows]
    return acc


def _inproj_kernel(x_ref, g_ref, wa_ref, wsc_ref, cw_ref, csc_ref, alog_ref, dtb_ref, head_ref,
                   qkv_ref, z_ref, ysc_ref, gate_ref, tail_ref, ext_ref, *, tiles_per_seq):
    @pl.when(pl.program_id(0) % tiles_per_seq == 0)
    def _sequence_start():
        ext_ref[:, pl.ds(0, SUBLANES), :] = head_ref[...]

    rows_per = x_ref.shape[0] // min(INPROJ_SPLIT, x_ref.shape[0] // SUBLANES)
    parts = [pl.ds(r, rows_per) for r in range(0, x_ref.shape[0], rows_per)]
    xb = [_rms_norm(x_ref[rows, :], g_ref[...]).astype(BF16) for rows in parts]
    def group(i, j):
        w_ref, j0 = (wa_ref, 0) if j < 4 else (wsc_ref, 4)
        return _dot(xb[i], w_ref[:, GROUP * (j - j0):GROUP * (j - j0 + 1)])

    blocks = lambda r: [r[:, LANES * c:LANES * (c + 1)] for c in range(HEADS)]

    conv_block = lambda slot, x, w: _causal_conv(ext_ref.at[slot], x, w)

    for t in range(3):
        for i, rows in enumerate(parts):
            for h, x in enumerate(blocks(group(i, t))):
                y = _silu(conv_block(HEADS * t + h, x, cw_ref[HEADS * t + h]))
                if t == 0:
                    y = _l2_normalize(y) * (HEAD_DIM ** -0.5)
                elif t == 1:
                    y = _l2_normalize(y)
                qkv_ref[HEADS * t + h, rows, :] = y.astype(BF16)
    for i, rows in enumerate(parts):
        sc_x, sc_c = blocks(group(i, 4)), blocks(group(i, 6))
        conv = [conv_block(3 * HEADS + c, sc_c[c] * sc_x[c], csc_ref[c]) for c in range(HEADS)]
        sc_b = blocks(group(i, 5))
        ysc_ref[rows, :] = jnp.concatenate([sc_b[c] * conv[c] for c in range(HEADS)], axis=1).astype(BF16)
    for i, rows in enumerate(parts):
        logits = _dot(xb[i], wa_ref[:, 4 * GROUP:4 * GROUP + LANES])
        lane = lax.broadcasted_iota(jnp.int32, logits.shape, 1)
        gate_ref[rows, :] = jnp.where(lane < HEADS, _sigmoid(logits),
                                      -jnp.exp(alog_ref[...]) * _softplus(logits + dtb_ref[...]))
    for i, rows in enumerate(parts):
        z_ref[rows, :] = group(i, 3).astype(BF16)
    tail_ref[...] = ext_ref[:, pl.ds(0, SUBLANES), :]


def _inproj(x2d, head_halo, gain, w_a, w_sc, conv_w, conv_sc, alog_row, dtb_row, seq, tm):
    rows = x2d.shape[0]
    n_tiles = rows // tm
    once = lambda shape: pl.BlockSpec(shape, lambda i: (0,) * len(shape), pipeline_mode=pl.Buffered(1))
    return pl.pallas_call(
        functools.partial(_inproj_kernel, tiles_per_seq=seq // tm),
        grid=(n_tiles,),
        in_specs=[
            pl.BlockSpec((tm, D_MODEL), lambda i: (i, 0)),
            once((1, D_MODEL)),
            once((D_MODEL, 4 * GROUP + LANES)),
            once((D_MODEL, 3 * GROUP)),
            once((3 * HEADS, 4, LANES)),
            once((HEADS, 3, LANES)),
            once((1, LANES)), once((1, LANES)),
            once((N_HALO, SUBLANES, LANES)),
        ],
        out_specs=[
            pl.BlockSpec((3 * HEADS, tm, LANES), lambda i: (0, i, 0)),
            pl.BlockSpec((tm, GDN_WIDTH), lambda i: (i, 0)),
            pl.BlockSpec((tm, SC_WIDTH), lambda i: (i, 0)),
            pl.BlockSpec((tm, LANES), lambda i: (i, 0)),
            pl.BlockSpec((N_HALO, SUBLANES, LANES), lambda i: (0, 0, 0)),
        ],
        out_shape=[
            jax.ShapeDtypeStruct((3 * HEADS, rows, LANES), BF16),
            jax.ShapeDtypeStruct((rows, GDN_WIDTH), BF16),
            jax.ShapeDtypeStruct((rows, SC_WIDTH), BF16),
            jax.ShapeDtypeStruct((rows, LANES), F32),
            jax.ShapeDtypeStruct((N_HALO, SUBLANES, LANES), F32),
        ],
        scratch_shapes=[pltpu.VMEM((N_HALO, SUBLANES + tm // min(INPROJ_SPLIT, tm // SUBLANES), LANES), F32)],
        compiler_params=pltpu.CompilerParams(
            dimension_semantics=("arbitrary",), vmem_limit_bytes=VMEM_LIMIT),
        name="inproj",
    )(x2d, gain, w_a, w_sc, conv_w, conv_sc, alog_row, dtb_row, head_halo)


def _pair_diag(x0, x1):
    z = jnp.zeros_like(x0)
    return jnp.concatenate([jnp.concatenate([x0, z], axis=1), jnp.concatenate([z, x1], axis=1)], axis=0)


def _diag_block_inverse(a2s):
    nblk = CHUNK // INV_BLK
    lane = lax.broadcasted_iota(jnp.int32, (INV_BLK, LANES), 1)
    row = lax.broadcasted_iota(jnp.int32, (INV_BLK, LANES), 0)
    grp = (lane & (CHUNK - 1)) // INV_BLK
    base = (lane - (lane & (INV_BLK - 1)))[:SUBLANES]
    eye = ((lane & (INV_BLK - 1)) == row).astype(F32)
    zs, ds = [], []
    for a2 in a2s:
        z = jnp.zeros((INV_BLK, LANES), F32)
        for m in range(nblk):
            z = jnp.where(grp == m, a2[INV_BLK * m:INV_BLK * (m + 1), :], z)
        zs.append([z[SUBLANES * t:SUBLANES * (t + 1)] for t in range(INV_BLK // SUBLANES)])
        ds.append([eye[SUBLANES * t:SUBLANES * (t + 1)] for t in range(INV_BLK // SUBLANES)])
    for j in range(INV_BLK - 1):
        idx = base + j
        for z, d in zip(zs, ds):
            pivot = d[j // SUBLANES][j % SUBLANES:j % SUBLANES + 1, :]
            for t in range(INV_BLK // SUBLANES):
                if SUBLANES * (t + 1) - 1 > j:
                    d[t] = d[t] - jnp.take_along_axis(z[t], idx, axis=1) * pivot
    out = []
    for d in ds:
        dz = jnp.concatenate(d, axis=0)
        out.append(jnp.concatenate([jnp.where(grp == m, dz, 0.0) for m in range(nblk)], axis=0))
    return out


def _chunk_factors(chunks):
    row = lax.broadcasted_iota(jnp.int32, (CHUNK, LANES), 0)
    lane = lax.broadcasted_iota(jnp.int32, (CHUNK, LANES), 1)
    col = lane & (CHUNK - 1)
    left = lane < CHUNK
    pick = lambda a, b: jnp.where(left, a, b)
    incl2, strict2 = row >= col, row > col
    tri2 = incl2.astype(BF16)
    eye2 = (row == col).astype(BF16)
    eye_stack = jnp.concatenate([eye2, eye2], axis=0)
    off_diag = row // INV_BLK != col // INV_BLK

    cums = [_dot(tri2, jnp.concatenate(_split_bf16(c[3]), axis=0)) for c in chunks]
    ys = [jnp.where(lane >= HEADS, cum, c[3]) for cum, c in zip(cums, chunks)]
    rows_all = [_dot_tn(jnp.concatenate(_split_bf16(y), axis=0), eye_stack) for y in ys]

    outs, items = [], []
    for (q16, k16, v16, _), y, rows in zip(chunks, ys, rows_all):
        beta_c = [jnp.broadcast_to(y[:, h:h + 1], (CHUNK, LANES)) for h in range(HEADS)]
        gc_c = [jnp.broadcast_to(y[:, HEADS + h:HEADS + h + 1], (CHUNK, LANES)) for h in range(HEADS)]
        out = dict(u=[], w=[], eg=[], dk=[], gl=[], qk=[])
        for h in range(HEADS):
            gc_last = gc_c[h][CHUNK - 1:CHUNK, :]
            out["eg"].append(jnp.exp(gc_c[h]))
            out["dk"].append(jnp.exp(gc_last - gc_c[h]))
            out["gl"].append(jnp.exp(gc_last))
        outs.append(out)
        for h0 in range(0, HEADS, 2):
            h1 = h0 + 1
            beta_r = pick(rows[h0:h0 + 1, :], rows[h1:h1 + 1, :])
            gc_r = pick(rows[HEADS + h0:HEADS + h0 + 1, :], rows[HEADS + h1:HEADS + h1 + 1, :])
            decay = jnp.where(incl2, jnp.exp(jnp.where(incl2, pick(gc_c[h0], gc_c[h1]) - gc_r, 0.0)), 0.0)
            items.append(dict(out=out, beta_r=beta_r, w_scale=beta_r * jnp.exp(gc_r), decay=decay,
                              beta_c=pick(beta_c[h0], beta_c[h1]),
                              k_cat=jnp.concatenate([k16[h0], k16[h1]], axis=1),
                              q_cat=jnp.concatenate([q16[h0], q16[h1]], axis=1),
                              k_bd=_pair_diag(k16[h0], k16[h1]), v_bd=_pair_diag(v16[h0], v16[h1])))

    halves_diag = lambda x: jnp.concatenate([jnp.where(left, x, 0.0), jnp.where(left, 0.0, x)], axis=0)

    for it in items:
        kq = _dot_nt(jnp.concatenate([it["k_cat"], it["q_cat"]], axis=0), it["k_bd"])
        it["a2"] = jnp.where(strict2, it["beta_c"] * kq[:CHUNK] * it["decay"], 0.0)
        it["out"]["qk"].append(jnp.where(incl2, kq[CHUNK:] * it["decay"], 0.0).astype(BF16))

    for it, xd in zip(items, _diag_block_inverse([it["a2"] for it in items])):
        it["xd"] = xd
    for it in items:
        off_bd = halves_diag(jnp.where(off_diag, it["a2"], 0.0))
        it["n16"] = _dot(it["xd"].astype(BF16), off_bd.astype(BF16)).astype(BF16)
        it["t"] = it["xd"]
    for _ in range(CHUNK // INV_BLK - 1):
        for it in items:
            it["t"] = it["xd"] - _dot(it["n16"], halves_diag(it["t"]).astype(BF16))
    for it in items:
        u01 = _dot((it["t"] * it["beta_r"]).astype(BF16), it["v_bd"])
        w01 = _dot((it["t"] * it["w_scale"]).astype(BF16), it["k_bd"])
        for i in range(2):
            it["out"]["u"].append(u01[:, LANES * i:LANES * (i + 1)])
            it["out"]["w"].append(w01[:, LANES * i:LANES * (i + 1)].astype(BF16))
    return outs


def _chunk_recurrence(streams, need_out):
    work = []
    for q16, k16, f, states in streams:
        work.append(dict(q=q16, k=k16, f=f, s=states, s16=[x.astype(BF16) for x in states],
                         v_new=[], q_s=[], outs=[]))
    for t in work:
        for h in range(HEADS):
            ws = _dot(jnp.concatenate([t["f"]["w"][h], t["q"][h]], axis=0), t["s16"][h])
            t["v_new"].append(t["f"]["u"][h] - ws[:CHUNK])
            t["q_s"].append(ws[CHUNK:])
    for t in work:
        t["vn16"] = [v.astype(BF16) for v in t["v_new"]]
        t["vd16"] = [(v * d).astype(BF16) for v, d in zip(t["v_new"], t["f"]["dk"])]
    for t in work:
        t["kv"] = [_dot_tn(t["k"][h], t["vd16"][h]) for h in range(HEADS)]
    if need_out:
        for t in work:
            for h0 in range(0, HEADS, 2):
                intra = _dot(t["f"]["qk"][h0 // 2], _pair_diag(t["vn16"][h0], t["vn16"][h0 + 1]))
                for i in range(2):
                    t["outs"].append(t["f"]["eg"][h0 + i] * t["q_s"][h0 + i]
                                     + intra[:, LANES * i:LANES * (i + 1)])
    return [(t["outs"], [t["s"][h] * t["f"]["gl"][h] + t["kv"][h] for h in range(HEADS)]) for t in work]


_FACTOR_KEYS = ("u", "w", "eg", "dk", "gl")


def _gdn_kernel(*refs, n_chunks, n_cast):
    q_ref, k_ref, v_ref, gate_ref, s0_ref = refs[:5]
    cast_in, o_ref, cast_out = refs[5:5 + n_cast], refs[5 + n_cast], refs[6 + n_cast:6 + 2 * n_cast]
    s_ref, u_ref, w_ref, eg_ref, dk_ref, gl_ref, qk_ref = refs[6 + 2 * n_cast:]
    f_refs = dict(u=u_ref, w=w_ref, eg=eg_ref, dk=dk_ref, gl=gl_ref)
    for src, dst in zip(cast_in, cast_out):
        dst[...] = src[...].astype(BF16)

    @pl.when(pl.program_id(1) == 0)
    def _sequence_start():
        for b in range(GDN_STREAMS):
            s_ref[b] = s0_ref[...]

    def rows_of(ref, b, c):
        r0 = pl.multiple_of(c * CHUNK, CHUNK)
        return [ref[h, b, pl.ds(r0, CHUNK), :] for h in range(HEADS)]

    def factors(i, carry):
        bcs = [(i * FACTOR_STREAMS + j, c) for j in range(FACTOR_STREAMS) for c in range(n_chunks)]
        fs = _chunk_factors([(rows_of(q_ref, b, c), rows_of(k_ref, b, c), rows_of(v_ref, b, c),
                              gate_ref[b, pl.ds(c * CHUNK, CHUNK), :]) for b, c in bcs])
        for (b, c), f in zip(bcs, fs):
            slot = b * n_chunks + c
            for h in range(HEADS):
                for key in _FACTOR_KEYS:
                    val = f[key][h]
                    f_refs[key][slot, h] = jnp.broadcast_to(val, f_refs[key].shape[2:]) if key == "gl" else val
            for p in range(HEADS // 2):
                qk_ref[slot, p] = f["qk"][p]
        return carry

    def recurrence(i, carry):
        states = [[s_ref[b, h] for h in range(HEADS)] for b in range(GDN_STREAMS)]
        for j in range(RECURRENCE_UNROLL):
            c = i * RECURRENCE_UNROLL + j
            r0 = pl.multiple_of(c * CHUNK, CHUNK)
            streams = []
            for b in range(GDN_STREAMS):
                slot = b * n_chunks + c
                f = {key: [f_refs[key][slot, h] for h in range(HEADS)] for key in _FACTOR_KEYS}
                f["gl"] = [g[:1] for g in f["gl"]]
                f["qk"] = [qk_ref[slot, p] for p in range(HEADS // 2)]
                streams.append((rows_of(q_ref, b, c), rows_of(k_ref, b, c), f, states[b]))
            results = _chunk_recurrence(streams, True)
            states = [st for _, st in results]
            for b, (outs, _) in enumerate(results):
                for h in range(HEADS):
                    o_ref[b, pl.ds(r0, CHUNK), LANES * h:LANES * (h + 1)] = outs[h].astype(BF16)
        for b in range(GDN_STREAMS):
            for h in range(HEADS):
                s_ref[b, h] = states[b][h]
        return carry

    lax.fori_loop(0, GDN_STREAMS // FACTOR_STREAMS, factors, 0)
    lax.fori_loop(0, n_chunks // RECURRENCE_UNROLL, recurrence, 0)


def _gdn(qkv, gates, state0, cast_weights, batch, seq, ts):
    ns = seq // ts
    nc = ts // CHUNK
    nb = GDN_STREAMS
    n_steps = (batch // nb) * ns
    blk = lambda t: pl.BlockSpec((HEADS, nb, ts, LANES), lambda b, s: (t, b, s, 0))
    per_head = lambda rows, dtype: pltpu.VMEM((nb * nc, HEADS, rows, LANES), dtype)
    slab_specs = []
    for w in cast_weights:
        slab = w.shape[0] // n_steps
        assert slab * n_steps == w.shape[0] and slab % (2 * SUBLANES) == 0, (w.shape, n_steps)
        slab_specs.append(pl.BlockSpec((slab, w.shape[1]), lambda b, s: (b * ns + s, 0)))
    o, *cast = pl.pallas_call(
        functools.partial(_gdn_kernel, n_chunks=nc, n_cast=len(cast_weights)),
        grid=(batch // nb, ns),
        in_specs=[
            blk(0), blk(1), blk(2),
            pl.BlockSpec((nb, ts, LANES), lambda b, s: (b, s, 0)),
            pl.BlockSpec((HEADS, HEAD_DIM, HEAD_DIM), lambda b, s: (0, 0, 0)),
        ] + slab_specs,
        out_specs=[pl.BlockSpec((nb, ts, GDN_WIDTH), lambda b, s: (b, s, 0))] + slab_specs,
        out_shape=[jax.ShapeDtypeStruct((batch, seq, GDN_WIDTH), BF16)]
        + [jax.ShapeDtypeStruct(w.shape, BF16) for w in cast_weights],
        scratch_shapes=[
            pltpu.VMEM((nb, HEADS, HEAD_DIM, HEAD_DIM), F32),
            per_head(CHUNK, F32), per_head(CHUNK, BF16), per_head(CHUNK, F32), per_head(CHUNK, F32),
            per_head(SUBLANES, F32),
            pltpu.VMEM((nb * nc, HEADS // 2, CHUNK, LANES), BF16),
        ],
        compiler_params=pltpu.CompilerParams(
            dimension_semantics=("parallel", "arbitrary"), vmem_limit_bytes=VMEM_LIMIT),
        name="gdn",
    )(*([qkv.reshape(3 * HEADS, batch, seq, LANES)] * 3), gates.reshape(batch, seq, LANES), state0,
      *cast_weights)
    return o.reshape(batch * seq, GDN_WIDTH), cast


def _gdn_meta_kernel(k_ref, v_ref, gate_ref, s_ref):
    pad = CHUNK - N_META
    front = lambda x: jnp.concatenate([jnp.zeros((pad, LANES), x.dtype), x], axis=0)
    k16 = [front(k_ref[h]) for h in range(HEADS)]
    v16 = [front(v_ref[h]) for h in range(HEADS)]
    f, = _chunk_factors([(k16, k16, v16, front(gate_ref[...]))])
    zero = [jnp.zeros((HEAD_DIM, HEAD_DIM), F32)] * HEADS
    (_, st), = _chunk_recurrence([(k16, k16, f, zero)], False)
    for h in range(HEADS):
        s_ref[h] = st[h]


def _gdn_meta(qkv_meta, gates_meta):
    blk = lambda t: pl.BlockSpec((HEADS, N_META, LANES), lambda i: (t, 0, 0))
    return pl.pallas_call(
        _gdn_meta_kernel,
        grid=(1,),
        in_specs=[blk(1), blk(2), pl.BlockSpec((N_META, LANES), lambda i: (0, 0))],
        out_specs=pl.BlockSpec((HEADS, HEAD_DIM, HEAD_DIM), lambda i: (0, 0, 0)),
        out_shape=jax.ShapeDtypeStruct((HEADS, HEAD_DIM, HEAD_DIM), F32),
        name="gdn_meta",
    )(qkv_meta, qkv_meta, gates_meta)


def _post_kernel(x_ref, o_ref, z_ref, ysc_ref, gn_ref, wo_ref, g_post_ref, g_pre_ref, wg_ref, wu_ref,
                 wd_ref, g_fpost_ref, out_ref):
    rows_per = x_ref.shape[0] // POST_SPLIT
    parts = [pl.ds(i * rows_per, rows_per) for i in range(POST_SPLIT)]
    gn = gn_ref[...]
    mixes = []
    for rows in parts:
        heads = []
        for h in range(HEADS):
            sl = slice(HEAD_DIM * h, HEAD_DIM * (h + 1))
            heads.append((_rms_norm(o_ref[rows, sl].astype(F32), gn)
                          * _silu(z_ref[rows, sl].astype(F32))).astype(BF16))
        mixes.append(_dot(jnp.concatenate(heads + [ysc_ref[rows, :]], axis=1), wo_ref[...]))
    h1 = [x_ref[rows, :] + _rms_norm(mix, g_post_ref[...]) for rows, mix in zip(parts, mixes)]
    u = [_rms_norm(h, g_pre_ref[...]).astype(BF16) for h in h1]
    acts = [[] for _ in range(POST_SPLIT)]
    for lo in range(0, D_FF, FF_TILE):
        sl = slice(lo, min(lo + FF_TILE, D_FF))
        for i in range(POST_SPLIT):
            acts[i].append((_silu(_dot(u[i], wg_ref[:, sl])) * _dot(u[i], wu_ref[:, sl])).astype(BF16))
    ffn = [_dot(jnp.concatenate(a, axis=1), wd_ref[...]) for a in acts]
    for i, rows in enumerate(parts):
        out_ref[rows, :] = h1[i] + _rms_norm(ffn[i], g_fpost_ref[...])


def _post(x2d, o, z, ysc, gnorm, w_out, g_post, g_pre, w_gate, w_up, w_down, g_fpost, tm):
    rows = x2d.shape[0]
    row_blk = lambda width: pl.BlockSpec((tm, width), lambda i: (i, 0))
    once = lambda shape: pl.BlockSpec(shape, lambda i: (0, 0), pipeline_mode=pl.Buffered(1))
    return pl.pallas_call(
        _post_kernel,
        grid=(rows // tm,),
        in_specs=[
            row_blk(D_MODEL), row_blk(GDN_WIDTH), row_blk(GDN_WIDTH), row_blk(SC_WIDTH),
            once((1, HEAD_DIM)),
            once((D_MODEL, D_MODEL)), once((1, D_MODEL)), once((1, D_MODEL)),
            once((D_MODEL, D_FF)), once((D_MODEL, D_FF)), once((D_FF, D_MODEL)), once((1, D_MODEL)),
        ],
        out_specs=row_blk(D_MODEL),
        out_shape=jax.ShapeDtypeStruct((rows, D_MODEL), F32),
        compiler_params=pltpu.CompilerParams(
            dimension_semantics=("parallel",), vmem_limit_bytes=VMEM_LIMIT),
        name="post",
    )(x2d, o, z, ysc, gnorm, w_out, g_post, g_pre, w_gate, w_up, w_down, g_fpost)


def kernel(x, meta_tokens, mix_pre_norm, mix_post_norm, ffn_pre_norm, ffn_post_norm, w_in, conv_qkv,
           a_log, dt_bias, gdn_norm, conv_sc, w_out, w_gate, w_up, w_down):
    batch, seq, _ = x.shape
    assert mix_pre_norm.shape[0] == 1, "single-layer block"
    assert batch % GDN_STREAMS == 0
    n_logit = 2 * HEADS
    cut = 4 * GDN_WIDTH
    w_in0 = w_in[0]
    w_a = w_in0.astype(BF16)
    w_sc = w_a[:, cut + n_logit:]
    lane_pad = lambda v: jnp.pad(v.reshape(1, HEADS), ((0, 0), (HEADS, LANES - 2 * HEADS)))
    conv_w = conv_qkv[0].reshape(4, 3 * HEADS, LANES).transpose(1, 0, 2)
    conv_s = conv_sc[0].reshape(3, HEADS, LANES).transpose(1, 0, 2)
    row = lambda v: v[0].reshape(1, -1)
    proj_args = (row(mix_pre_norm), w_a, w_sc, conv_w, conv_s, lane_pad(a_log[0]), lane_pad(dt_bias[0]))

    x2d = x.reshape(batch * seq, D_MODEL)
    no_halo = jnp.zeros((N_HALO, SUBLANES, LANES), F32)
    qkv_m, _, _, gates_m, tail_m = _inproj(meta_tokens, no_halo, *proj_args, N_META, N_META)
    state0 = _gdn_meta(qkv_m, gates_m)
    qkv, z, ysc, gates, _ = _inproj(x2d, tail_m, *proj_args, seq, min(INPROJ_ROWS, seq))
    o, (w_out16, w_gate16, w_up16, w_down16) = _gdn(
        qkv, gates, state0, (w_out[0], w_gate[0], w_up[0], w_down[0]), batch, seq, min(GDN_ROWS, seq))
    out = _post(x2d, o, z, ysc, row(gdn_norm), w_out16, row(mix_post_norm), row(ffn_pre_norm),
                w_gate16, w_up16, w_down16, row(ffn_post_norm), min(POST_ROWS, seq))
    return out.reshape(batch, seq, D_MODEL)
```
